```python
import math
import jax, jax.numpy as jnp
from jax import lax
import numpy as np

D_MODEL = 1024
BATCH = 16
SEQ = 2048
DEPTH = 2

F_GROUPS = 4
F_GROUP_DIM = 64
F_WIDTH = F_GROUPS * F_GROUP_DIM
N_HEADS = 6
Q_LORA = 256
KV_LORA = 256
QK_NOPE = 128
QK_ROPE = 64
V_DIM = 128
QK_DIM = QK_NOPE + QK_ROPE
A_WIDTH = N_HEADS * V_DIM
MIX_WIDTH = F_WIDTH + A_WIDTH
IN_WIDTH = F_WIDTH + Q_LORA + KV_LORA + QK_ROPE
ROPE_BASE = 10000.0
Q_BLOCK = 128
MAX_POS_OFFSET = 1024
D_FF = 4 * D_MODEL
EPS = 1e-6

kernel_name = "hybrid_fnet_mla_encoder"


def rms_norm(x, g):
    xf = x.astype(jnp.float32)
    y = xf * lax.rsqrt(jnp.mean(xf * xf, axis=-1, keepdims=True) + EPS)
    return (y * g.astype(jnp.float32)).astype(x.dtype)


def rotary_tables(positions):
    half = QK_ROPE // 2
    inv_freq = ROPE_BASE ** (-jnp.arange(half, dtype=jnp.float32) / half)
    ang = positions.astype(jnp.float32)[..., None] * inv_freq
    return jnp.cos(ang)[:, :, None, :], jnp.sin(ang)[:, :, None, :]


def apply_rotary(x, cos, sin):
    xf = x.astype(jnp.float32)
    x1, x2 = jnp.split(xf, 2, axis=-1)
    out = jnp.concatenate([x1 * cos - x2 * sin, x2 * cos + x1 * sin], axis=-1)
    return out.astype(x.dtype)


def fourier_mixer(u, w_fourier):
    b, s, _ = u.shape
    ug = u.reshape(b, s, F_GROUPS, F_GROUP_DIM).astype(jnp.float32)
    fr = jnp.real(jnp.fft.fft2(ug, axes=(1, 3), norm='ortho'))
    y = jnp.einsum('bsgc,gcd->bsgd', fr, w_fourier.astype(jnp.float32))
    return y.reshape(b, s, F_WIDTH).astype(u.dtype)


def dense_attention(q, k, v):
    b, s, h, dk = q.shape
    dv = v.shape[-1]
    nblk = s // Q_BLOCK
    scale = 1.0 / math.sqrt(dk)
    kf = jnp.transpose(k, (0, 2, 1, 3)).astype(jnp.float32)
    vf = jnp.transpose(v, (0, 2, 1, 3)).astype(jnp.float32)
    qb = jnp.transpose(q, (0, 2, 1, 3)).reshape(b, h, nblk, Q_BLOCK, dk)
    qb = jnp.transpose(qb, (2, 0, 1, 3, 4))

    def one_block(qblk):
        scores = jnp.einsum('bhqd,bhkd->bhqk', qblk.astype(jnp.float32), kf) * scale
        probs = jax.nn.softmax(scores, axis=-1)
        return jnp.einsum('bhqk,bhkd->bhqd', probs, vf)

    o = lax.map(one_block, qb)
    o = jnp.transpose(o, (1, 0, 3, 2, 4)).reshape(b, s, h, dv)
    return o.astype(q.dtype)


def mla_mixer(c_q, c_kv, k_pe, cos, sin, q_a_g, w_q_up, kv_a_g, w_kv_up, q_norm_g, k_norm_g):
    b, s, _ = c_q.shape
    q = (rms_norm(c_q, q_a_g) @ w_q_up).reshape(b, s, N_HEADS, QK_DIM)
    kv = (rms_norm(c_kv, kv_a_g) @ w_kv_up).reshape(b, s, N_HEADS, QK_NOPE + V_DIM)
    k_nope, v = kv[..., :QK_NOPE], kv[..., QK_NOPE:]
    k_pe_h = jnp.broadcast_to(k_pe[:, :, None, :], (b, s, N_HEADS, QK_ROPE))
    k = jnp.concatenate([k_nope, k_pe_h], axis=-1)
    q = rms_norm(q, q_norm_g)
    k = rms_norm(k, k_norm_g)
    q = jnp.concatenate([q[..., :QK_NOPE], apply_rotary(q[..., QK_NOPE:], cos, sin)], axis=-1)
    k = jnp.concatenate([k[..., :QK_NOPE], apply_rotary(k[..., QK_NOPE:], cos, sin)], axis=-1)
    o = dense_attention(q, k, v)
    return o.reshape(b, s, A_WIDTH)


def setup_inputs(seed: int = 0) -> dict:
    key = jax.random.key(seed)
    ks = jax.random.split(key, 18)
    f32 = jnp.float32

    def w(k, shape, fan_in):
        return jax.random.normal(k, shape, f32) * (fan_in ** -0.5)

    def gain(k, shape):
        return 1.0 + 0.05 * jax.random.normal(k, shape, f32)

    x = jax.random.normal(ks[0], (BATCH, SEQ, D_MODEL), f32)
    offsets = jax.random.randint(ks[1], (BATCH, 1), 0, MAX_POS_OFFSET, dtype=jnp.int32)
    positions = (jnp.arange(SEQ, dtype=jnp.int32)[None, :] + offsets).astype(jnp.int32)
    return {
        'x': x,
        'positions': positions,
        'attn_norm_g': gain(ks[2], (DEPTH, D_MODEL)),
        'w_in': w(ks[3], (DEPTH, D_MODEL, IN_WIDTH), D_MODEL),
        'w_fourier': w(ks[4], (DEPTH, F_GROUPS, F_GROUP_DIM, F_GROUP_DIM), F_GROUP_DIM),
        'q_a_g': gain(ks[5], (DEPTH, Q_LORA)),
        'w_q_up': w(ks[6], (DEPTH, Q_LORA, N_HEADS * QK_DIM), Q_LORA),
        'kv_a_g': gain(ks[7], (DEPTH, KV_LORA)),
        'w_kv_up': w(ks[8], (DEPTH, KV_LORA, N_HEADS * (QK_NOPE + V_DIM)), KV_LORA),
        'q_norm_g': gain(ks[9], (DEPTH, QK_DIM)),
        'k_norm_g': gain(ks[10], (DEPTH, QK_DIM)),
        'fourier_out_g': gain(ks[11], (DEPTH, F_WIDTH)),
        'attn_out_g': gain(ks[12], (DEPTH, A_WIDTH)),
        'w_out': w(ks[13], (DEPTH, MIX_WIDTH, D_MODEL), MIX_WIDTH),
        'mlp_norm_g': gain(ks[14], (DEPTH, D_MODEL)),
        'w_mlp_in': w(ks[15], (DEPTH, D_MODEL, D_FF), D_MODEL),
        'w_mlp_out': w(ks[16], (DEPTH, D_FF, D_MODEL), D_FF),
    }


def reference(x, positions, attn_norm_g, w_in, w_fourier, q_a_g, w_q_up, kv_a_g, w_kv_up,
              q_norm_g, k_norm_g, fourier_out_g, attn_out_g, w_out, mlp_norm_g,
              w_mlp_in, w_mlp_out):
    cos, sin = rotary_tables(positions)
    splits = [F_WIDTH, F_WIDTH + Q_LORA, F_WIDTH + Q_LORA + KV_LORA]
    for l in range(DEPTH):
        h = rms_norm(x, attn_norm_g[l])
        z = h @ w_in[l]
        z_f, c_q, c_kv, k_pe = jnp.split(z, splits, axis=-1)
        y_f = rms_norm(fourier_mixer(z_f, w_fourier[l]), fourier_out_g[l])
        y_a = rms_norm(mla_mixer(c_q, c_kv, k_pe, cos, sin, q_a_g[l], w_q_up[l], kv_a_g[l],
                                 w_kv_up[l], q_norm_g[l], k_norm_g[l]), attn_out_g[l])
        x = x + jnp.concatenate([y_f, y_a], axis=-1) @ w_out[l]
        hm = rms_norm(x, mlp_norm_g[l]) @ w_mlp_in[l]
        x = x + jnp.square(jax.nn.relu(hm)) @ w_mlp_out[l]
    return x
```

```python
import functools
import math

import numpy as np
import jax
import jax.numpy as jnp
from jax import lax
from jax.experimental import pallas as pl
from jax.experimental.pallas import tpu as pltpu

D_MODEL = 1024
F_GROUPS = 4
F_GROUP_DIM = 64
F_WIDTH = F_GROUPS * F_GROUP_DIM
N_HEADS = 6
Q_LORA = 256
KV_LORA = 256
QK_NOPE = 128
QK_ROPE = 64
V_DIM = 128
QK_DIM = QK_NOPE + QK_ROPE
A_WIDTH = N_HEADS * V_DIM
ROPE_BASE = 10000.0
D_FF = 4 * D_MODEL
EPS = 1e-6

LANES = 128
HEAD_PAD = 2 * LANES
HALF = QK_ROPE // 2
Z_WIDTH = F_WIDTH + Q_LORA + KV_LORA + LANES
VMEM_LIMIT = 56 * 1024 * 1024

ROW_TILE = 512
Q_TILE = 512
FF_CHUNK = 1024

BF16 = jnp.bfloat16
F32 = jnp.float32


def _rope_tile_cols(start):
    cols = np.full((LANES,), -1, np.int64)
    cols[0:HALF] = start + np.arange(HALF)
    cols[2 * HALF:3 * HALF] = start + HALF + np.arange(HALF)
    return cols


def _gather_cols(w, cols):
    taken = jnp.take(w, jnp.asarray(np.maximum(cols, 0)), axis=-1)
    return jnp.where(jnp.asarray(cols >= 0), taken, jnp.zeros((), w.dtype))


def _rms(x, g):
    return x * lax.rsqrt(jnp.mean(x * x, axis=-1, keepdims=True) + EPS) * g


def _rotate(t, cos_t, sin_t):
    return t * cos_t + pltpu.roll(t, 2 * HALF, axis=1) * sin_t


def _fold_kernel(cc_ref, sc_ref, w_ref, out_ref):
    out_ref[...] = jnp.zeros_like(out_ref)
    for g in range(F_GROUPS):
        w = w_ref[0, g]
        a = jnp.dot(cc_ref[...], w, preferred_element_type=F32, precision=lax.Precision.HIGHEST)
        b = jnp.dot(sc_ref[...], w, preferred_element_type=F32, precision=lax.Precision.HIGHEST)
        lo, hi = g * F_GROUP_DIM, (g + 1) * F_GROUP_DIM
        out_ref[0, lo:hi, lo:hi] = a
        out_ref[0, lo:hi, F_WIDTH + lo:F_WIDTH + hi] = b


def _fold_fourier_weights(w_fourier, seq):
    depth = w_fourier.shape[0]
    c = np.arange(F_GROUP_DIM)
    ang = 2.0 * np.pi * ((c[:, None] * c[None, :]) % F_GROUP_DIM) / F_GROUP_DIM
    ortho = 1.0 / math.sqrt(seq * F_GROUP_DIM)
    cc = jnp.asarray(np.cos(ang) * ortho, F32)
    sc = jnp.asarray(np.sin(ang) * ortho, F32)
    return pl.pallas_call(
        _fold_kernel,
        grid=(depth,),
        in_specs=[
            pl.BlockSpec((F_GROUP_DIM, F_GROUP_DIM), lambda l: (0, 0)),
            pl.BlockSpec((F_GROUP_DIM, F_GROUP_DIM), lambda l: (0, 0)),
            pl.BlockSpec((1, F_GROUPS, F_GROUP_DIM, F_GROUP_DIM), lambda l: (l, 0, 0, 0)),
        ],
        out_specs=pl.BlockSpec((1, F_WIDTH, 2 * F_WIDTH), lambda l: (l, 0, 0)),
        out_shape=jax.ShapeDtypeStruct((depth, F_WIDTH, 2 * F_WIDTH), F32),
        name="fourier_fold",
    )(cc, sc, w_fourier)


def _trig_kernel(ang_ref, cos_ref, sin_ref):
    a = ang_ref[...]
    cos_ref[...] = jnp.cos(a)
    sin_ref[...] = jnp.sin(a)


def _rotary_tables(positions):
    b, s = positions.shape
    inv_freq = ROPE_BASE ** (-jnp.arange(HALF, dtype=F32) / HALF)
    ang = positions.astype(F32)[..., None] * inv_freq
    rows = b * s * HALF // LANES
    dense = ang.reshape(rows, LANES)
    tile = min(1024, rows)
    cos_d, sin_d = pl.pallas_call(
        _trig_kernel,
        grid=(rows // tile,),
        in_specs=[pl.BlockSpec((tile, LANES), lambda i: (i, 0))],
        out_specs=[pl.BlockSpec((tile, LANES), lambda i: (i, 0))] * 2,
        out_shape=[jax.ShapeDtypeStruct((rows, LANES), F32)] * 2,
        name="rotary_trig",
    )(dense)
    c = cos_d.reshape(b * s, HALF)
    sn = sin_d.reshape(b * s, HALF)
    z = jnp.zeros_like(c)
    cos_t = jnp.concatenate([c, z, c, z], axis=-1)
    sin_t = jnp.concatenate([-sn, z, sn, z], axis=-1)
    return cos_t, sin_t


def _proj_kernel(x_ref, g_ref, win_ref, wab_ref, qag_ref, wq_ref, kvag_ref, wkv_ref,
                 qng_ref, kng_ref, cos_ref, sin_ref, xab_ref, q_ref, k_ref, v_ref, *, q_scale):
    h = _rms(x_ref[...], g_ref[...])
    z = jnp.dot(h.astype(BF16), win_ref[...], preferred_element_type=F32)
    xab_ref[...] = jnp.dot(z[:, :F_WIDTH].astype(BF16), wab_ref[...],
                           preferred_element_type=F32).astype(BF16)
    c_q = z[:, F_WIDTH:F_WIDTH + Q_LORA]
    c_kv = z[:, F_WIDTH + Q_LORA:F_WIDTH + Q_LORA + KV_LORA]
    k_pe = z[:, F_WIDTH + Q_LORA + KV_LORA:]
    qf = jnp.dot(_rms(c_q, qag_ref[...]).astype(BF16), wq_ref[...], preferred_element_type=F32)
    kvf = jnp.dot(_rms(c_kv, kvag_ref[...]).astype(BF16), wkv_ref[...], preferred_element_type=F32)

    cos_t = cos_ref[...]
    sin_t = sin_ref[...]
    qg = qng_ref[...]
    kg = kng_ref[...]
    qg_nope, qg_rope = qg[:, :LANES], qg[:, LANES:]
    kg_nope, kg_rope = kg[:, :LANES], kg[:, LANES:]

    k_rot = _rotate(k_pe * kg_rope, cos_t, sin_t)
    ss_pe = jnp.sum(k_pe * k_pe, axis=-1, keepdims=True)
    inv_dim = 1.0 / QK_DIM
    for hd in range(N_HEADS):
        qh = qf[:, hd * HEAD_PAD:(hd + 1) * HEAD_PAD]
        rq = lax.rsqrt(jnp.sum(qh * qh, axis=-1, keepdims=True) * inv_dim + EPS) * q_scale
        q_ref[hd, :, :LANES] = (qh[:, :LANES] * rq * qg_nope).astype(BF16)
        q_ref[hd, :, LANES:] = _rotate(qh[:, LANES:] * rq * qg_rope, cos_t, sin_t).astype(BF16)
        kn = kvf[:, hd * HEAD_PAD:hd * HEAD_PAD + LANES]
        rk = lax.rsqrt((jnp.sum(kn * kn, axis=-1, keepdims=True) + ss_pe) * inv_dim + EPS)
        k_ref[hd, :, :LANES] = (kn * rk * kg_nope).astype(BF16)
        k_ref[hd, :, LANES:] = (k_rot * rk).astype(BF16)
        v_ref[hd] = kvf[:, hd * HEAD_PAD + LANES:(hd + 1) * HEAD_PAD].astype(BF16)


def _const_spec(shape):
    return pl.BlockSpec(shape, lambda *_: (0,) * len(shape))


def _project(x2, g, w_in, w_ab, qag, w_q, kvag, w_kv, qng, kng, cos_t, sin_t):
    n = x2.shape[0]
    tm = ROW_TILE
    kern = functools.partial(_proj_kernel, q_scale=math.log2(math.e) / math.sqrt(QK_DIM))
    row = lambda w: pl.BlockSpec((tm, w), lambda i: (i, 0))
    head = lambda w: pl.BlockSpec((N_HEADS, tm, w), lambda i: (0, i, 0))
    return pl.pallas_call(
        kern,
        grid=(n // tm,),
        in_specs=[
            row(D_MODEL), _const_spec((1, D_MODEL)), _const_spec((D_MODEL, Z_WIDTH)),
            _const_spec((F_WIDTH, 2 * F_WIDTH)), _const_spec((1, Q_LORA)),
            _const_spec((Q_LORA, N_HEADS * HEAD_PAD)), _const_spec((1, KV_LORA)),
            _const_spec((KV_LORA, N_HEADS * HEAD_PAD)), _const_spec((1, HEAD_PAD)),
            _const_spec((1, HEAD_PAD)), row(LANES), row(LANES),
        ],
        out_specs=[row(2 * F_WIDTH), head(HEAD_PAD), head(HEAD_PAD), head(V_DIM)],
        out_shape=[
            jax.ShapeDtypeStruct((n, 2 * F_WIDTH), BF16),
            jax.ShapeDtypeStruct((N_HEADS, n, HEAD_PAD), BF16),
            jax.ShapeDtypeStruct((N_HEADS, n, HEAD_PAD), BF16),
            jax.ShapeDtypeStruct((N_HEADS, n, V_DIM), BF16),
        ],
        compiler_params=pltpu.CompilerParams(dimension_semantics=("arbitrary",),
                                             vmem_limit_bytes=VMEM_LIMIT),
        name="in_proj",
    )(x2, g, w_in, w_ab, qag, w_q, kvag, w_kv, qng, kng, cos_t, sin_t)


def _fourier_kernel(c_ref, s_ref, xab_ref, g_ref, y_ref):
    xa = xab_ref[:, :F_WIDTH]
    xb = xab_ref[:, F_WIDTH:]
    y = (jnp.dot(c_ref[...], xa, preferred_element_type=F32)
         + jnp.dot(s_ref[...], xb, preferred_element_type=F32))
    y_ref[...] = _rms(y, g_ref[...]).astype(BF16)


def _fourier(xab, cmat, smat_neg, g, seq):
    n = xab.shape[0]
    return pl.pallas_call(
        _fourier_kernel,
        grid=(n // seq,),
        in_specs=[
            _const_spec((seq, seq)), _const_spec((seq, seq)),
            pl.BlockSpec((seq, 2 * F_WIDTH), lambda b: (b, 0)), _const_spec((1, F_WIDTH)),
        ],
        out_specs=pl.BlockSpec((seq, F_WIDTH), lambda b: (b, 0)),
        out_shape=jax.ShapeDtypeStruct((n, F_WIDTH), BF16),
        compiler_params=pltpu.CompilerParams(dimension_semantics=("arbitrary",),
                                             vmem_limit_bytes=VMEM_LIMIT),
        name="seq_dft",
    )(cmat, smat_neg, xab, g)


def _attn_kernel(q_ref, k_ref, v_ref, o_ref):
    s = lax.dot_general(q_ref[0], k_ref[0], (((1,), (1,)), ((), ())),
                        preferred_element_type=F32)
    m = jnp.max(s, axis=-1, keepdims=True)
    p = jnp.exp2(s - m)
    l = jnp.sum(p, axis=-1, keepdims=True)
    o = jnp.dot(p.astype(BF16), v_ref[0], preferred_element_type=F32)
    o_ref[...] = (o / l).astype(BF16)


def _attention(q, k, v, batch, seq):
    n = q.shape[1]
    tq = Q_TILE
    nq = seq // tq
    return pl.pallas_call(
        _attn_kernel,
        grid=(batch, N_HEADS, nq),
        in_specs=[
            pl.BlockSpec((1, tq, HEAD_PAD), lambda b, h, i: (h, b * nq + i, 0)),
            pl.BlockSpec((1, seq, HEAD_PAD), lambda b, h, i: (h, b, 0)),
            pl.BlockSpec((1, seq, V_DIM), lambda b, h, i: (h, b, 0)),
        ],
        out_specs=pl.BlockSpec((tq, V_DIM), lambda b, h, i: (b * nq + i, h)),
        out_shape=jax.ShapeDtypeStruct((n, A_WIDTH), BF16),
        compiler_params=pltpu.CompilerParams(
            dimension_semantics=("arbitrary", "arbitrary", "arbitrary"),
            vmem_limit_bytes=VMEM_LIMIT),
        name="attention",
    )(q, k, v)


def _out_mlp_kernel(x_ref, yf_ref, o_ref, ag_ref, wo_ref, mg_ref, w1_ref, w2_ref, out_ref, act_ref):
    ya = _rms(o_ref[...].astype(F32), ag_ref[...]).astype(BF16)
    x1 = (x_ref[...]
          + jnp.dot(yf_ref[...], wo_ref[:F_WIDTH, :], preferred_element_type=F32)
          + jnp.dot(ya, wo_ref[F_WIDTH:, :], preferred_element_type=F32))
    hn = _rms(x1, mg_ref[...]).astype(BF16)
    for c in range(D_FF // FF_CHUNK):
        cols = slice(c * FF_CHUNK, (c + 1) * FF_CHUNK)
        hm = jnp.dot(hn, w1_ref[:, cols], preferred_element_type=F32)
        act_ref[:, cols] = jnp.square(jnp.maximum(hm, 0.0)).astype(BF16)
    out_ref[...] = x1 + jnp.dot(act_ref[...], w2_ref[...], preferred_element_type=F32)


def _out_mlp(x2, yf, o, ag, w_out, mg, w1, w2):
    n = x2.shape[0]
    tm = ROW_TILE
    row = lambda w: pl.BlockSpec((tm, w), lambda i: (i, 0))
    return pl.pallas_call(
        _out_mlp_kernel,
        grid=(n // tm,),
        in_specs=[
            row(D_MODEL), row(F_WIDTH), row(A_WIDTH), _const_spec((1, A_WIDTH)),
            _const_spec((D_MODEL, D_MODEL)), _const_spec((1, D_MODEL)),
            _const_spec((D_MODEL, D_FF)), _const_spec((D_FF, D_MODEL)),
        ],
        out_specs=row(D_MODEL),
        out_shape=jax.ShapeDtypeStruct((n, D_MODEL), F32),
        scratch_shapes=[pltpu.VMEM((tm, D_FF), BF16)],
        compiler_params=pltpu.CompilerParams(dimension_semantics=("arbitrary",),
                                             vmem_limit_bytes=VMEM_LIMIT),
        name="out_mlp",
    )(x2, yf, o, ag, w_out, mg, w1, w2)


def _dft_matrices(seq):
    k = jnp.arange(seq, dtype=jnp.int32)
    idx = (k[:, None] * k[None, :]) % seq
    ang = idx.astype(F32) * (2.0 * math.pi / seq)
    return jnp.cos(ang).astype(BF16), (-jnp.sin(ang)).astype(BF16)


def kernel(x, positions, attn_norm_g, w_in, w_fourier, q_a_g, w_q_up, kv_a_g, w_kv_up, q_norm_g,
           k_norm_g, fourier_out_g, attn_out_g, w_out, mlp_norm_g, w_mlp_in, w_mlp_out):
    batch, seq, _ = x.shape
    depth = w_in.shape[0]
    n = batch * seq

    cos_t, sin_t = _rotary_tables(positions)
    cmat, smat_neg = _dft_matrices(seq)
    w_ab = _fold_fourier_weights(w_fourier, seq).astype(BF16)

    rope_cols = _rope_tile_cols(0)
    z_cols = np.concatenate([np.arange(F_WIDTH + Q_LORA + KV_LORA),
                             np.where(rope_cols >= 0, rope_cols + F_WIDTH + Q_LORA + KV_LORA, -1)])
    head_cols = np.concatenate([np.arange(QK_NOPE), _rope_tile_cols(QK_NOPE)])
    q_cols = np.concatenate([np.where(head_cols >= 0, head_cols + hd * QK_DIM, -1)
                             for hd in range(N_HEADS)])

    x2 = x.reshape(n, D_MODEL)
    for l in range(depth):
        w_in_l = _gather_cols(w_in[l], z_cols).astype(BF16)
        w_q_l = _gather_cols(w_q_up[l], q_cols).astype(BF16)
        w_kv_l = w_kv_up[l].astype(BF16)
        qng = _gather_cols(q_norm_g[l][None, :], head_cols)
        kng = _gather_cols(k_norm_g[l][None, :], head_cols)
        xab, q, k, v = _project(x2, attn_norm_g[l][None, :], w_in_l, w_ab[l], q_a_g[l][None, :],
                                w_q_l, kv_a_g[l][None, :], w_kv_l, qng, kng, cos_t, sin_t)
        yf = _fourier(xab, cmat, smat_neg, fourier_out_g[l][None, :], seq)
        o = _attention(q, k, v, batch, seq)
        x2 = _out_mlp(x2, yf, o, attn_out_g[l][None, :], w_out[l].astype(BF16),
                      mlp_norm_g[l][None, :], w_mlp_in[l].astype(BF16), w_mlp_out[l].astype(BF16))
    return x2.reshape(batch, seq, D_MODEL)
```

```python
import functools
import math

import numpy as np
import jax
import jax.numpy as jnp
from jax import lax
from jax.experimental import pallas as pl
from jax.experimental.pallas import tpu as pltpu

D_MODEL = 1024
F_GROUPS = 4
F_GROUP_DIM = 64
F_WIDTH = F_GROUPS * F_GROUP_DIM
N_HEADS = 6
Q_LORA = 256
KV_LORA = 256
QK_NOPE = 128
QK_ROPE = 64
V_DIM = 128
QK_DIM = QK_NOPE + QK_ROPE
A_WIDTH = N_HEADS * V_DIM
ROPE_BASE = 10000.0
D_FF = 4 * D_MODEL
EPS = 1e-6

LANES = 128
HEAD_PAD = 2 * LANES
HALF = QK_ROPE // 2
Z_WIDTH = F_WIDTH + Q_LORA + KV_LORA + LANES
VMEM_LIMIT = 56 * 1024 * 1024

ROW_TILE = 512
Q_TILE = 256
FF_CHUNK = 1024

BF16 = jnp.bfloat16
F32 = jnp.float32


def _rope_tile_cols(start):
    cols = np.full((LANES,), -1, np.int64)
    cols[0:HALF] = start + np.arange(HALF)
    cols[2 * HALF:3 * HALF] = start + HALF + np.arange(HALF)
    return cols


def _gather_cols(w, cols):
    taken = jnp.take(w, jnp.asarray(np.maximum(cols, 0)), axis=-1)
    return jnp.where(jnp.asarray(cols >= 0), taken, jnp.zeros((), w.dtype))


def _rms(x, g):
    return x * lax.rsqrt(jnp.mean(x * x, axis=-1, keepdims=True) + EPS) * g


def _rotate(t, cos_t, sin_t):
    return t * cos_t + pltpu.roll(t, 2 * HALF, axis=1) * sin_t


def _fold_kernel(cc_ref, sc_ref, w_ref, out_ref):
    out_ref[...] = jnp.zeros_like(out_ref)
    for g in range(F_GROUPS):
        w = w_ref[0, g]
        a = jnp.dot(cc_ref[...], w, preferred_element_type=F32, precision=lax.Precision.HIGHEST)
        b = jnp.dot(sc_ref[...], w, preferred_element_type=F32, precision=lax.Precision.HIGHEST)
        lo, hi = g * F_GROUP_DIM, (g + 1) * F_GROUP_DIM
        out_ref[0, lo:hi, lo:hi] = a
        out_ref[0, lo:hi, F_WIDTH + lo:F_WIDTH + hi] = b


def _fold_fourier_weights(w_fourier, seq):
    depth = w_fourier.shape[0]
    c = np.arange(F_GROUP_DIM)
    ang = 2.0 * np.pi * ((c[:, None] * c[None, :]) % F_GROUP_DIM) / F_GROUP_DIM
    ortho = 1.0 / math.sqrt(seq * F_GROUP_DIM)
    cc = jnp.asarray(np.cos(ang) * ortho, F32)
    sc = jnp.asarray(np.sin(ang) * ortho, F32)
    return pl.pallas_call(
        _fold_kernel,
        grid=(depth,),
        in_specs=[
            pl.BlockSpec((F_GROUP_DIM, F_GROUP_DIM), lambda l: (0, 0)),
            pl.BlockSpec((F_GROUP_DIM, F_GROUP_DIM), lambda l: (0, 0)),
            pl.BlockSpec((1, F_GROUPS, F_GROUP_DIM, F_GROUP_DIM), lambda l: (l, 0, 0, 0)),
        ],
        out_specs=pl.BlockSpec((1, F_WIDTH, 2 * F_WIDTH), lambda l: (l, 0, 0)),
        out_shape=jax.ShapeDtypeStruct((depth, F_WIDTH, 2 * F_WIDTH), F32),
        name="fourier_fold",
    )(cc, sc, w_fourier)


def _trig_kernel(ang_ref, cos_ref, sin_ref):
    a = ang_ref[...]
    cos_ref[...] = jnp.cos(a)
    sin_ref[...] = jnp.sin(a)


def _rotary_tables(positions):
    b, s = positions.shape
    inv_freq = ROPE_BASE ** (-jnp.arange(HALF, dtype=F32) / HALF)
    ang = positions.astype(F32)[..., None] * inv_freq
    rows = b * s * HALF // LANES
    dense = ang.reshape(rows, LANES)
    tile = min(1024, rows)
    cos_d, sin_d = pl.pallas_call(
        _trig_kernel,
        grid=(rows // tile,),
        in_specs=[pl.BlockSpec((tile, LANES), lambda i: (i, 0))],
        out_specs=[pl.BlockSpec((tile, LANES), lambda i: (i, 0))] * 2,
        out_shape=[jax.ShapeDtypeStruct((rows, LANES), F32)] * 2,
        name="rotary_trig",
    )(dense)
    c = cos_d.reshape(b * s, HALF)
    sn = sin_d.reshape(b * s, HALF)
    z = jnp.zeros_like(c)
    cos_t = jnp.concatenate([c, z, c, z], axis=-1)
    sin_t = jnp.concatenate([-sn, z, sn, z], axis=-1)
    return cos_t, sin_t


def _proj_kernel(x_ref, g_ref, win_ref, wab_ref, qag_ref, wq_ref, kvag_ref, wkv_ref,
                 qng_ref, kng_ref, cos_ref, sin_ref, xab_ref, q_ref, k_ref, v_ref, *, q_scale):
    h = _rms(x_ref[...], g_ref[...])
    z = jnp.dot(h.astype(BF16), win_ref[...], preferred_element_type=F32)
    xab_ref[...] = jnp.dot(z[:, :F_WIDTH].astype(BF16), wab_ref[...],
                           preferred_element_type=F32).astype(BF16)
    c_q = z[:, F_WIDTH:F_WIDTH + Q_LORA]
    c_kv = z[:, F_WIDTH + Q_LORA:F_WIDTH + Q_LORA + KV_LORA]
    k_pe = z[:, F_WIDTH + Q_LORA + KV_LORA:]
    qf = jnp.dot(_rms(c_q, qag_ref[...]).astype(BF16), wq_ref[...], preferred_element_type=F32)
    kvf = jnp.dot(_rms(c_kv, kvag_ref[...]).astype(BF16), wkv_ref[...], preferred_element_type=F32)

    cos_t = cos_ref[...]
    sin_t = sin_ref[...]
    qg = qng_ref[...]
    kg = kng_ref[...]
    qg_nope, qg_rope = qg[:, :LANES], qg[:, LANES:]
    kg_nope, kg_rope = kg[:, :LANES], kg[:, LANES:]

    k_rot = _rotate(k_pe * kg_rope, cos_t, sin_t)
    ss_pe = jnp.sum(k_pe * k_pe, axis=-1, keepdims=True)
    inv_dim = 1.0 / QK_DIM
    for hd in range(N_HEADS):
        qh = qf[:, hd * HEAD_PAD:(hd + 1) * HEAD_PAD]
        rq = lax.rsqrt(jnp.sum(qh * qh, axis=-1, keepdims=True) * inv_dim + EPS) * q_scale
        q_ref[hd, :, :LANES] = (qh[:, :LANES] * rq * qg_nope).astype(BF16)
        q_ref[hd, :, LANES:] = _rotate(qh[:, LANES:] * rq * qg_rope, cos_t, sin_t).astype(BF16)
        kn = kvf[:, hd * HEAD_PAD:hd * HEAD_PAD + LANES]
        rk = lax.rsqrt((jnp.sum(kn * kn, axis=-1, keepdims=True) + ss_pe) * inv_dim + EPS)
        k_ref[hd, :, :LANES] = (kn * rk * kg_nope).astype(BF16)
        k_ref[hd, :, LANES:] = (k_rot * rk).astype(BF16)
        v_ref[hd] = kvf[:, hd * HEAD_PAD + LANES:(hd + 1) * HEAD_PAD].astype(BF16)


def _const_spec(shape):
    return pl.BlockSpec(shape, lambda *_: (0,) * len(shape))


def _project(x2, g, w_in, w_ab, qag, w_q, kvag, w_kv, qng, kng, cos_t, sin_t):
    n = x2.shape[0]
    tm = ROW_TILE
    kern = functools.partial(_proj_kernel, q_scale=math.log2(math.e) / math.sqrt(QK_DIM))
    row = lambda w: pl.BlockSpec((tm, w), lambda i: (i, 0))
    head = lambda w: pl.BlockSpec((N_HEADS, tm, w), lambda i: (0, i, 0))
    return pl.pallas_call(
        kern,
        grid=(n // tm,),
        in_specs=[
            row(D_MODEL), _const_spec((1, D_MODEL)), _const_spec((D_MODEL, Z_WIDTH)),
            _const_spec((F_WIDTH, 2 * F_WIDTH)), _const_spec((1, Q_LORA)),
            _const_spec((Q_LORA, N_HEADS * HEAD_PAD)), _const_spec((1, KV_LORA)),
            _const_spec((KV_LORA, N_HEADS * HEAD_PAD)), _const_spec((1, HEAD_PAD)),
            _const_spec((1, HEAD_PAD)), row(LANES), row(LANES),
        ],
        out_specs=[row(2 * F_WIDTH), head(HEAD_PAD), head(HEAD_PAD), head(V_DIM)],
        out_shape=[
            jax.ShapeDtypeStruct((n, 2 * F_WIDTH), BF16),
            jax.ShapeDtypeStruct((N_HEADS, n, HEAD_PAD), BF16),
            jax.ShapeDtypeStruct((N_HEADS, n, HEAD_PAD), BF16),
            jax.ShapeDtypeStruct((N_HEADS, n, V_DIM), BF16),
        ],
        compiler_params=pltpu.CompilerParams(dimension_semantics=("arbitrary",),
                                             vmem_limit_bytes=VMEM_LIMIT),
        name="in_proj",
    )(x2, g, w_in, w_ab, qag, w_q, kvag, w_kv, qng, kng, cos_t, sin_t)


def _fourier_kernel(c_ref, s_ref, xab_ref, g_ref, y_ref):
    xa = xab_ref[:, :F_WIDTH]
    xb = xab_ref[:, F_WIDTH:]
    y = (jnp.dot(c_ref[...], xa, preferred_element_type=F32)
         + jnp.dot(s_ref[...], xb, preferred_element_type=F32))
    y_ref[...] = _rms(y, g_ref[...]).astype(BF16)


def _fourier(xab, cmat, smat_neg, g, seq):
    n = xab.shape[0]
    return pl.pallas_call(
        _fourier_kernel,
        grid=(n // seq,),
        in_specs=[
            _const_spec((seq, seq)), _const_spec((seq, seq)),
            pl.BlockSpec((seq, 2 * F_WIDTH), lambda b: (b, 0)), _const_spec((1, F_WIDTH)),
        ],
        out_specs=pl.BlockSpec((seq, F_WIDTH), lambda b: (b, 0)),
        out_shape=jax.ShapeDtypeStruct((n, F_WIDTH), BF16),
        compiler_params=pltpu.CompilerParams(dimension_semantics=("arbitrary",),
                                             vmem_limit_bytes=VMEM_LIMIT),
        name="seq_dft",
    )(cmat, smat_neg, xab, g)


def _attn_kernel(q_ref, k_ref, v_ref, o_ref, vx_ref):
    vx_ref[:, :V_DIM] = v_ref[0]
    vx_ref[:, V_DIM:] = jnp.ones((vx_ref.shape[0], V_DIM), BF16)
    for j in range(q_ref.shape[1] // Q_TILE):
        rows = slice(j * Q_TILE, (j + 1) * Q_TILE)
        s = lax.dot_general(q_ref[0, rows, :], k_ref[0], (((1,), (1,)), ((), ())),
                            preferred_element_type=F32)
        m = jnp.max(s, axis=-1, keepdims=True)
        p = jnp.exp2(s - m).astype(BF16)
        ox = jnp.dot(p, vx_ref[...], preferred_element_type=F32)
        o_ref[rows, :] = (ox[:, :V_DIM] / ox[:, V_DIM:]).astype(BF16)


def _attention(q, k, v, batch, seq):
    n = q.shape[1]
    return pl.pallas_call(
        _attn_kernel,
        grid=(batch, N_HEADS),
        in_specs=[
            pl.BlockSpec((1, seq, HEAD_PAD), lambda b, h: (h, b, 0)),
            pl.BlockSpec((1, seq, HEAD_PAD), lambda b, h: (h, b, 0)),
            pl.BlockSpec((1, seq, V_DIM), lambda b, h: (h, b, 0)),
        ],
        out_specs=pl.BlockSpec((seq, V_DIM), lambda b, h: (b, h)),
        out_shape=jax.ShapeDtypeStruct((n, A_WIDTH), BF16),
        scratch_shapes=[pltpu.VMEM((seq, 2 * V_DIM), BF16)],
        compiler_params=pltpu.CompilerParams(
            dimension_semantics=("arbitrary", "arbitrary"),
            vmem_limit_bytes=VMEM_LIMIT),
        name="attention",
    )(q, k, v)


def _out_mlp_kernel(x_ref, yf_ref, o_ref, ag_ref, wo_ref, mg_ref, w1_ref, w2_ref, out_ref, act_ref):
    ya = _rms(o_ref[...].astype(F32), ag_ref[...]).astype(BF16)
    x1 = (x_ref[...]
          + jnp.dot(yf_ref[...], wo_ref[:F_WIDTH, :], preferred_element_type=F32)
          + jnp.dot(ya, wo_ref[F_WIDTH:, :], preferred_element_type=F32))
    hn = _rms(x1, mg_ref[...]).astype(BF16)
    for c in range(D_FF // FF_CHUNK):
        cols = slice(c * FF_CHUNK, (c + 1) * FF_CHUNK)
        hm = jnp.dot(hn, w1_ref[:, cols], preferred_element_type=F32)
        act_ref[:, cols] = jnp.square(jnp.maximum(hm, 0.0)).astype(BF16)
    out_ref[...] = x1 + jnp.dot(act_ref[...], w2_ref[...], preferred_element_type=F32)


def _out_mlp(x2, yf, o, ag, w_out, mg, w1, w2):
    n = x2.shape[0]
    tm = ROW_TILE
    row = lambda w: pl.BlockSpec((tm, w), lambda i: (i, 0))
    return pl.pallas_call(
        _out_mlp_kernel,
        grid=(n // tm,),
        in_specs=[
            row(D_MODEL), row(F_WIDTH), row(A_WIDTH), _const_spec((1, A_WIDTH)),
            _const_spec((D_MODEL, D_MODEL)), _const_spec((1, D_MODEL)),
            _const_spec((D_MODEL, D_FF)), _const_spec((D_FF, D_MODEL)),
        ],
        out_specs=row(D_MODEL),
        out_shape=jax.ShapeDtypeStruct((n, D_MODEL), F32),
        scratch_shapes=[pltpu.VMEM((tm, D_FF), BF16)],
        compiler_params=pltpu.CompilerParams(dimension_semantics=("arbitrary",),
                                             vmem_limit_bytes=VMEM_LIMIT),
        name="out_mlp",
    )(x2, yf, o, ag, w_out, mg, w1, w2)


def _dft_matrices(seq):
    k = jnp.arange(seq, dtype=jnp.int32)
    idx = (k[:, None] * k[None, :]) % seq
    ang = idx.astype(F32) * (2.0 * math.pi / seq)
    return jnp.cos(ang).astype(BF16), (-jnp.sin(ang)).astype(BF16)


def kernel(x, positions, attn_norm_g, w_in, w_fourier, q_a_g, w_q_up, kv_a_g, w_kv_up, q_norm_g,
           k_norm_g, fourier_out_g, attn_out_g, w_out, mlp_norm_g, w_mlp_in, w_mlp_out):
    batch, seq, _ = x.shape
    depth = w_in.shape[0]
    n = batch * seq

    cos_t, sin_t = _rotary_tables(positions)
    cmat, smat_neg = _dft_matrices(seq)
    w_ab = _fold_fourier_weights(w_fourier, seq).astype(BF16)

    rope_cols = _rope_tile_cols(0)
    z_cols = np.concatenate([np.arange(F_WIDTH + Q_LORA + KV_LORA),
                             np.where(rope_cols >= 0, rope_cols + F_WIDTH + Q_LORA + KV_LORA, -1)])
    head_cols = np.concatenate([np.arange(QK_NOPE), _rope_tile_cols(QK_NOPE)])
    q_cols = np.concatenate([np.where(head_cols >= 0, head_cols + hd * QK_DIM, -1)
                             for hd in range(N_HEADS)])

    x2 = x.reshape(n, D_MODEL)
    for l in range(depth):
        w_in_l = _gather_cols(w_in[l], z_cols).astype(BF16)
        w_q_l = _gather_cols(w_q_up[l], q_cols).astype(BF16)
        w_kv_l = w_kv_up[l].astype(BF16)
        qng = _gather_cols(q_norm_g[l][None, :], head_cols)
        kng = _gather_cols(k_norm_g[l][None, :], head_cols)
        xab, q, k, v = _project(x2, attn_norm_g[l][None, :], w_in_l, w_ab[l], q_a_g[l][None, :],
                                w_q_l, kv_a_g[l][None, :], w_kv_l, qng, kng, cos_t, sin_t)
        yf = _fourier(xab, cmat, smat_neg, fourier_out_g[l][None, :], seq)
        o = _attention(q, k, v, batch, seq)
        x2 = _out_mlp(x2, yf, o, attn_out_g[l][None, :], w_out[l].astype(BF16),
                      mlp_norm_g[l][None, :], w_mlp_in[l].astype(BF16), w_mlp_out[l].astype(BF16))
    return x2.reshape(batch, seq, D_MODEL)
```

```python
import functools
import math

import numpy as np
import jax
import jax.numpy as jnp
from jax import lax
from jax.experimental import pallas as pl
from jax.experimental.pallas import tpu as pltpu

D_MODEL = 1024
F_GROUPS = 4
F_GROUP_DIM = 64
F_WIDTH = F_GROUPS * F_GROUP_DIM
N_HEADS = 6
Q_LORA = 256
KV_LORA = 256
QK_NOPE = 128
QK_ROPE = 64
V_DIM = 128
QK_DIM = QK_NOPE + QK_ROPE
A_WIDTH = N_HEADS * V_DIM
ROPE_BASE = 10000.0
D_FF = 4 * D_MODEL
EPS = 1e-6

LANES = 128
HEAD_PAD = 2 * LANES
HALF = QK_ROPE // 2
Z_WIDTH = F_WIDTH + Q_LORA + KV_LORA + LANES
VMEM_LIMIT = 56 * 1024 * 1024

ROW_TILE = 512
PROJ_TILE = 1024
PROJ_SUB = 128
DFT_SPLIT = 64
Q_TILE = 256
FF_CHUNK = 1024

BF16 = jnp.bfloat16
F32 = jnp.float32


def _rope_tile_cols(start):
    cols = np.full((LANES,), -1, np.int64)
    cols[0:HALF] = start + np.arange(HALF)
    cols[2 * HALF:3 * HALF] = start + HALF + np.arange(HALF)
    return cols


def _gather_cols(w, cols):
    taken = jnp.take(w, jnp.asarray(np.maximum(cols, 0)), axis=-1)
    return jnp.where(jnp.asarray(cols >= 0), taken, jnp.zeros((), w.dtype))


def _rms(x, g):
    return x * lax.rsqrt(jnp.mean(x * x, axis=-1, keepdims=True) + EPS) * g


def _rotate(t, cos_t, sin_t):
    return t * cos_t + pltpu.roll(t, 2 * HALF, axis=1) * sin_t


def _fold_kernel(cc_ref, sc_ref, w_ref, out_ref):
    out_ref[...] = jnp.zeros_like(out_ref)
    for g in range(F_GROUPS):
        w = w_ref[0, g]
        a = jnp.dot(cc_ref[...], w, preferred_element_type=F32, precision=lax.Precision.HIGHEST)
        b = jnp.dot(sc_ref[...], w, preferred_element_type=F32, precision=lax.Precision.HIGHEST)
        lo, hi = g * F_GROUP_DIM, (g + 1) * F_GROUP_DIM
        out_ref[0, lo:hi, lo:hi] = a
        out_ref[0, lo:hi, F_WIDTH + lo:F_WIDTH + hi] = b


def _fold_fourier_weights(w_fourier, seq):
    depth = w_fourier.shape[0]
    c = np.arange(F_GROUP_DIM)
    ang = 2.0 * np.pi * ((c[:, None] * c[None, :]) % F_GROUP_DIM) / F_GROUP_DIM
    ortho = 1.0 / math.sqrt(seq * F_GROUP_DIM)
    cc = jnp.asarray(np.cos(ang) * ortho, F32)
    sc = jnp.asarray(np.sin(ang) * ortho, F32)
    return pl.pallas_call(
        _fold_kernel,
        grid=(depth,),
        in_specs=[
            pl.BlockSpec((F_GROUP_DIM, F_GROUP_DIM), lambda l: (0, 0)),
            pl.BlockSpec((F_GROUP_DIM, F_GROUP_DIM), lambda l: (0, 0)),
            pl.BlockSpec((1, F_GROUPS, F_GROUP_DIM, F_GROUP_DIM), lambda l: (l, 0, 0, 0)),
        ],
        out_specs=pl.BlockSpec((1, F_WIDTH, 2 * F_WIDTH), lambda l: (l, 0, 0)),
        out_shape=jax.ShapeDtypeStruct((depth, F_WIDTH, 2 * F_WIDTH), F32),
        name="fourier_fold",
    )(cc, sc, w_fourier)


def _trig_kernel(ang_ref, cos_ref, sin_ref):
    a = ang_ref[...]
    cos_ref[...] = jnp.cos(a)
    sin_ref[...] = jnp.sin(a)


def _rotary_tables(positions):
    b, s = positions.shape
    inv_freq = ROPE_BASE ** (-jnp.arange(HALF, dtype=F32) / HALF)
    ang = positions.astype(F32)[..., None] * inv_freq
    rows = b * s * HALF // LANES
    dense = ang.reshape(rows, LANES)
    tile = min(1024, rows)
    cos_d, sin_d = pl.pallas_call(
        _trig_kernel,
        grid=(rows // tile,),
        in_specs=[pl.BlockSpec((tile, LANES), lambda i: (i, 0))],
        out_specs=[pl.BlockSpec((tile, LANES), lambda i: (i, 0))] * 2,
        out_shape=[jax.ShapeDtypeStruct((rows, LANES), F32)] * 2,
        name="rotary_trig",
    )(dense)
    c = cos_d.reshape(b * s, HALF)
    sn = sin_d.reshape(b * s, HALF)
    z = jnp.zeros_like(c)
    cos_t = jnp.concatenate([c, z, c, z], axis=-1)
    sin_t = jnp.concatenate([-sn, z, sn, z], axis=-1)
    return cos_t, sin_t


def _proj_kernel(x_ref, g_ref, win_ref, wab_ref, qag_ref, wq_ref, kvag_ref, wkv_ref,
                 qng_ref, kng_ref, cos_ref, sin_ref, xab_ref, q_ref, k_ref, v_ref, *, q_scale):
    qg = qng_ref[...]
    kg = kng_ref[...]
    qg_nope, qg_rope = qg[:, :LANES], qg[:, LANES:]
    kg_nope, kg_rope = kg[:, :LANES], kg[:, LANES:]
    inv_dim = 1.0 / QK_DIM
    for j in range(x_ref.shape[0] // PROJ_SUB):
        rows = slice(j * PROJ_SUB, (j + 1) * PROJ_SUB)
        h = _rms(x_ref[rows, :], g_ref[...])
        z = jnp.dot(h.astype(BF16), win_ref[...], preferred_element_type=F32)
        xab_ref[rows, :] = jnp.dot(z[:, :F_WIDTH].astype(BF16), wab_ref[...],
                                   preferred_element_type=F32).astype(BF16)
        c_q = z[:, F_WIDTH:F_WIDTH + Q_LORA]
        c_kv = z[:, F_WIDTH + Q_LORA:F_WIDTH + Q_LORA + KV_LORA]
        k_pe = z[:, F_WIDTH + Q_LORA + KV_LORA:]
        qf = jnp.dot(_rms(c_q, qag_ref[...]).astype(BF16), wq_ref[...], preferred_element_type=F32)
        kvf = jnp.dot(_rms(c_kv, kvag_ref[...]).astype(BF16), wkv_ref[...],
                      preferred_element_type=F32)
        cos_t = cos_ref[rows, :]
        sin_t = sin_ref[rows, :]
        k_rot = _rotate(k_pe * kg_rope, cos_t, sin_t)
        ss_pe = jnp.sum(k_pe * k_pe, axis=-1, keepdims=True)
        for hd in range(N_HEADS):
            qh = qf[:, hd * HEAD_PAD:(hd + 1) * HEAD_PAD]
            rq = lax.rsqrt(jnp.sum(qh * qh, axis=-1, keepdims=True) * inv_dim + EPS) * q_scale
            q_ref[hd, rows, :LANES] = (qh[:, :LANES] * rq * qg_nope).astype(BF16)
            q_ref[hd, rows, LANES:] = _rotate(qh[:, LANES:] * rq * qg_rope, cos_t, sin_t).astype(BF16)
            kn = kvf[:, hd * HEAD_PAD:hd * HEAD_PAD + LANES]
            rk = lax.rsqrt((jnp.sum(kn * kn, axis=-1, keepdims=True) + ss_pe) * inv_dim + EPS)
            k_ref[hd, rows, :LANES] = (kn * rk * kg_nope).astype(BF16)
            k_ref[hd, rows, LANES:] = (k_rot * rk).astype(BF16)
            v_ref[hd, rows, :] = kvf[:, hd * HEAD_PAD + LANES:(hd + 1) * HEAD_PAD].astype(BF16)


def _const_spec(shape):
    return pl.BlockSpec(shape, lambda *_: (0,) * len(shape))


def _project(x2, g, w_in, w_ab, qag, w_q, kvag, w_kv, qng, kng, cos_t, sin_t):
    n = x2.shape[0]
    tm = PROJ_TILE
    kern = functools.partial(_proj_kernel, q_scale=math.log2(math.e) / math.sqrt(QK_DIM))
    row = lambda w: pl.BlockSpec((tm, w), lambda i: (i, 0))
    head = lambda w: pl.BlockSpec((N_HEADS, tm, w), lambda i: (0, i, 0))
    return pl.pallas_call(
        kern,
        grid=(n // tm,),
        in_specs=[
            row(D_MODEL), _const_spec((1, D_MODEL)), _const_spec((D_MODEL, Z_WIDTH)),
            _const_spec((F_WIDTH, 2 * F_WIDTH)), _const_spec((1, Q_LORA)),
            _const_spec((Q_LORA, N_HEADS * HEAD_PAD)), _const_spec((1, KV_LORA)),
            _const_spec((KV_LORA, N_HEADS * HEAD_PAD)), _const_spec((1, HEAD_PAD)),
            _const_spec((1, HEAD_PAD)), row(LANES), row(LANES),
        ],
        out_specs=[row(2 * F_WIDTH), head(HEAD_PAD), head(HEAD_PAD), head(V_DIM)],
        out_shape=[
            jax.ShapeDtypeStruct((n, 2 * F_WIDTH), BF16),
            jax.ShapeDtypeStruct((N_HEADS, n, HEAD_PAD), BF16),
            jax.ShapeDtypeStruct((N_HEADS, n, HEAD_PAD), BF16),
            jax.ShapeDtypeStruct((N_HEADS, n, V_DIM), BF16),
        ],
        compiler_params=pltpu.CompilerParams(dimension_semantics=("arbitrary",),
                                             vmem_limit_bytes=VMEM_LIMIT),
        name="in_proj",
    )(x2, g, w_in, w_ab, qag, w_q, kvag, w_kv, qng, kng, cos_t, sin_t)


def _fourier_kernel(c_ref, s_ref, xab_ref, g_ref, y_ref):
    xa = xab_ref[:, :F_WIDTH]
    xb = xab_ref[:, F_WIDTH:]
    y = (jnp.dot(c_ref[...], xa, preferred_element_type=F32)
         + jnp.dot(s_ref[...], xb, preferred_element_type=F32))
    y_ref[...] = _rms(y, g_ref[...]).astype(BF16)


def _fourier(xab, cmat, smat_neg, g, seq):
    n = xab.shape[0]
    return pl.pallas_call(
        _fourier_kernel,
        grid=(n // seq,),
        in_specs=[
            _const_spec((seq, seq)), _const_spec((seq, seq)),
            pl.BlockSpec((seq, 2 * F_WIDTH), lambda b: (b, 0)), _const_spec((1, F_WIDTH)),
        ],
        out_specs=pl.BlockSpec((seq, F_WIDTH), lambda b: (b, 0)),
        out_shape=jax.ShapeDtypeStruct((n, F_WIDTH), BF16),
        compiler_params=pltpu.CompilerParams(dimension_semantics=("arbitrary",),
                                             vmem_limit_bytes=VMEM_LIMIT),
        name="seq_dft",
    )(cmat, smat_neg, xab, g)


def _attn_kernel(q_ref, k_ref, v_ref, o_ref, vx_ref):
    vx_ref[:, :V_DIM] = v_ref[0]
    vx_ref[:, V_DIM:] = jnp.ones((vx_ref.shape[0], V_DIM), BF16)
    for j in range(q_ref.shape[1] // Q_TILE):
        rows = slice(j * Q_TILE, (j + 1) * Q_TILE)
        s = lax.dot_general(q_ref[0, rows, :], k_ref[0], (((1,), (1,)), ((), ())),
                            preferred_element_type=F32)
        m = jnp.max(s, axis=-1, keepdims=True)
        p = jnp.exp2(s - m).astype(BF16)
        ox = jnp.dot(p, vx_ref[...], preferred_element_type=F32)
        o_ref[rows, :] = (ox[:, :V_DIM] / ox[:, V_DIM:]).astype(BF16)


def _attention(q, k, v, batch, seq):
    n = q.shape[1]
    return pl.pallas_call(
        _attn_kernel,
        grid=(batch, N_HEADS),
        in_specs=[
            pl.BlockSpec((1, seq, HEAD_PAD), lambda b, h: (h, b, 0)),
            pl.BlockSpec((1, seq, HEAD_PAD), lambda b, h: (h, b, 0)),
            pl.BlockSpec((1, seq, V_DIM), lambda b, h: (h, b, 0)),
        ],
        out_specs=pl.BlockSpec((seq, V_DIM), lambda b, h: (b, h)),
        out_shape=jax.ShapeDtypeStruct((n, A_WIDTH), BF16),
        scratch_shapes=[pltpu.VMEM((seq, 2 * V_DIM), BF16)],
        compiler_params=pltpu.CompilerParams(
            dimension_semantics=("arbitrary", "arbitrary"),
            vmem_limit_bytes=VMEM_LIMIT),
        name="attention",
    )(q, k, v)


def _out_mlp_kernel(x_ref, yf_ref, o_ref, ag_ref, wo_ref, mg_ref, w1_ref, w2_ref, out_ref, act_ref):
    ya = _rms(o_ref[...].astype(F32), ag_ref[...]).astype(BF16)
    x1 = (x_ref[...]
          + jnp.dot(yf_ref[...], wo_ref[:F_WIDTH, :], preferred_element_type=F32)
          + jnp.dot(ya, wo_ref[F_WIDTH:, :], preferred_element_type=F32))
    hn = _rms(x1, mg_ref[...]).astype(BF16)
    for c in range(D_FF // FF_CHUNK):
        cols = slice(c * FF_CHUNK, (c + 1) * FF_CHUNK)
        hm = jnp.dot(hn, w1_ref[:, cols], preferred_element_type=F32)
        act_ref[:, cols] = jnp.square(jnp.maximum(hm, 0.0)).astype(BF16)
    out_ref[...] = x1 + jnp.dot(act_ref[...], w2_ref[...], preferred_element_type=F32)


def _out_mlp(x2, yf, o, ag, w_out, mg, w1, w2):
    n = x2.shape[0]
    tm = ROW_TILE
    row = lambda w: pl.BlockSpec((tm, w), lambda i: (i, 0))
    return pl.pallas_call(
        _out_mlp_kernel,
        grid=(n // tm,),
        in_specs=[
            row(D_MODEL), row(F_WIDTH), row(A_WIDTH), _const_spec((1, A_WIDTH)),
            _const_spec((D_MODEL, D_MODEL)), _const_spec((1, D_MODEL)),
            _const_spec((D_MODEL, D_FF)), _const_spec((D_FF, D_MODEL)),
        ],
        out_specs=row(D_MODEL),
        out_shape=jax.ShapeDtypeStruct((n, D_MODEL), F32),
        scratch_shapes=[pltpu.VMEM((tm, D_FF), BF16)],
        compiler_params=pltpu.CompilerParams(dimension_semantics=("arbitrary",),
                                             vmem_limit_bytes=VMEM_LIMIT),
        name="out_mlp",
    )(x2, yf, o, ag, w_out, mg, w1, w2)


def _dft_matrices(seq):
    lo_n = DFT_SPLIT
    hi_n = seq // lo_n
    k = np.arange(seq)[:, None]
    ang_hi = 2.0 * np.pi * ((k * lo_n * np.arange(hi_n)[None, :]) % seq) / seq
    ang_lo = 2.0 * np.pi * ((k * np.arange(lo_n)[None, :]) % seq) / seq
    ch, sh = jnp.asarray(np.cos(ang_hi), F32)[:, :, None], jnp.asarray(np.sin(ang_hi), F32)[:, :, None]
    cl, sl = jnp.asarray(np.cos(ang_lo), F32)[:, None, :], jnp.asarray(np.sin(ang_lo), F32)[:, None, :]
    cmat = (ch * cl - sh * sl).reshape(seq, seq)
    smat_neg = (-(sh * cl + ch * sl)).reshape(seq, seq)
    return cmat.astype(BF16), smat_neg.astype(BF16)


def kernel(x, positions, attn_norm_g, w_in, w_fourier, q_a_g, w_q_up, kv_a_g, w_kv_up, q_norm_g,
           k_norm_g, fourier_out_g, attn_out_g, w_out, mlp_norm_g, w_mlp_in, w_mlp_out):
    batch, seq, _ = x.shape
    depth = w_in.shape[0]
    n = batch * seq

    cos_t, sin_t = _rotary_tables(positions)
    cmat, smat_neg = _dft_matrices(seq)
    w_ab = _fold_fourier_weights(w_fourier, seq).astype(BF16)

    rope_cols = _rope_tile_cols(0)
    z_cols = np.concatenate([np.arange(F_WIDTH + Q_LORA + KV_LORA),
                             np.where(rope_cols >= 0, rope_cols + F_WIDTH + Q_LORA + KV_LORA, -1)])
    head_cols = np.concatenate([np.arange(QK_NOPE), _rope_tile_cols(QK_NOPE)])
    q_cols = np.concatenate([np.where(head_cols >= 0, head_cols + hd * QK_DIM, -1)
                             for hd in range(N_HEADS)])

    x2 = x.reshape(n, D_MODEL)
    for l in range(depth):
        w_in_l = _gather_cols(w_in[l], z_cols).astype(BF16)
        w_q_l = _gather_cols(w_q_up[l], q_cols).astype(BF16)
        w_kv_l = w_kv_up[l].astype(BF16)
        qng = _gather_cols(q_norm_g[l][None, :], head_cols)
        kng = _gather_cols(k_norm_g[l][None, :], head_cols)
        xab, q, k, v = _project(x2, attn_norm_g[l][None, :], w_in_l, w_ab[l], q_a_g[l][None, :],
                                w_q_l, kv_a_g[l][None, :], w_kv_l, qng, kng, cos_t, sin_t)
        yf = _fourier(xab, cmat, smat_neg, fourier_out_g[l][None, :], seq)
        o = _attention(q, k, v, batch, seq)
        x2 = _out_mlp(x2, yf, o, attn_out_g[l][None, :], w_out[l].astype(BF16),
                      mlp_norm_g[l][None, :], w_mlp_in[l].astype(BF16), w_mlp_out[l].astype(BF16))
    return x2.reshape(batch, seq, D_MODEL)
```

```python
import functools
import math

import numpy as np
import jax
import jax.numpy as jnp
from jax import lax
from jax.experimental import pallas as pl
from jax.experimental.pallas import tpu as pltpu

D_MODEL = 1024
F_GROUPS = 4
F_GROUP_DIM = 64
F_WIDTH = F_GROUPS * F_GROUP_DIM
N_HEADS = 6
Q_LORA = 256
KV_LORA = 256
QK_NOPE = 128
QK_ROPE = 64
V_DIM = 128
QK_DIM = QK_NOPE + QK_ROPE
A_WIDTH = N_HEADS * V_DIM
ROPE_BASE = 10000.0
D_FF = 4 * D_MODEL
EPS = 1e-6

LANES = 128
HEAD_PAD = 2 * LANES
HALF = QK_ROPE // 2
Z_WIDTH = F_WIDTH + Q_LORA + KV_LORA + LANES
VMEM_LIMIT = 56 * 1024 * 1024

ROW_TILE = 512
PROJ_TILE = 1024
PROJ_SUB = 128
DFT_SPLIT = 64
Q_TILE = 256
FF_CHUNK = 1024

BF16 = jnp.bfloat16
F32 = jnp.float32


def _rope_tile_cols(start):
    cols = np.full((LANES,), -1, np.int64)
    cols[0:HALF] = start + np.arange(HALF)
    cols[2 * HALF:3 * HALF] = start + HALF + np.arange(HALF)
    return cols


def _gather_cols(w, cols):
    taken = jnp.take(w, jnp.asarray(np.maximum(cols, 0)), axis=-1)
    return jnp.where(jnp.asarray(cols >= 0), taken, jnp.zeros((), w.dtype))


def _rms(x, g):
    return x * lax.rsqrt(jnp.mean(x * x, axis=-1, keepdims=True) + EPS) * g


def _rotate(t, cos_t, sin_t):
    return t * cos_t + pltpu.roll(t, 2 * HALF, axis=1) * sin_t


def _fold_kernel(cc_ref, sc_ref, w_ref, out_ref):
    out_ref[...] = jnp.zeros_like(out_ref)
    for g in range(F_GROUPS):
        w = w_ref[0, g]
        a = jnp.dot(cc_ref[...], w, preferred_element_type=F32, precision=lax.Precision.HIGHEST)
        b = jnp.dot(sc_ref[...], w, preferred_element_type=F32, precision=lax.Precision.HIGHEST)
        lo, hi = g * F_GROUP_DIM, (g + 1) * F_GROUP_DIM
        out_ref[0, lo:hi, lo:hi] = a
        out_ref[0, lo:hi, F_WIDTH + lo:F_WIDTH + hi] = b


def _fold_fourier_weights(w_fourier, seq):
    depth = w_fourier.shape[0]
    c = np.arange(F_GROUP_DIM)
    ang = 2.0 * np.pi * ((c[:, None] * c[None, :]) % F_GROUP_DIM) / F_GROUP_DIM
    ortho = 1.0 / math.sqrt(seq * F_GROUP_DIM)
    cc = jnp.asarray(np.cos(ang) * ortho, F32)
    sc = jnp.asarray(np.sin(ang) * ortho, F32)
    return pl.pallas_call(
        _fold_kernel,
        grid=(depth,),
        in_specs=[
            pl.BlockSpec((F_GROUP_DIM, F_GROUP_DIM), lambda l: (0, 0)),
            pl.BlockSpec((F_GROUP_DIM, F_GROUP_DIM), lambda l: (0, 0)),
            pl.BlockSpec((1, F_GROUPS, F_GROUP_DIM, F_GROUP_DIM), lambda l: (l, 0, 0, 0)),
        ],
        out_specs=pl.BlockSpec((1, F_WIDTH, 2 * F_WIDTH), lambda l: (l, 0, 0)),
        out_shape=jax.ShapeDtypeStruct((depth, F_WIDTH, 2 * F_WIDTH), F32),
        name="fourier_fold",
    )(cc, sc, w_fourier)


def _trig_kernel(ang_ref, cos_ref, sin_ref):
    a = ang_ref[...]
    lane = lax.broadcasted_iota(jnp.int32, a.shape, 1)
    s = jnp.sin(a)
    cos_ref[...] = jnp.where(lane % (2 * HALF) < HALF, jnp.cos(a), 0.0)
    sin_ref[...] = jnp.where(lane < HALF, -s, jnp.where((lane >= 2 * HALF) & (lane < 3 * HALF), s, 0.0))


def _rotary_tables(positions):
    n = positions.size
    inv_freq = ROPE_BASE ** (-jnp.arange(HALF, dtype=F32) / HALF)
    ang = positions.astype(F32).reshape(n, 1) * jnp.tile(inv_freq, LANES // HALF)[None, :]
    tile = min(2048, n)
    spec = pl.BlockSpec((tile, LANES), lambda i: (i, 0))
    return pl.pallas_call(
        _trig_kernel,
        grid=(n // tile,),
        in_specs=[spec],
        out_specs=[spec, spec],
        out_shape=[jax.ShapeDtypeStruct((n, LANES), F32)] * 2,
        name="rotary_trig",
    )(ang)


def _unit_rms(x):
    return x * lax.rsqrt(jnp.mean(x * x, axis=-1, keepdims=True) + EPS)


def _proj_kernel(x_ref, win_ref, wab_ref, wq_ref, wkv_ref, qg_ref, kg_ref, cos_ref, sin_ref,
                 xab_ref, q_ref, k_ref, v_ref):
    qg = qg_ref[...]
    qg_nope, qg_rope = qg[:, :LANES], qg[:, LANES:]
    kg_rope = kg_ref[...]
    eps_sum = QK_DIM * EPS
    for j in range(x_ref.shape[0] // PROJ_SUB):
        rows = slice(j * PROJ_SUB, (j + 1) * PROJ_SUB)
        h = _unit_rms(x_ref[rows, :])
        z = jnp.dot(h.astype(BF16), win_ref[...], preferred_element_type=F32)
        xab_ref[rows, :] = jnp.dot(z[:, :F_WIDTH].astype(BF16), wab_ref[...],
                                   preferred_element_type=F32).astype(BF16)
        c_q = z[:, F_WIDTH:F_WIDTH + Q_LORA]
        c_kv = z[:, F_WIDTH + Q_LORA:F_WIDTH + Q_LORA + KV_LORA]
        k_pe = z[:, F_WIDTH + Q_LORA + KV_LORA:]
        qf = jnp.dot(_unit_rms(c_q).astype(BF16), wq_ref[...], preferred_element_type=F32)
        kvf = jnp.dot(_unit_rms(c_kv).astype(BF16), wkv_ref[...], preferred_element_type=F32)
        cos_t = cos_ref[rows, :]
        sin_t = sin_ref[rows, :]
        k_rot = _rotate(k_pe * kg_rope, cos_t, sin_t)
        ss_pe = jnp.sum(k_pe * k_pe, axis=-1, keepdims=True) + eps_sum
        for hd in range(N_HEADS):
            qh = qf[:, hd * HEAD_PAD:(hd + 1) * HEAD_PAD]
            rq = lax.rsqrt(jnp.sum(qh * qh, axis=-1, keepdims=True) + eps_sum)
            q_ref[hd, rows, :LANES] = (qh[:, :LANES] * rq * qg_nope).astype(BF16)
            q_ref[hd, rows, LANES:] = _rotate(qh[:, LANES:] * rq * qg_rope, cos_t, sin_t).astype(BF16)
            kn = kvf[:, hd * HEAD_PAD:hd * HEAD_PAD + LANES]
            rk = lax.rsqrt(jnp.sum(kn * kn, axis=-1, keepdims=True) + ss_pe)
            k_ref[hd, rows, :LANES] = (kn * rk).astype(BF16)
            k_ref[hd, rows, LANES:] = (k_rot * rk).astype(BF16)
            v_ref[hd, rows, :] = kvf[:, hd * HEAD_PAD + LANES:(hd + 1) * HEAD_PAD].astype(BF16)


def _const_spec(shape):
    return pl.BlockSpec(shape, lambda *_: (0,) * len(shape))


def _project(x2, w_in, w_ab, w_q, w_kv, qg, kg, cos_t, sin_t):
    n = x2.shape[0]
    tm = PROJ_TILE
    row = lambda w: pl.BlockSpec((tm, w), lambda i: (i, 0))
    head = lambda w: pl.BlockSpec((N_HEADS, tm, w), lambda i: (0, i, 0))
    return pl.pallas_call(
        _proj_kernel,
        grid=(n // tm,),
        in_specs=[
            row(D_MODEL), _const_spec((D_MODEL, Z_WIDTH)), _const_spec((F_WIDTH, 2 * F_WIDTH)),
            _const_spec((Q_LORA, N_HEADS * HEAD_PAD)), _const_spec((KV_LORA, N_HEADS * HEAD_PAD)),
            _const_spec((1, HEAD_PAD)), _const_spec((1, LANES)), row(LANES), row(LANES),
        ],
        out_specs=[row(2 * F_WIDTH), head(HEAD_PAD), head(HEAD_PAD), head(V_DIM)],
        out_shape=[
            jax.ShapeDtypeStruct((n, 2 * F_WIDTH), BF16),
            jax.ShapeDtypeStruct((N_HEADS, n, HEAD_PAD), BF16),
            jax.ShapeDtypeStruct((N_HEADS, n, HEAD_PAD), BF16),
            jax.ShapeDtypeStruct((N_HEADS, n, V_DIM), BF16),
        ],
        compiler_params=pltpu.CompilerParams(dimension_semantics=("arbitrary",),
                                             vmem_limit_bytes=VMEM_LIMIT),
        name="in_proj",
    )(x2, w_in, w_ab, w_q, w_kv, qg, kg, cos_t, sin_t)


def _fourier_kernel(c_ref, s_ref, xab_ref, g_ref, y_ref):
    xa = xab_ref[:, :F_WIDTH]
    xb = xab_ref[:, F_WIDTH:]
    y = (jnp.dot(c_ref[...], xa, preferred_element_type=F32)
         + jnp.dot(s_ref[...], xb, preferred_element_type=F32))
    y_ref[...] = _rms(y, g_ref[...]).astype(BF16)


def _fourier(xab, cmat, smat_neg, g, seq):
    n = xab.shape[0]
    return pl.pallas_call(
        _fourier_kernel,
        grid=(n // seq,),
        in_specs=[
            _const_spec((seq, seq)), _const_spec((seq, seq)),
            pl.BlockSpec((seq, 2 * F_WIDTH), lambda b: (b, 0)), _const_spec((1, F_WIDTH)),
        ],
        out_specs=pl.BlockSpec((seq, F_WIDTH), lambda b: (b, 0)),
        out_shape=jax.ShapeDtypeStruct((n, F_WIDTH), BF16),
        compiler_params=pltpu.CompilerParams(dimension_semantics=("arbitrary",),
                                             vmem_limit_bytes=VMEM_LIMIT),
        name="seq_dft",
    )(cmat, smat_neg, xab, g)


def _attn_kernel(q_ref, k_ref, v_ref, o_ref, vx_ref):
    vx_ref[:, :V_DIM] = v_ref[0]
    vx_ref[:, V_DIM:] = jnp.ones((vx_ref.shape[0], V_DIM), BF16)
    for j in range(q_ref.shape[1] // Q_TILE):
        rows = slice(j * Q_TILE, (j + 1) * Q_TILE)
        s = lax.dot_general(q_ref[0, rows, :], k_ref[0], (((1,), (1,)), ((), ())),
                            preferred_element_type=F32)
        m = jnp.max(s, axis=-1, keepdims=True)
        p = jnp.exp2(s - m).astype(BF16)
        ox = jnp.dot(p, vx_ref[...], preferred_element_type=F32)
        o_ref[rows, :] = (ox[:, :V_DIM] / ox[:, V_DIM:]).astype(BF16)


def _attention(q, k, v, batch, seq):
    n = q.shape[1]
    return pl.pallas_call(
        _attn_kernel,
        grid=(batch, N_HEADS),
        in_specs=[
            pl.BlockSpec((1, seq, HEAD_PAD), lambda b, h: (h, b, 0)),
            pl.BlockSpec((1, seq, HEAD_PAD), lambda b, h: (h, b, 0)),
            pl.BlockSpec((1, seq, V_DIM), lambda b, h: (h, b, 0)),
        ],
        out_specs=pl.BlockSpec((seq, V_DIM), lambda b, h: (b, h)),
        out_shape=jax.ShapeDtypeStruct((n, A_WIDTH), BF16),
        scratch_shapes=[pltpu.VMEM((seq, 2 * V_DIM), BF16)],
        compiler_params=pltpu.CompilerParams(
            dimension_semantics=("arbitrary", "arbitrary"),
            vmem_limit_bytes=VMEM_LIMIT),
        name="attention",
    )(q, k, v)


def _out_mlp_kernel(x_ref, yf_ref, o_ref, ag_ref, wo_ref, mg_ref, w1_ref, w2_ref, out_ref, act_ref):
    ya = _rms(o_ref[...].astype(F32), ag_ref[...]).astype(BF16)
    x1 = (x_ref[...]
          + jnp.dot(yf_ref[...], wo_ref[:F_WIDTH, :], preferred_element_type=F32)
          + jnp.dot(ya, wo_ref[F_WIDTH:, :], preferred_element_type=F32))
    hn = _rms(x1, mg_ref[...]).astype(BF16)
    for c in range(D_FF // FF_CHUNK):
        cols = slice(c * FF_CHUNK, (c + 1) * FF_CHUNK)
        hm = jnp.dot(hn, w1_ref[:, cols], preferred_element_type=F32)
        act_ref[:, cols] = jnp.square(jnp.maximum(hm, 0.0)).astype(BF16)
    out_ref[...] = x1 + jnp.dot(act_ref[...], w2_ref[...], preferred_element_type=F32)


def _out_mlp(x2, yf, o, ag, w_out, mg, w1, w2):
    n = x2.shape[0]
    tm = ROW_TILE
    row = lambda w: pl.BlockSpec((tm, w), lambda i: (i, 0))
    return pl.pallas_call(
        _out_mlp_kernel,
        grid=(n // tm,),
        in_specs=[
            row(D_MODEL), row(F_WIDTH), row(A_WIDTH), _const_spec((1, A_WIDTH)),
            _const_spec((D_MODEL, D_MODEL)), _const_spec((1, D_MODEL)),
            _const_spec((D_MODEL, D_FF)), _const_spec((D_FF, D_MODEL)),
        ],
        out_specs=row(D_MODEL),
        out_shape=jax.ShapeDtypeStruct((n, D_MODEL), F32),
        scratch_shapes=[pltpu.VMEM((tm, D_FF), BF16)],
        compiler_params=pltpu.CompilerParams(dimension_semantics=("arbitrary",),
                                             vmem_limit_bytes=VMEM_LIMIT),
        name="out_mlp",
    )(x2, yf, o, ag, w_out, mg, w1, w2)


def _dft_matrices(seq):
    lo_n = DFT_SPLIT
    hi_n = seq // lo_n
    k = np.arange(seq)[:, None]
    ang_hi = 2.0 * np.pi * ((k * lo_n * np.arange(hi_n)[None, :]) % seq) / seq
    ang_lo = 2.0 * np.pi * ((k * np.arange(lo_n)[None, :]) % seq) / seq
    ch, sh = jnp.asarray(np.cos(ang_hi), F32)[:, :, None], jnp.asarray(np.sin(ang_hi), F32)[:, :, None]
    cl, sl = jnp.asarray(np.cos(ang_lo), F32)[:, None, :], jnp.asarray(np.sin(ang_lo), F32)[:, None, :]
    cmat = (ch * cl - sh * sl).reshape(seq, seq)
    smat_neg = (-(sh * cl + ch * sl)).reshape(seq, seq)
    return cmat.astype(BF16), smat_neg.astype(BF16)


def kernel(x, positions, attn_norm_g, w_in, w_fourier, q_a_g, w_q_up, kv_a_g, w_kv_up, q_norm_g,
           k_norm_g, fourier_out_g, attn_out_g, w_out, mlp_norm_g, w_mlp_in, w_mlp_out):
    batch, seq, _ = x.shape
    depth = w_in.shape[0]
    n = batch * seq

    cos_t, sin_t = _rotary_tables(positions)
    cmat, smat_neg = _dft_matrices(seq)
    w_ab = _fold_fourier_weights(w_fourier, seq).astype(BF16)

    rope_cols = _rope_tile_cols(0)
    z_cols = np.concatenate([np.arange(F_WIDTH + Q_LORA + KV_LORA),
                             np.where(rope_cols >= 0, rope_cols + F_WIDTH + Q_LORA + KV_LORA, -1)])
    head_cols = np.concatenate([np.arange(QK_NOPE), _rope_tile_cols(QK_NOPE)])
    q_cols = np.concatenate([np.where(head_cols >= 0, head_cols + hd * QK_DIM, -1)
                             for hd in range(N_HEADS)])

    root = math.sqrt(QK_DIM)
    q_scale = math.log2(math.e) / root
    nope = np.arange(HEAD_PAD) < QK_NOPE

    x2 = x.reshape(n, D_MODEL)
    for l in range(depth):
        w_in_l = _gather_cols(w_in[l] * attn_norm_g[l][:, None], z_cols).astype(BF16)
        w_q_l = _gather_cols(w_q_up[l] * q_a_g[l][:, None], q_cols).astype(BF16)
        w_kv_l = (w_kv_up[l] * kv_a_g[l][:, None]).astype(BF16)
        qng = _gather_cols(q_norm_g[l][None, :], head_cols)
        kng = _gather_cols(k_norm_g[l][None, :], head_cols)
        qg = jnp.where(jnp.asarray(nope), qng * kng * (root * root * q_scale), qng * (root * q_scale))
        kg = kng[:, LANES:] * root
        xab, q, k, v = _project(x2, w_in_l, w_ab[l], w_q_l, w_kv_l, qg, kg, cos_t, sin_t)
        yf = _fourier(xab, cmat, smat_neg, fourier_out_g[l][None, :], seq)
        o = _attention(q, k, v, batch, seq)
        x2 = _out_mlp(x2, yf, o, attn_out_g[l][None, :], w_out[l].astype(BF16),
                      mlp_norm_g[l][None, :], w_mlp_in[l].astype(BF16), w_mlp_out[l].astype(BF16))
    return x2.reshape(batch, seq, D_MODEL)
```

```python
import functools
import math

import numpy as np
import jax
import jax.numpy as jnp
from jax import lax
from jax.experimental import pallas as pl
from jax.experimental.pallas import tpu as pltpu

D_MODEL = 1024
F_GROUPS = 4
F_GROUP_DIM = 64
F_WIDTH = F_GROUPS * F_GROUP_DIM
N_HEADS = 6
Q_LORA = 256
KV_LORA = 256
QK_NOPE = 128
QK_ROPE = 64
V_DIM = 128
QK_DIM = QK_NOPE + QK_ROPE
A_WIDTH = N_HEADS * V_DIM
ROPE_BASE = 10000.0
D_FF = 4 * D_MODEL
EPS = 1e-6

LANES = 128
HEAD_PAD = 2 * LANES
HALF = QK_ROPE // 2
Z_WIDTH = F_WIDTH + Q_LORA + KV_LORA + LANES
VMEM_LIMIT = 56 * 1024 * 1024

ROW_TILE = 512
PROJ_TILE = 1024
PROJ_ZROWS = 256
PROJ_SUB = 64
DFT_SPLIT = 64
Q_TILE = 256
ATTN_HEADS = 3
FF_CHUNK = 1024

BF16 = jnp.bfloat16
F32 = jnp.float32


def _rope_tile_cols(start):
    cols = np.full((LANES,), -1, np.int64)
    cols[0:HALF] = start + np.arange(HALF)
    cols[2 * HALF:3 * HALF] = start + HALF + np.arange(HALF)
    return cols


def _gather_cols(w, cols):
    taken = jnp.take(w, jnp.asarray(np.maximum(cols, 0)), axis=-1)
    return jnp.where(jnp.asarray(cols >= 0), taken, jnp.zeros((), w.dtype))


def _rms(x, g):
    return x * lax.rsqrt(jnp.mean(x * x, axis=-1, keepdims=True) + EPS) * g


def _rotate(t, cos_t, sin_t):
    return t * cos_t + pltpu.roll(t, 2 * HALF, axis=1) * sin_t


def _fold_kernel(cc_ref, sc_ref, w_ref, out_ref):
    out_ref[...] = jnp.zeros_like(out_ref)
    for g in range(F_GROUPS):
        w = w_ref[0, g]
        a = jnp.dot(cc_ref[...], w, preferred_element_type=F32, precision=lax.Precision.HIGHEST)
        b = jnp.dot(sc_ref[...], w, preferred_element_type=F32, precision=lax.Precision.HIGHEST)
        lo, hi = g * F_GROUP_DIM, (g + 1) * F_GROUP_DIM
        out_ref[0, lo:hi, lo:hi] = a
        out_ref[0, lo:hi, F_WIDTH + lo:F_WIDTH + hi] = b


def _fold_fourier_weights(w_fourier, seq):
    depth = w_fourier.shape[0]
    c = np.arange(F_GROUP_DIM)
    ang = 2.0 * np.pi * ((c[:, None] * c[None, :]) % F_GROUP_DIM) / F_GROUP_DIM
    ortho = 1.0 / math.sqrt(seq * F_GROUP_DIM)
    cc = jnp.asarray(np.cos(ang) * ortho, F32)
    sc = jnp.asarray(np.sin(ang) * ortho, F32)
    return pl.pallas_call(
        _fold_kernel,
        grid=(depth,),
        in_specs=[
            pl.BlockSpec((F_GROUP_DIM, F_GROUP_DIM), lambda l: (0, 0)),
            pl.BlockSpec((F_GROUP_DIM, F_GROUP_DIM), lambda l: (0, 0)),
            pl.BlockSpec((1, F_GROUPS, F_GROUP_DIM, F_GROUP_DIM), lambda l: (l, 0, 0, 0)),
        ],
        out_specs=pl.BlockSpec((1, F_WIDTH, 2 * F_WIDTH), lambda l: (l, 0, 0)),
        out_shape=jax.ShapeDtypeStruct((depth, F_WIDTH, 2 * F_WIDTH), F32),
        name="fourier_fold",
    )(cc, sc, w_fourier)


def _trig_kernel(ang_ref, cos_ref, sin_ref):
    a = ang_ref[...]
    lane = lax.broadcasted_iota(jnp.int32, a.shape, 1)
    s = jnp.sin(a)
    cos_ref[...] = jnp.where(lane % (2 * HALF) < HALF, jnp.cos(a), 0.0)
    sin_ref[...] = jnp.where(lane < HALF, -s, jnp.where((lane >= 2 * HALF) & (lane < 3 * HALF), s, 0.0))


def _rotary_tables(positions):
    n = positions.size
    inv_freq = ROPE_BASE ** (-jnp.arange(HALF, dtype=F32) / HALF)
    ang = positions.astype(F32).reshape(n, 1) * jnp.tile(inv_freq, LANES // HALF)[None, :]
    tile = min(2048, n)
    spec = pl.BlockSpec((tile, LANES), lambda i: (i, 0))
    return pl.pallas_call(
        _trig_kernel,
        grid=(n // tile,),
        in_specs=[spec],
        out_specs=[spec, spec],
        out_shape=[jax.ShapeDtypeStruct((n, LANES), F32)] * 2,
        name="rotary_trig",
    )(ang)


def _unit_rms(x):
    return x * lax.rsqrt(jnp.mean(x * x, axis=-1, keepdims=True) + EPS)


def _proj_kernel(x_ref, win_ref, wab_ref, wq_ref, wkv_ref, qg_ref, kg_ref, cos_ref, sin_ref,
                 xab_ref, q_ref, k_ref, v_ref):
    qg = qg_ref[...]
    qg_nope, qg_rope = qg[:, :LANES], qg[:, LANES:]
    kg_rope = kg_ref[...]
    eps_sum = QK_DIM * EPS
    for i in range(x_ref.shape[0] // PROJ_ZROWS):
        zrows = slice(i * PROJ_ZROWS, (i + 1) * PROJ_ZROWS)
        h = _unit_rms(x_ref[zrows, :])
        z = jnp.dot(h.astype(BF16), win_ref[...], preferred_element_type=F32)
        xab_ref[zrows, :] = jnp.dot(z[:, :F_WIDTH].astype(BF16), wab_ref[...],
                                    preferred_element_type=F32).astype(BF16)
        cqn = _unit_rms(z[:, F_WIDTH:F_WIDTH + Q_LORA]).astype(BF16)
        ckvn = _unit_rms(z[:, F_WIDTH + Q_LORA:F_WIDTH + Q_LORA + KV_LORA]).astype(BF16)
        k_pe_all = z[:, F_WIDTH + Q_LORA + KV_LORA:]
        for j in range(PROJ_ZROWS // PROJ_SUB):
            sub = slice(j * PROJ_SUB, (j + 1) * PROJ_SUB)
            rows = slice(i * PROJ_ZROWS + j * PROJ_SUB, i * PROJ_ZROWS + (j + 1) * PROJ_SUB)
            qf = jnp.dot(cqn[sub, :], wq_ref[...], preferred_element_type=F32)
            kvf = jnp.dot(ckvn[sub, :], wkv_ref[...], preferred_element_type=F32)
            k_pe = k_pe_all[sub, :]
            cos_t = cos_ref[rows, :]
            sin_t = sin_ref[rows, :]
            k_rot = _rotate(k_pe * kg_rope, cos_t, sin_t)
            ss_pe = jnp.sum(k_pe * k_pe, axis=-1, keepdims=True) + eps_sum
            for hd in range(N_HEADS):
                qh = qf[:, hd * HEAD_PAD:(hd + 1) * HEAD_PAD]
                rq = lax.rsqrt(jnp.sum(qh * qh, axis=-1, keepdims=True) + eps_sum)
                q_ref[hd, rows, :LANES] = (qh[:, :LANES] * rq * qg_nope).astype(BF16)
                q_ref[hd, rows, LANES:] = _rotate(qh[:, LANES:] * rq * qg_rope, cos_t, sin_t).astype(BF16)
                kn = kvf[:, hd * HEAD_PAD:hd * HEAD_PAD + LANES]
                rk = lax.rsqrt(jnp.sum(kn * kn, axis=-1, keepdims=True) + ss_pe)
                k_ref[hd, rows, :LANES] = (kn * rk).astype(BF16)
                k_ref[hd, rows, LANES:] = (k_rot * rk).astype(BF16)
                v_ref[hd, rows, :] = kvf[:, hd * HEAD_PAD + LANES:(hd + 1) * HEAD_PAD].astype(BF16)


def _const_spec(shape):
    return pl.BlockSpec(shape, lambda *_: (0,) * len(shape))


def _project(x2, w_in, w_ab, w_q, w_kv, qg, kg, cos_t, sin_t):
    n = x2.shape[0]
    tm = PROJ_TILE
    row = lambda w: pl.BlockSpec((tm, w), lambda i: (i, 0))
    head = lambda w: pl.BlockSpec((N_HEADS, tm, w), lambda i: (0, i, 0))
    return pl.pallas_call(
        _proj_kernel,
        grid=(n // tm,),
        in_specs=[
            row(D_MODEL), _const_spec((D_MODEL, Z_WIDTH)), _const_spec((F_WIDTH, 2 * F_WIDTH)),
            _const_spec((Q_LORA, N_HEADS * HEAD_PAD)), _const_spec((KV_LORA, N_HEADS * HEAD_PAD)),
            _const_spec((1, HEAD_PAD)), _const_spec((1, LANES)), row(LANES), row(LANES),
        ],
        out_specs=[row(2 * F_WIDTH), head(HEAD_PAD), head(HEAD_PAD), head(V_DIM)],
        out_shape=[
            jax.ShapeDtypeStruct((n, 2 * F_WIDTH), BF16),
            jax.ShapeDtypeStruct((N_HEADS, n, HEAD_PAD), BF16),
            jax.ShapeDtypeStruct((N_HEADS, n, HEAD_PAD), BF16),
            jax.ShapeDtypeStruct((N_HEADS, n, V_DIM), BF16),
        ],
        compiler_params=pltpu.CompilerParams(dimension_semantics=("arbitrary",),
                                             vmem_limit_bytes=VMEM_LIMIT),
        name="in_proj",
    )(x2, w_in, w_ab, w_q, w_kv, qg, kg, cos_t, sin_t)


def _fourier_kernel(c_ref, s_ref, xab_ref, g_ref, y_ref):
    xa = xab_ref[:, :F_WIDTH]
    xb = xab_ref[:, F_WIDTH:]
    y = (jnp.dot(c_ref[...], xa, preferred_element_type=F32)
         + jnp.dot(s_ref[...], xb, preferred_element_type=F32))
    y_ref[...] = _rms(y, g_ref[...]).astype(BF16)


def _fourier(xab, cmat, smat_neg, g, seq):
    n = xab.shape[0]
    return pl.pallas_call(
        _fourier_kernel,
        grid=(n // seq,),
        in_specs=[
            _const_spec((seq, seq)), _const_spec((seq, seq)),
            pl.BlockSpec((seq, 2 * F_WIDTH), lambda b: (b, 0)), _const_spec((1, F_WIDTH)),
        ],
        out_specs=pl.BlockSpec((seq, F_WIDTH), lambda b: (b, 0)),
        out_shape=jax.ShapeDtypeStruct((n, F_WIDTH), BF16),
        compiler_params=pltpu.CompilerParams(dimension_semantics=("arbitrary",),
                                             vmem_limit_bytes=VMEM_LIMIT),
        name="seq_dft",
    )(cmat, smat_neg, xab, g)


def _attn_kernel(q_ref, k_ref, v_ref, o_ref, vx_ref):
    seq = q_ref.shape[1]
    for hd in range(ATTN_HEADS):
        vx_ref[hd, :, :V_DIM] = v_ref[hd]
        vx_ref[hd, :, V_DIM:] = jnp.ones((seq, V_DIM), BF16)
    for hd in range(ATTN_HEADS):
        for j in range(seq // Q_TILE):
            rows = slice(j * Q_TILE, (j + 1) * Q_TILE)
            s = lax.dot_general(q_ref[hd, rows, :], k_ref[hd], (((1,), (1,)), ((), ())),
                                preferred_element_type=F32)
            m = jnp.max(s, axis=-1, keepdims=True)
            p = jnp.exp2(s - m).astype(BF16)
            ox = jnp.dot(p, vx_ref[hd], preferred_element_type=F32)
            o_ref[rows, hd * V_DIM:(hd + 1) * V_DIM] = (ox[:, :V_DIM] / ox[:, V_DIM:]).astype(BF16)


def _attention(q, k, v, batch, seq):
    n = q.shape[1]
    g = ATTN_HEADS
    return pl.pallas_call(
        _attn_kernel,
        grid=(batch, N_HEADS // g),
        in_specs=[
            pl.BlockSpec((g, seq, HEAD_PAD), lambda b, h: (h, b, 0)),
            pl.BlockSpec((g, seq, HEAD_PAD), lambda b, h: (h, b, 0)),
            pl.BlockSpec((g, seq, V_DIM), lambda b, h: (h, b, 0)),
        ],
        out_specs=pl.BlockSpec((seq, g * V_DIM), lambda b, h: (b, h)),
        out_shape=jax.ShapeDtypeStruct((n, A_WIDTH), BF16),
        scratch_shapes=[pltpu.VMEM((g, seq, 2 * V_DIM), BF16)],
        compiler_params=pltpu.CompilerParams(
            dimension_semantics=("arbitrary", "arbitrary"),
            vmem_limit_bytes=VMEM_LIMIT),
        name="attention",
    )(q, k, v)


def _out_mlp_kernel(x_ref, yf_ref, o_ref, ag_ref, wo_ref, mg_ref, w1_ref, w2_ref, out_ref, act_ref):
    ya = _rms(o_ref[...].astype(F32), ag_ref[...]).astype(BF16)
    x1 = (x_ref[...]
          + jnp.dot(yf_ref[...], wo_ref[:F_WIDTH, :], preferred_element_type=F32)
          + jnp.dot(ya, wo_ref[F_WIDTH:, :], preferred_element_type=F32))
    hn = _rms(x1, mg_ref[...]).astype(BF16)
    for c in range(D_FF // FF_CHUNK):
        cols = slice(c * FF_CHUNK, (c + 1) * FF_CHUNK)
        hm = jnp.dot(hn, w1_ref[:, cols], preferred_element_type=F32)
        act_ref[:, cols] = jnp.square(jnp.maximum(hm, 0.0)).astype(BF16)
    out_ref[...] = x1 + jnp.dot(act_ref[...], w2_ref[...], preferred_element_type=F32)


def _out_mlp(x2, yf, o, ag, w_out, mg, w1, w2):
    n = x2.shape[0]
    tm = ROW_TILE
    row = lambda w: pl.BlockSpec((tm, w), lambda i: (i, 0))
    return pl.pallas_call(
        _out_mlp_kernel,
        grid=(n // tm,),
        in_specs=[
            row(D_MODEL), row(F_WIDTH), row(A_WIDTH), _const_spec((1, A_WIDTH)),
            _const_spec((D_MODEL, D_MODEL)), _const_spec((1, D_MODEL)),
            _const_spec((D_MODEL, D_FF)), _const_spec((D_FF, D_MODEL)),
        ],
        out_specs=row(D_MODEL),
        out_shape=jax.ShapeDtypeStruct((n, D_MODEL), F32),
        scratch_shapes=[pltpu.VMEM((tm, D_FF), BF16)],
        compiler_params=pltpu.CompilerParams(dimension_semantics=("arbitrary",),
                                             vmem_limit_bytes=VMEM_LIMIT),
        name="out_mlp",
    )(x2, yf, o, ag, w_out, mg, w1, w2)


def _dft_matrices(seq):
    lo_n = DFT_SPLIT
    hi_n = seq // lo_n
    k = np.arange(seq)[None, :]
    ang_hi = 2.0 * np.pi * ((k * lo_n * np.arange(hi_n)[:, None]) % seq) / seq
    ang_lo = 2.0 * np.pi * ((k * np.arange(lo_n)[:, None]) % seq) / seq
    ch, sh = jnp.asarray(np.cos(ang_hi), F32)[:, None, :], jnp.asarray(np.sin(ang_hi), F32)[:, None, :]
    cl, sl = jnp.asarray(np.cos(ang_lo), F32)[None, :, :], jnp.asarray(np.sin(ang_lo), F32)[None, :, :]
    cmat = (ch * cl - sh * sl).astype(BF16).reshape(seq, seq)
    smat_neg = (-(sh * cl + ch * sl)).astype(BF16).reshape(seq, seq)
    return cmat, smat_neg


def kernel(x, positions, attn_norm_g, w_in, w_fourier, q_a_g, w_q_up, kv_a_g, w_kv_up, q_norm_g,
           k_norm_g, fourier_out_g, attn_out_g, w_out, mlp_norm_g, w_mlp_in, w_mlp_out):
    batch, seq, _ = x.shape
    depth = w_in.shape[0]
    n = batch * seq

    cos_t, sin_t = _rotary_tables(positions)
    cmat, smat_neg = _dft_matrices(seq)
    w_ab = _fold_fourier_weights(w_fourier, seq).astype(BF16)

    rope_cols = _rope_tile_cols(0)
    z_cols = np.concatenate([np.arange(F_WIDTH + Q_LORA + KV_LORA),
                             np.where(rope_cols >= 0, rope_cols + F_WIDTH + Q_LORA + KV_LORA, -1)])
    head_cols = np.concatenate([np.arange(QK_NOPE), _rope_tile_cols(QK_NOPE)])
    q_cols = np.concatenate([np.where(head_cols >= 0, head_cols + hd * QK_DIM, -1)
                             for hd in range(N_HEADS)])

    root = math.sqrt(QK_DIM)
    q_scale = math.log2(math.e) / root
    nope = np.arange(HEAD_PAD) < QK_NOPE

    x2 = x.reshape(n, D_MODEL)
    for l in range(depth):
        w_in_l = _gather_cols(w_in[l] * attn_norm_g[l][:, None], z_cols).astype(BF16)
        w_q_l = _gather_cols(w_q_up[l] * q_a_g[l][:, None], q_cols).astype(BF16)
        w_kv_l = (w_kv_up[l] * kv_a_g[l][:, None]).astype(BF16)
        qng = _gather_cols(q_norm_g[l][None, :], head_cols)
        kng = _gather_cols(k_norm_g[l][None, :], head_cols)
        qg = jnp.where(jnp.asarray(nope), qng * kng * (root * root * q_scale), qng * (root * q_scale))
        kg = kng[:, LANES:] * root
        xab, q, k, v = _project(x2, w_in_l, w_ab[l], w_q_l, w_kv_l, qg, kg, cos_t, sin_t)
        yf = _fourier(xab, cmat, smat_neg, fourier_out_g[l][None, :], seq)
        o = _attention(q, k, v, batch, seq)
        x2 = _out_mlp(x2, yf, o, attn_out_g[l][None, :], w_out[l].astype(BF16),
                      mlp_norm_g[l][None, :], w_mlp_in[l].astype(BF16), w_mlp_out[l].astype(BF16))
    return x2.reshape(batch, seq, D_MODEL)
```

```python
import functools
import math

import numpy as np
import jax
import jax.numpy as jnp
from jax import lax
from jax.experimental import pallas as pl
from jax.experimental.pallas import tpu as pltpu

D_MODEL = 1024
F_GROUPS = 4
F_GROUP_DIM = 64
F_WIDTH = F_GROUPS * F_GROUP_DIM
N_HEADS = 6
Q_LORA = 256
KV_LORA = 256
QK_NOPE = 128
QK_ROPE = 64
V_DIM = 128
QK_DIM = QK_NOPE + QK_ROPE
A_WIDTH = N_HEADS * V_DIM
ROPE_BASE = 10000.0
D_FF = 4 * D_MODEL
EPS = 1e-6

LANES = 128
HEAD_PAD = 2 * LANES
HALF = QK_ROPE // 2
Z_WIDTH = F_WIDTH + Q_LORA + KV_LORA + LANES
VMEM_LIMIT = 56 * 1024 * 1024

ROW_TILE = 512
PROJ_TILE = 512
PROJ_ZROWS = 512
PROJ_SUB = 512
DFT_SPLIT = 64
Q_TILE = 256
ATTN_HEADS = 3
FF_CHUNK = 1024

BF16 = jnp.bfloat16
F32 = jnp.float32


def _rope_tile_cols(start):
    cols = np.full((LANES,), -1, np.int64)
    cols[0:HALF] = start + np.arange(HALF)
    cols[2 * HALF:3 * HALF] = start + HALF + np.arange(HALF)
    return cols


def _gather_cols(w, cols):
    taken = jnp.take(w, jnp.asarray(np.maximum(cols, 0)), axis=-1)
    return jnp.where(jnp.asarray(cols >= 0), taken, jnp.zeros((), w.dtype))


def _rms(x, g):
    return x * lax.rsqrt(jnp.mean(x * x, axis=-1, keepdims=True) + EPS) * g


def _rotate(t, cos_t, sin_t):
    return t * cos_t + pltpu.roll(t, 2 * HALF, axis=1) * sin_t


def _fold_kernel(cc_ref, sc_ref, w_ref, out_ref):
    out_ref[...] = jnp.zeros_like(out_ref)
    for g in range(F_GROUPS):
        w = w_ref[0, g]
        a = jnp.dot(cc_ref[...], w, preferred_element_type=F32, precision=lax.Precision.HIGHEST)
        b = jnp.dot(sc_ref[...], w, preferred_element_type=F32, precision=lax.Precision.HIGHEST)
        lo, hi = g * F_GROUP_DIM, (g + 1) * F_GROUP_DIM
        out_ref[0, lo:hi, lo:hi] = a
        out_ref[0, lo:hi, F_WIDTH + lo:F_WIDTH + hi] = b


def _fold_fourier_weights(w_fourier, seq):
    depth = w_fourier.shape[0]
    c = np.arange(F_GROUP_DIM)
    ang = 2.0 * np.pi * ((c[:, None] * c[None, :]) % F_GROUP_DIM) / F_GROUP_DIM
    ortho = 1.0 / math.sqrt(seq * F_GROUP_DIM)
    cc = jnp.asarray(np.cos(ang) * ortho, F32)
    sc = jnp.asarray(np.sin(ang) * ortho, F32)
    return pl.pallas_call(
        _fold_kernel,
        grid=(depth,),
        in_specs=[
            pl.BlockSpec((F_GROUP_DIM, F_GROUP_DIM), lambda l: (0, 0)),
            pl.BlockSpec((F_GROUP_DIM, F_GROUP_DIM), lambda l: (0, 0)),
            pl.BlockSpec((1, F_GROUPS, F_GROUP_DIM, F_GROUP_DIM), lambda l: (l, 0, 0, 0)),
        ],
        out_specs=pl.BlockSpec((1, F_WIDTH, 2 * F_WIDTH), lambda l: (l, 0, 0)),
        out_shape=jax.ShapeDtypeStruct((depth, F_WIDTH, 2 * F_WIDTH), F32),
        name="fourier_fold",
    )(cc, sc, w_fourier)


def _trig_kernel(ang_ref, cos_ref, sin_ref):
    a = ang_ref[...]
    lane = lax.broadcasted_iota(jnp.int32, a.shape, 1)
    s = jnp.sin(a)
    cos_ref[...] = jnp.where(lane % (2 * HALF) < HALF, jnp.cos(a), 0.0)
    sin_ref[...] = jnp.where(lane < HALF, -s, jnp.where((lane >= 2 * HALF) & (lane < 3 * HALF), s, 0.0))


def _rotary_tables(positions):
    n = positions.size
    inv_freq = ROPE_BASE ** (-jnp.arange(HALF, dtype=F32) / HALF)
    ang = positions.astype(F32).reshape(n, 1) * jnp.tile(inv_freq, LANES // HALF)[None, :]
    tile = min(2048, n)
    spec = pl.BlockSpec((tile, LANES), lambda i: (i, 0))
    return pl.pallas_call(
        _trig_kernel,
        grid=(n // tile,),
        in_specs=[spec],
        out_specs=[spec, spec],
        out_shape=[jax.ShapeDtypeStruct((n, LANES), F32)] * 2,
        name="rotary_trig",
    )(ang)


def _unit_rms(x):
    return x * lax.rsqrt(jnp.mean(x * x, axis=-1, keepdims=True) + EPS)


def _proj_matmul_stage(x_ref, win_ref, wab_ref, wq_ref, wkv_ref, xab_ref, qf_ref, kvf_ref, kpe_ref):
    for i in range(x_ref.shape[0] // PROJ_ZROWS):
        rows = slice(i * PROJ_ZROWS, (i + 1) * PROJ_ZROWS)
        h = _unit_rms(x_ref[rows, :])
        z = jnp.dot(h.astype(BF16), win_ref[...], preferred_element_type=F32)
        xab_ref[rows, :] = jnp.dot(z[:, :F_WIDTH].astype(BF16), wab_ref[...],
                                   preferred_element_type=F32).astype(BF16)
        cqn = _unit_rms(z[:, F_WIDTH:F_WIDTH + Q_LORA]).astype(BF16)
        ckvn = _unit_rms(z[:, F_WIDTH + Q_LORA:F_WIDTH + Q_LORA + KV_LORA]).astype(BF16)
        kpe_ref[rows, :] = z[:, F_WIDTH + Q_LORA + KV_LORA:]
        qf_ref[rows, :] = jnp.dot(cqn, wq_ref[...], preferred_element_type=F32)
        kvf_ref[rows, :] = jnp.dot(ckvn, wkv_ref[...], preferred_element_type=F32)


def _proj_norm_stage(qf_ref, kvf_ref, kpe_ref, qg_ref, kg_ref, cos_ref, sin_ref, q_ref, k_ref, v_ref):
    qg = qg_ref[...]
    qg_nope, qg_rope = qg[:, :LANES], qg[:, LANES:]
    kg_rope = kg_ref[...]
    eps_sum = QK_DIM * EPS
    for j in range(qf_ref.shape[0] // PROJ_SUB):
        rows = slice(j * PROJ_SUB, (j + 1) * PROJ_SUB)
        k_pe = kpe_ref[rows, :]
        cos_t = cos_ref[rows, :]
        sin_t = sin_ref[rows, :]
        k_rot = _rotate(k_pe * kg_rope, cos_t, sin_t)
        ss_pe = jnp.sum(k_pe * k_pe, axis=-1, keepdims=True) + eps_sum
        for hd in range(N_HEADS):
            qh = qf_ref[rows, hd * HEAD_PAD:(hd + 1) * HEAD_PAD]
            rq = lax.rsqrt(jnp.sum(qh * qh, axis=-1, keepdims=True) + eps_sum)
            q_ref[hd, rows, :LANES] = (qh[:, :LANES] * rq * qg_nope).astype(BF16)
            q_ref[hd, rows, LANES:] = _rotate(qh[:, LANES:] * rq * qg_rope, cos_t, sin_t).astype(BF16)
            kn = kvf_ref[rows, hd * HEAD_PAD:hd * HEAD_PAD + LANES]
            rk = lax.rsqrt(jnp.sum(kn * kn, axis=-1, keepdims=True) + ss_pe)
            k_ref[hd, rows, :LANES] = (kn * rk).astype(BF16)
            k_ref[hd, rows, LANES:] = (k_rot * rk).astype(BF16)
            v_ref[hd, rows, :] = kvf_ref[rows, hd * HEAD_PAD + LANES:(hd + 1) * HEAD_PAD].astype(BF16)


def _proj_kernel(x_ref, win_ref, wab_ref, wq_ref, wkv_ref, qg_ref, kg_ref, cos_ref, sin_ref,
                 xab_ref, q_ref, k_ref, v_ref, qf0, kvf0, kpe0, qf1, kvf1, kpe1):
    step = pl.program_id(0)

    @pl.when(step == 0)
    def _():
        qf1[...] = jnp.zeros_like(qf1)
        kvf1[...] = jnp.zeros_like(kvf1)
        kpe1[...] = jnp.zeros_like(kpe1)

    def body(fill, drain):
        _proj_matmul_stage(x_ref, win_ref, wab_ref, wq_ref, wkv_ref, xab_ref, *fill)
        _proj_norm_stage(*drain, qg_ref, kg_ref, cos_ref, sin_ref, q_ref, k_ref, v_ref)

    @pl.when(step % 2 == 0)
    def _():
        body((qf0, kvf0, kpe0), (qf1, kvf1, kpe1))

    @pl.when(step % 2 == 1)
    def _():
        body((qf1, kvf1, kpe1), (qf0, kvf0, kpe0))


def _const_spec(shape):
    return pl.BlockSpec(shape, lambda *_: (0,) * len(shape))


def _project(x2, w_in, w_ab, w_q, w_kv, qg, kg, cos_t, sin_t):
    n = x2.shape[0]
    tm = PROJ_TILE
    nt = n // tm
    fill = lambda w: pl.BlockSpec((tm, w), lambda i: (jnp.minimum(i, nt - 1), 0))
    drain = lambda w: pl.BlockSpec((tm, w), lambda i: (jnp.maximum(i - 1, 0), 0))
    head = lambda w: pl.BlockSpec((N_HEADS, tm, w), lambda i: (0, jnp.maximum(i - 1, 0), 0))
    up_width = N_HEADS * HEAD_PAD
    handoff = [pltpu.VMEM((tm, up_width), F32), pltpu.VMEM((tm, up_width), F32), pltpu.VMEM((tm, LANES), F32)]
    return pl.pallas_call(
        _proj_kernel,
        grid=(nt + 1,),
        in_specs=[
            fill(D_MODEL), _const_spec((D_MODEL, Z_WIDTH)), _const_spec((F_WIDTH, 2 * F_WIDTH)),
            _const_spec((Q_LORA, up_width)), _const_spec((KV_LORA, up_width)),
            _const_spec((1, HEAD_PAD)), _const_spec((1, LANES)), drain(LANES), drain(LANES),
        ],
        out_specs=[fill(2 * F_WIDTH), head(HEAD_PAD), head(HEAD_PAD), head(V_DIM)],
        out_shape=[
            jax.ShapeDtypeStruct((n, 2 * F_WIDTH), BF16),
            jax.ShapeDtypeStruct((N_HEADS, n, HEAD_PAD), BF16),
            jax.ShapeDtypeStruct((N_HEADS, n, HEAD_PAD), BF16),
            jax.ShapeDtypeStruct((N_HEADS, n, V_DIM), BF16),
        ],
        scratch_shapes=handoff + handoff,
        compiler_params=pltpu.CompilerParams(dimension_semantics=("arbitrary",),
                                             vmem_limit_bytes=VMEM_LIMIT),
        name="in_proj",
    )(x2, w_in, w_ab, w_q, w_kv, qg, kg, cos_t, sin_t)


def _fourier_kernel(c_ref, s_ref, xab_ref, g_ref, y_ref):
    xa = xab_ref[:, :F_WIDTH]
    xb = xab_ref[:, F_WIDTH:]
    y = (jnp.dot(c_ref[...], xa, preferred_element_type=F32)
         + jnp.dot(s_ref[...], xb, preferred_element_type=F32))
    y_ref[...] = _rms(y, g_ref[...]).astype(BF16)


def _fourier(xab, cmat, smat_neg, g, seq):
    n = xab.shape[0]
    return pl.pallas_call(
        _fourier_kernel,
        grid=(n // seq,),
        in_specs=[
            _const_spec((seq, seq)), _const_spec((seq, seq)),
            pl.BlockSpec((seq, 2 * F_WIDTH), lambda b: (b, 0)), _const_spec((1, F_WIDTH)),
        ],
        out_specs=pl.BlockSpec((seq, F_WIDTH), lambda b: (b, 0)),
        out_shape=jax.ShapeDtypeStruct((n, F_WIDTH), BF16),
        compiler_params=pltpu.CompilerParams(dimension_semantics=("arbitrary",),
                                             vmem_limit_bytes=VMEM_LIMIT),
        name="seq_dft",
    )(cmat, smat_neg, xab, g)


def _attn_kernel(q_ref, k_ref, v_ref, o_ref, vx_ref):
    seq = q_ref.shape[1]
    for hd in range(ATTN_HEADS):
        vx_ref[hd, :, :V_DIM] = v_ref[hd]
        vx_ref[hd, :, V_DIM:] = jnp.ones((seq, V_DIM), BF16)
    for hd in range(ATTN_HEADS):
        for j in range(seq // Q_TILE):
            rows = slice(j * Q_TILE, (j + 1) * Q_TILE)
            s = lax.dot_general(q_ref[hd, rows, :], k_ref[hd], (((1,), (1,)), ((), ())),
                                preferred_element_type=F32)
            m = jnp.max(s, axis=-1, keepdims=True)
            p = jnp.exp2(s - m).astype(BF16)
            ox = jnp.dot(p, vx_ref[hd], preferred_element_type=F32)
            o_ref[rows, hd * V_DIM:(hd + 1) * V_DIM] = (ox[:, :V_DIM] / ox[:, V_DIM:]).astype(BF16)


def _attention(q, k, v, batch, seq):
    n = q.shape[1]
    g = ATTN_HEADS
    return pl.pallas_call(
        _attn_kernel,
        grid=(batch, N_HEADS // g),
        in_specs=[
            pl.BlockSpec((g, seq, HEAD_PAD), lambda b, h: (h, b, 0)),
            pl.BlockSpec((g, seq, HEAD_PAD), lambda b, h: (h, b, 0)),
            pl.BlockSpec((g, seq, V_DIM), lambda b, h: (h, b, 0)),
        ],
        out_specs=pl.BlockSpec((seq, g * V_DIM), lambda b, h: (b, h)),
        out_shape=jax.ShapeDtypeStruct((n, A_WIDTH), BF16),
        scratch_shapes=[pltpu.VMEM((g, seq, 2 * V_DIM), BF16)],
        compiler_params=pltpu.CompilerParams(
            dimension_semantics=("arbitrary", "arbitrary"),
            vmem_limit_bytes=VMEM_LIMIT),
        name="attention",
    )(q, k, v)


def _out_mlp_kernel(x_ref, yf_ref, o_ref, ag_ref, wo_ref, mg_ref, w1_ref, w2_ref, out_ref, act_ref):
    ya = _rms(o_ref[...].astype(F32), ag_ref[...]).astype(BF16)
    x1 = (x_ref[...]
          + jnp.dot(yf_ref[...], wo_ref[:F_WIDTH, :], preferred_element_type=F32)
          + jnp.dot(ya, wo_ref[F_WIDTH:, :], preferred_element_type=F32))
    hn = _rms(x1, mg_ref[...]).astype(BF16)
    for c in range(D_FF // FF_CHUNK):
        cols = slice(c * FF_CHUNK, (c + 1) * FF_CHUNK)
        hm = jnp.dot(hn, w1_ref[:, cols], preferred_element_type=F32)
        act_ref[:, cols] = jnp.square(jnp.maximum(hm, 0.0)).astype(BF16)
    out_ref[...] = x1 + jnp.dot(act_ref[...], w2_ref[...], preferred_element_type=F32)


def _out_mlp(x2, yf, o, ag, w_out, mg, w1, w2):
    n = x2.shape[0]
    tm = ROW_TILE
    row = lambda w: pl.BlockSpec((tm, w), lambda i: (i, 0))
    return pl.pallas_call(
        _out_mlp_kernel,
        grid=(n // tm,),
        in_specs=[
            row(D_MODEL), row(F_WIDTH), row(A_WIDTH), _const_spec((1, A_WIDTH)),
            _const_spec((D_MODEL, D_MODEL)), _const_spec((1, D_MODEL)),
            _const_spec((D_MODEL, D_FF)), _const_spec((D_FF, D_MODEL)),
        ],
        out_specs=row(D_MODEL),
        out_shape=jax.ShapeDtypeStruct((n, D_MODEL), F32),
        scratch_shapes=[pltpu.VMEM((tm, D_FF), BF16)],
        compiler_params=pltpu.CompilerParams(dimension_semantics=("arbitrary",),
                                             vmem_limit_bytes=VMEM_LIMIT),
        name="out_mlp",
    )(x2, yf, o, ag, w_out, mg, w1, w2)


def _dft_matrices(seq):
    lo_n = DFT_SPLIT
    hi_n = seq // lo_n
    k = np.arange(seq)[None, :]
    ang_hi = 2.0 * np.pi * ((k * lo_n * np.arange(hi_n)[:, None]) % seq) / seq
    ang_lo = 2.0 * np.pi * ((k * np.arange(lo_n)[:, None]) % seq) / seq
    ch, sh = jnp.asarray(np.cos(ang_hi), F32)[:, None, :], jnp.asarray(np.sin(ang_hi), F32)[:, None, :]
    cl, sl = jnp.asarray(np.cos(ang_lo), F32)[None, :, :], jnp.asarray(np.sin(ang_lo), F32)[None, :, :]
    cmat = (ch * cl - sh * sl).astype(BF16).reshape(seq, seq)
    smat_neg = (-(sh * cl + ch * sl)).astype(BF16).reshape(seq, seq)
    return cmat, smat_neg


def kernel(x, positions, attn_norm_g, w_in, w_fourier, q_a_g, w_q_up, kv_a_g, w_kv_up, q_norm_g,
           k_norm_g, fourier_out_g, attn_out_g, w_out, mlp_norm_g, w_mlp_in, w_mlp_out):
    batch, seq, _ = x.shape
    depth = w_in.shape[0]
    n = batch * seq

    cos_t, sin_t = _rotary_tables(positions)
    cmat, smat_neg = _dft_matrices(seq)
    w_ab = _fold_fourier_weights(w_fourier, seq).astype(BF16)

    rope_cols = _rope_tile_cols(0)
    z_cols = np.concatenate([np.arange(F_WIDTH + Q_LORA + KV_LORA),
                             np.where(rope_cols >= 0, rope_cols + F_WIDTH + Q_LORA + KV_LORA, -1)])
    head_cols = np.concatenate([np.arange(QK_NOPE), _rope_tile_cols(QK_NOPE)])
    q_cols = np.concatenate([np.where(head_cols >= 0, head_cols + hd * QK_DIM, -1)
                             for hd in range(N_HEADS)])

    root = math.sqrt(QK_DIM)
    q_scale = math.log2(math.e) / root
    nope = np.arange(HEAD_PAD) < QK_NOPE

    x2 = x.reshape(n, D_MODEL)
    for l in range(depth):
        w_in_l = _gather_cols(w_in[l] * attn_norm_g[l][:, None], z_cols).astype(BF16)
        w_q_l = _gather_cols(w_q_up[l] * q_a_g[l][:, None], q_cols).astype(BF16)
        w_kv_l = (w_kv_up[l] * kv_a_g[l][:, None]).astype(BF16)
        qng = _gather_cols(q_norm_g[l][None, :], head_cols)
        kng = _gather_cols(k_norm_g[l][None, :], head_cols)
        qg = jnp.where(jnp.asarray(nope), qng * kng * (root * root * q_scale), qng * (root * q_scale))
        kg = kng[:, LANES:] * root
        xab, q, k, v = _project(x2, w_in_l, w_ab[l], w_q_l, w_kv_l, qg, kg, cos_t, sin_t)
        yf = _fourier(xab, cmat, smat_neg, fourier_out_g[l][None, :], seq)
        o = _attention(q, k, v, batch, seq)
        x2 = _out_mlp(x2, yf, o, attn_out_g[l][None, :], w_out[l].astype(BF16),
                      mlp_norm_g[l][None, :], w_mlp_in[l].astype(BF16), w_mlp_out[l].astype(BF16))
    return x2.reshape(batch, seq, D_MODEL)
```

```python
import functools
import math

import numpy as np
import jax
import jax.numpy as jnp
from jax import lax
from jax.experimental import pallas as pl
from jax.experimental.pallas import tpu as pltpu

D_MODEL = 1024
F_GROUPS = 4
F_GROUP_DIM = 64
F_WIDTH = F_GROUPS * F_GROUP_DIM
N_HEADS = 6
Q_LORA = 256
KV_LORA = 256
QK_NOPE = 128
QK_ROPE = 64
V_DIM = 128
QK_DIM = QK_NOPE + QK_ROPE
A_WIDTH = N_HEADS * V_DIM
ROPE_BASE = 10000.0
D_FF = 4 * D_MODEL
EPS = 1e-6

LANES = 128
HEAD_PAD = 2 * LANES
HALF = QK_ROPE // 2
Z_WIDTH = F_WIDTH + Q_LORA + KV_LORA + LANES
VMEM_LIMIT = 56 * 1024 * 1024

ROW_TILE = 512
PROJ_TILE = 512
PROJ_ZROWS = 512
PROJ_SUB = 512
DFT_SPLIT = 64
Q_TILE = 256
ATTN_HEADS = 3
FF_CHUNK = 1024
SHIFT_SAFE = 60.0
BOUND_MARGIN = 1.02

BF16 = jnp.bfloat16
F32 = jnp.float32


def _rope_tile_cols(start):
    cols = np.full((LANES,), -1, np.int64)
    cols[0:HALF] = start + np.arange(HALF)
    cols[2 * HALF:3 * HALF] = start + HALF + np.arange(HALF)
    return cols


def _gather_cols(w, cols):
    taken = jnp.take(w, jnp.asarray(np.maximum(cols, 0)), axis=-1)
    return jnp.where(jnp.asarray(cols >= 0), taken, jnp.zeros((), w.dtype))


def _rms(x, g):
    return x * lax.rsqrt(jnp.mean(x * x, axis=-1, keepdims=True) + EPS) * g


def _rotate(t, cos_t, sin_t):
    return t * cos_t + pltpu.roll(t, 2 * HALF, axis=1) * sin_t


def _fold_kernel(cc_ref, sc_ref, w_ref, out_ref):
    out_ref[...] = jnp.zeros_like(out_ref)
    for g in range(F_GROUPS):
        w = w_ref[0, g]
        a = jnp.dot(cc_ref[...], w, preferred_element_type=F32, precision=lax.Precision.HIGHEST)
        b = jnp.dot(sc_ref[...], w, preferred_element_type=F32, precision=lax.Precision.HIGHEST)
        lo, hi = g * F_GROUP_DIM, (g + 1) * F_GROUP_DIM
        out_ref[0, lo:hi, lo:hi] = a
        out_ref[0, lo:hi, F_WIDTH + lo:F_WIDTH + hi] = b


def _fold_fourier_weights(w_fourier, seq):
    depth = w_fourier.shape[0]
    c = np.arange(F_GROUP_DIM)
    ang = 2.0 * np.pi * ((c[:, None] * c[None, :]) % F_GROUP_DIM) / F_GROUP_DIM
    ortho = 1.0 / math.sqrt(seq * F_GROUP_DIM)
    cc = jnp.asarray(np.cos(ang) * ortho, F32)
    sc = jnp.asarray(np.sin(ang) * ortho, F32)
    return pl.pallas_call(
        _fold_kernel,
        grid=(depth,),
        in_specs=[
            pl.BlockSpec((F_GROUP_DIM, F_GROUP_DIM), lambda l: (0, 0)),
            pl.BlockSpec((F_GROUP_DIM, F_GROUP_DIM), lambda l: (0, 0)),
            pl.BlockSpec((1, F_GROUPS, F_GROUP_DIM, F_GROUP_DIM), lambda l: (l, 0, 0, 0)),
        ],
        out_specs=pl.BlockSpec((1, F_WIDTH, 2 * F_WIDTH), lambda l: (l, 0, 0)),
        out_shape=jax.ShapeDtypeStruct((depth, F_WIDTH, 2 * F_WIDTH), F32),
        name="fourier_fold",
    )(cc, sc, w_fourier)


def _trig_kernel(ang_ref, cos_ref, sin_ref):
    a = ang_ref[...]
    lane = lax.broadcasted_iota(jnp.int32, a.shape, 1)
    s = jnp.sin(a)
    cos_ref[...] = jnp.where(lane % (2 * HALF) < HALF, jnp.cos(a), 0.0)
    sin_ref[...] = jnp.where(lane < HALF, -s, jnp.where((lane >= 2 * HALF) & (lane < 3 * HALF), s, 0.0))


def _rotary_tables(positions):
    n = positions.size
    inv_freq = ROPE_BASE ** (-jnp.arange(HALF, dtype=F32) / HALF)
    ang = positions.astype(F32).reshape(n, 1) * jnp.tile(inv_freq, LANES // HALF)[None, :]
    tile = min(2048, n)
    spec = pl.BlockSpec((tile, LANES), lambda i: (i, 0))
    return pl.pallas_call(
        _trig_kernel,
        grid=(n // tile,),
        in_specs=[spec],
        out_specs=[spec, spec],
        out_shape=[jax.ShapeDtypeStruct((n, LANES), F32)] * 2,
        name="rotary_trig",
    )(ang)


def _unit_rms(x):
    return x * lax.rsqrt(jnp.mean(x * x, axis=-1, keepdims=True) + EPS)


def _proj_matmul_stage(x_ref, win_ref, wab_ref, wq_ref, wkv_ref, xab_ref, qf_ref, kvf_ref, kpe_ref):
    for i in range(x_ref.shape[0] // PROJ_ZROWS):
        rows = slice(i * PROJ_ZROWS, (i + 1) * PROJ_ZROWS)
        h = _unit_rms(x_ref[rows, :])
        z = jnp.dot(h.astype(BF16), win_ref[...], preferred_element_type=F32)
        xab_ref[rows, :] = jnp.dot(z[:, :F_WIDTH].astype(BF16), wab_ref[...],
                                   preferred_element_type=F32).astype(BF16)
        cqn = _unit_rms(z[:, F_WIDTH:F_WIDTH + Q_LORA]).astype(BF16)
        ckvn = _unit_rms(z[:, F_WIDTH + Q_LORA:F_WIDTH + Q_LORA + KV_LORA]).astype(BF16)
        kpe_ref[rows, :] = z[:, F_WIDTH + Q_LORA + KV_LORA:]
        qf_ref[rows, :] = jnp.dot(cqn, wq_ref[...], preferred_element_type=F32)
        kvf_ref[rows, :] = jnp.dot(ckvn, wkv_ref[...], preferred_element_type=F32)


def _proj_norm_stage(qf_ref, kvf_ref, kpe_ref, qg_ref, kg_ref, cos_ref, sin_ref, q_ref, k_ref, v_ref):
    qg = qg_ref[0:1, :]
    qg_nope, qg_rope = qg[:, :LANES], qg[:, LANES:]
    q_const = qg_ref[1:2, LANES:]
    kg_rope = kg_ref[0:1, :]
    k_const = kg_ref[1:2, :]
    eps_sum = QK_DIM * EPS
    for j in range(qf_ref.shape[0] // PROJ_SUB):
        rows = slice(j * PROJ_SUB, (j + 1) * PROJ_SUB)
        k_pe = kpe_ref[rows, :]
        cos_t = cos_ref[rows, :]
        sin_t = sin_ref[rows, :]
        k_rot = _rotate(k_pe * kg_rope, cos_t, sin_t)
        ss_pe = jnp.sum(k_pe * k_pe, axis=-1, keepdims=True) + eps_sum
        for hd in range(N_HEADS):
            qh = qf_ref[rows, hd * HEAD_PAD:(hd + 1) * HEAD_PAD]
            rq = lax.rsqrt(jnp.sum(qh * qh, axis=-1, keepdims=True) + eps_sum)
            q_ref[hd, rows, :LANES] = (qh[:, :LANES] * rq * qg_nope).astype(BF16)
            q_rot = _rotate(qh[:, LANES:] * rq * qg_rope, cos_t, sin_t)
            q_ref[hd, rows, LANES:] = (q_rot + q_const).astype(BF16)
            kn = kvf_ref[rows, hd * HEAD_PAD:hd * HEAD_PAD + LANES]
            rk = lax.rsqrt(jnp.sum(kn * kn, axis=-1, keepdims=True) + ss_pe)
            k_ref[hd, rows, :LANES] = (kn * rk).astype(BF16)
            k_ref[hd, rows, LANES:] = (k_rot * rk + k_const).astype(BF16)
            v_ref[hd, rows, :] = kvf_ref[rows, hd * HEAD_PAD + LANES:(hd + 1) * HEAD_PAD].astype(BF16)


def _proj_kernel(x_ref, win_ref, wab_ref, wq_ref, wkv_ref, qg_ref, kg_ref, cos_ref, sin_ref,
                 xab_ref, q_ref, k_ref, v_ref, qf0, kvf0, kpe0, qf1, kvf1, kpe1):
    step = pl.program_id(0)

    @pl.when(step == 0)
    def _():
        qf1[...] = jnp.zeros_like(qf1)
        kvf1[...] = jnp.zeros_like(kvf1)
        kpe1[...] = jnp.zeros_like(kpe1)

    def body(fill, drain):
        _proj_matmul_stage(x_ref, win_ref, wab_ref, wq_ref, wkv_ref, xab_ref, *fill)
        _proj_norm_stage(*drain, qg_ref, kg_ref, cos_ref, sin_ref, q_ref, k_ref, v_ref)

    @pl.when(step % 2 == 0)
    def _():
        body((qf0, kvf0, kpe0), (qf1, kvf1, kpe1))

    @pl.when(step % 2 == 1)
    def _():
        body((qf1, kvf1, kpe1), (qf0, kvf0, kpe0))


def _const_spec(shape):
    return pl.BlockSpec(shape, lambda *_: (0,) * len(shape))


def _project(x2, w_in, w_ab, w_q, w_kv, qg, kg, cos_t, sin_t):
    n = x2.shape[0]
    tm = PROJ_TILE
    nt = n // tm
    fill = lambda w: pl.BlockSpec((tm, w), lambda i: (jnp.minimum(i, nt - 1), 0))
    drain = lambda w: pl.BlockSpec((tm, w), lambda i: (jnp.maximum(i - 1, 0), 0))
    head = lambda w: pl.BlockSpec((N_HEADS, tm, w), lambda i: (0, jnp.maximum(i - 1, 0), 0))
    up_width = N_HEADS * HEAD_PAD
    handoff = [pltpu.VMEM((tm, up_width), F32), pltpu.VMEM((tm, up_width), F32), pltpu.VMEM((tm, LANES), F32)]
    return pl.pallas_call(
        _proj_kernel,
        grid=(nt + 1,),
        in_specs=[
            fill(D_MODEL), _const_spec((D_MODEL, Z_WIDTH)), _const_spec((F_WIDTH, 2 * F_WIDTH)),
            _const_spec((Q_LORA, up_width)), _const_spec((KV_LORA, up_width)),
            _const_spec((2, HEAD_PAD)), _const_spec((2, LANES)), drain(LANES), drain(LANES),
        ],
        out_specs=[fill(2 * F_WIDTH), head(HEAD_PAD), head(HEAD_PAD), head(V_DIM)],
        out_shape=[
            jax.ShapeDtypeStruct((n, 2 * F_WIDTH), BF16),
            jax.ShapeDtypeStruct((N_HEADS, n, HEAD_PAD), BF16),
            jax.ShapeDtypeStruct((N_HEADS, n, HEAD_PAD), BF16),
            jax.ShapeDtypeStruct((N_HEADS, n, V_DIM), BF16),
        ],
        scratch_shapes=handoff + handoff,
        compiler_params=pltpu.CompilerParams(dimension_semantics=("arbitrary",),
                                             vmem_limit_bytes=VMEM_LIMIT),
        name="in_proj",
    )(x2, w_in, w_ab, w_q, w_kv, qg, kg, cos_t, sin_t)


def _fourier_kernel(c_ref, s_ref, xab_ref, g_ref, y_ref):
    xa = xab_ref[:, :F_WIDTH]
    xb = xab_ref[:, F_WIDTH:]
    y = (jnp.dot(c_ref[...], xa, preferred_element_type=F32)
         + jnp.dot(s_ref[...], xb, preferred_element_type=F32))
    y_ref[...] = _rms(y, g_ref[...]).astype(BF16)


def _fourier(xab, cmat, smat_neg, g, seq):
    n = xab.shape[0]
    return pl.pallas_call(
        _fourier_kernel,
        grid=(n // seq,),
        in_specs=[
            _const_spec((seq, seq)), _const_spec((seq, seq)),
            pl.BlockSpec((seq, 2 * F_WIDTH), lambda b: (b, 0)), _const_spec((1, F_WIDTH)),
        ],
        out_specs=pl.BlockSpec((seq, F_WIDTH), lambda b: (b, 0)),
        out_shape=jax.ShapeDtypeStruct((n, F_WIDTH), BF16),
        compiler_params=pltpu.CompilerParams(dimension_semantics=("arbitrary",),
                                             vmem_limit_bytes=VMEM_LIMIT),
        name="seq_dft",
    )(cmat, smat_neg, xab, g)


def _attn_kernel(q_ref, k_ref, v_ref, o_ref, vx_ref, *, heads, row_max):
    seq = q_ref.shape[1]
    for hd in range(heads):
        vx_ref[hd, :, :V_DIM] = v_ref[hd]
        vx_ref[hd, :, V_DIM:] = jnp.ones((seq, V_DIM), BF16)
    for hd in range(heads):
        for j in range(seq // Q_TILE):
            rows = slice(j * Q_TILE, (j + 1) * Q_TILE)
            s = lax.dot_general(q_ref[hd, rows, :], k_ref[hd], (((1,), (1,)), ((), ())),
                                preferred_element_type=F32)
            if row_max:
                s = s - jnp.max(s, axis=-1, keepdims=True)
            p = jnp.exp2(s).astype(BF16)
            ox = jnp.dot(p, vx_ref[hd], preferred_element_type=F32)
            o_ref[rows, hd * V_DIM:(hd + 1) * V_DIM] = (ox[:, :V_DIM] / ox[:, V_DIM:]).astype(BF16)


def _attention(q, k, v, batch, seq, row_max):
    n = q.shape[1]
    g = 1 if row_max else ATTN_HEADS
    return pl.pallas_call(
        functools.partial(_attn_kernel, heads=g, row_max=row_max),
        grid=(batch, N_HEADS // g),
        in_specs=[
            pl.BlockSpec((g, seq, HEAD_PAD), lambda b, h: (h, b, 0)),
            pl.BlockSpec((g, seq, HEAD_PAD), lambda b, h: (h, b, 0)),
            pl.BlockSpec((g, seq, V_DIM), lambda b, h: (h, b, 0)),
        ],
        out_specs=pl.BlockSpec((seq, g * V_DIM), lambda b, h: (b, h)),
        out_shape=jax.ShapeDtypeStruct((n, A_WIDTH), BF16),
        scratch_shapes=[pltpu.VMEM((g, seq, 2 * V_DIM), BF16)],
        compiler_params=pltpu.CompilerParams(
            dimension_semantics=("arbitrary", "arbitrary"),
            vmem_limit_bytes=VMEM_LIMIT),
        name="attention_rowmax" if row_max else "attention",
    )(q, k, v)


def _out_mlp_kernel(x_ref, yf_ref, o_ref, ag_ref, wo_ref, mg_ref, w1_ref, w2_ref, out_ref, act_ref):
    ya = _rms(o_ref[...].astype(F32), ag_ref[...]).astype(BF16)
    x1 = (x_ref[...]
          + jnp.dot(yf_ref[...], wo_ref[:F_WIDTH, :], preferred_element_type=F32)
          + jnp.dot(ya, wo_ref[F_WIDTH:, :], preferred_element_type=F32))
    hn = _rms(x1, mg_ref[...]).astype(BF16)
    for c in range(D_FF // FF_CHUNK):
        cols = slice(c * FF_CHUNK, (c + 1) * FF_CHUNK)
        hm = jnp.dot(hn, w1_ref[:, cols], preferred_element_type=F32)
        act_ref[:, cols] = jnp.square(jnp.maximum(hm, 0.0)).astype(BF16)
    out_ref[...] = x1 + jnp.dot(act_ref[...], w2_ref[...], preferred_element_type=F32)


def _out_mlp(x2, yf, o, ag, w_out, mg, w1, w2):
    n = x2.shape[0]
    tm = ROW_TILE
    row = lambda w: pl.BlockSpec((tm, w), lambda i: (i, 0))
    return pl.pallas_call(
        _out_mlp_kernel,
        grid=(n // tm,),
        in_specs=[
            row(D_MODEL), row(F_WIDTH), row(A_WIDTH), _const_spec((1, A_WIDTH)),
            _const_spec((D_MODEL, D_MODEL)), _const_spec((1, D_MODEL)),
            _const_spec((D_MODEL, D_FF)), _const_spec((D_FF, D_MODEL)),
        ],
        out_specs=row(D_MODEL),
        out_shape=jax.ShapeDtypeStruct((n, D_MODEL), F32),
        scratch_shapes=[pltpu.VMEM((tm, D_FF), BF16)],
        compiler_params=pltpu.CompilerParams(dimension_semantics=("arbitrary",),
                                             vmem_limit_bytes=VMEM_LIMIT),
        name="out_mlp",
    )(x2, yf, o, ag, w_out, mg, w1, w2)


def _dft_matrices(seq):
    lo_n = DFT_SPLIT
    hi_n = seq // lo_n
    k = np.arange(seq)[None, :]
    ang_hi = 2.0 * np.pi * ((k * lo_n * np.arange(hi_n)[:, None]) % seq) / seq
    ang_lo = 2.0 * np.pi * ((k * np.arange(lo_n)[:, None]) % seq) / seq
    ch, sh = jnp.asarray(np.cos(ang_hi), F32)[:, None, :], jnp.asarray(np.sin(ang_hi), F32)[:, None, :]
    cl, sl = jnp.asarray(np.cos(ang_lo), F32)[None, :, :], jnp.asarray(np.sin(ang_lo), F32)[None, :, :]
    cmat = (ch * cl - sh * sl).astype(BF16).reshape(seq, seq)
    smat_neg = (-(sh * cl + ch * sl)).astype(BF16).reshape(seq, seq)
    return cmat, smat_neg


def kernel(x, positions, attn_norm_g, w_in, w_fourier, q_a_g, w_q_up, kv_a_g, w_kv_up, q_norm_g,
           k_norm_g, fourier_out_g, attn_out_g, w_out, mlp_norm_g, w_mlp_in, w_mlp_out):
    batch, seq, _ = x.shape
    depth = w_in.shape[0]
    n = batch * seq

    cos_t, sin_t = _rotary_tables(positions)
    cmat, smat_neg = _dft_matrices(seq)
    w_ab = _fold_fourier_weights(w_fourier, seq).astype(BF16)

    rope_cols = _rope_tile_cols(0)
    z_cols = np.concatenate([np.arange(F_WIDTH + Q_LORA + KV_LORA),
                             np.where(rope_cols >= 0, rope_cols + F_WIDTH + Q_LORA + KV_LORA, -1)])
    head_cols = np.concatenate([np.arange(QK_NOPE), _rope_tile_cols(QK_NOPE)])
    q_cols = np.concatenate([np.where(head_cols >= 0, head_cols + hd * QK_DIM, -1)
                             for hd in range(N_HEADS)])

    root = math.sqrt(QK_DIM)
    q_scale = math.log2(math.e) / root
    nope = np.arange(HEAD_PAD) < QK_NOPE

    x2 = x.reshape(n, D_MODEL)
    for l in range(depth):
        w_in_l = _gather_cols(w_in[l] * attn_norm_g[l][:, None], z_cols).astype(BF16)
        w_q_l = _gather_cols(w_q_up[l] * q_a_g[l][:, None], q_cols).astype(BF16)
        w_kv_l = (w_kv_up[l] * kv_a_g[l][:, None]).astype(BF16)
        qng = _gather_cols(q_norm_g[l][None, :], head_cols)
        kng = _gather_cols(k_norm_g[l][None, :], head_cols)
        qg = jnp.where(jnp.asarray(nope), qng * kng * (root * root * q_scale), qng * (root * q_scale))
        kg = kng[:, LANES:] * root
        bound = BOUND_MARGIN * jnp.maximum(jnp.max(jnp.abs(qg[:, :LANES])),
                                           jnp.max(jnp.abs(qg[:, LANES:])) * jnp.max(jnp.abs(kg)))
        const_lane = jnp.asarray(np.arange(LANES) == HALF)
        q_const = jnp.concatenate([jnp.zeros((1, LANES), F32), jnp.where(const_lane, -bound, 0.0)[None, :]], axis=1)
        k_const = jnp.where(const_lane, 1.0, 0.0)[None, :].astype(F32)
        xab, q, k, v = _project(x2, w_in_l, w_ab[l], w_q_l, w_kv_l, jnp.concatenate([qg, q_const]),
                                jnp.concatenate([kg, k_const]), cos_t, sin_t)
        yf = _fourier(xab, cmat, smat_neg, fourier_out_g[l][None, :], seq)
        o = lax.cond(bound <= SHIFT_SAFE,
                     lambda qkv: _attention(*qkv, batch, seq, row_max=False),
                     lambda qkv: _attention(*qkv, batch, seq, row_max=True), (q, k, v))
        x2 = _out_mlp(x2, yf, o, attn_out_g[l][None, :], w_out[l].astype(BF16),
                      mlp_norm_g[l][None, :], w_mlp_in[l].astype(BF16), w_mlp_out[l].astype(BF16))
    return x2.reshape(batch, seq, D_MODEL)
```

```python
import functools
import math

import numpy as np
import jax
import jax.numpy as jnp
from jax import lax
from jax.experimental import pallas as pl
from jax.experimental.pallas import tpu as pltpu

D_MODEL = 1024
F_GROUPS = 4
F_GROUP_DIM = 64
F_WIDTH = F_GROUPS * F_GROUP_DIM
N_HEADS = 6
Q_LORA = 256
KV_LORA = 256
QK_NOPE = 128
QK_ROPE = 64
V_DIM = 128
QK_DIM = QK_NOPE + QK_ROPE
A_WIDTH = N_HEADS * V_DIM
ROPE_BASE = 10000.0
D_FF = 4 * D_MODEL
EPS = 1e-6

LANES = 128
HEAD_PAD = 2 * LANES
HALF = QK_ROPE // 2
Z_WIDTH = F_WIDTH + Q_LORA + KV_LORA + LANES
VMEM_LIMIT = 56 * 1024 * 1024

ROW_TILE = 512
PROJ_TILE = 512
PROJ_ZROWS = 512
PROJ_SUB = 512
DFT_SPLIT = 64
Q_TILE = 256
ATTN_HEADS = 3
FF_CHUNK = 1024
SHIFT_SAFE = 60.0
BOUND_MARGIN = 1.02

BF16 = jnp.bfloat16
F32 = jnp.float32


def _rope_tile_cols(start):
    cols = np.full((LANES,), -1, np.int64)
    cols[0:HALF] = start + np.arange(HALF)
    cols[2 * HALF:3 * HALF] = start + HALF + np.arange(HALF)
    return cols


def _gather_cols(w, cols):
    taken = jnp.take(w, jnp.asarray(np.maximum(cols, 0)), axis=-1)
    return jnp.where(jnp.asarray(cols >= 0), taken, jnp.zeros((), w.dtype))


def _rms(x, g):
    return x * lax.rsqrt(jnp.mean(x * x, axis=-1, keepdims=True) + EPS) * g


def _rotate(t, cos_t, sin_t):
    return t * cos_t + pltpu.roll(t, 2 * HALF, axis=1) * sin_t


def _fold_kernel(cc_ref, sc_ref, w_ref, out_ref):
    out_ref[...] = jnp.zeros_like(out_ref)
    for g in range(F_GROUPS):
        w = w_ref[0, g]
        a = jnp.dot(cc_ref[...], w, preferred_element_type=F32, precision=lax.Precision.HIGHEST)
        b = jnp.dot(sc_ref[...], w, preferred_element_type=F32, precision=lax.Precision.HIGHEST)
        lo, hi = g * F_GROUP_DIM, (g + 1) * F_GROUP_DIM
        out_ref[0, lo:hi, lo:hi] = a
        out_ref[0, lo:hi, F_WIDTH + lo:F_WIDTH + hi] = b


def _fold_fourier_weights(w_fourier, seq):
    depth = w_fourier.shape[0]
    c = np.arange(F_GROUP_DIM)
    ang = 2.0 * np.pi * ((c[:, None] * c[None, :]) % F_GROUP_DIM) / F_GROUP_DIM
    ortho = 1.0 / math.sqrt(seq * F_GROUP_DIM)
    cc = jnp.asarray(np.cos(ang) * ortho, F32)
    sc = jnp.asarray(np.sin(ang) * ortho, F32)
    return pl.pallas_call(
        _fold_kernel,
        grid=(depth,),
        in_specs=[
            pl.BlockSpec((F_GROUP_DIM, F_GROUP_DIM), lambda l: (0, 0)),
            pl.BlockSpec((F_GROUP_DIM, F_GROUP_DIM), lambda l: (0, 0)),
            pl.BlockSpec((1, F_GROUPS, F_GROUP_DIM, F_GROUP_DIM), lambda l: (l, 0, 0, 0)),
        ],
        out_specs=pl.BlockSpec((1, F_WIDTH, 2 * F_WIDTH), lambda l: (l, 0, 0)),
        out_shape=jax.ShapeDtypeStruct((depth, F_WIDTH, 2 * F_WIDTH), F32),
        name="fourier_fold",
    )(cc, sc, w_fourier)


def _trig_kernel(ang_ref, cos_ref, sin_ref):
    a = ang_ref[...]
    lane = lax.broadcasted_iota(jnp.int32, a.shape, 1)
    s = jnp.sin(a)
    cos_ref[...] = jnp.where(lane % (2 * HALF) < HALF, jnp.cos(a), 0.0)
    sin_ref[...] = jnp.where(lane < HALF, -s, jnp.where((lane >= 2 * HALF) & (lane < 3 * HALF), s, 0.0))


def _rotary_tables(positions):
    n = positions.size
    inv_freq = ROPE_BASE ** (-jnp.arange(HALF, dtype=F32) / HALF)
    ang = positions.astype(F32).reshape(n, 1) * jnp.tile(inv_freq, LANES // HALF)[None, :]
    tile = min(2048, n)
    spec = pl.BlockSpec((tile, LANES), lambda i: (i, 0))
    return pl.pallas_call(
        _trig_kernel,
        grid=(n // tile,),
        in_specs=[spec],
        out_specs=[spec, spec],
        out_shape=[jax.ShapeDtypeStruct((n, LANES), F32)] * 2,
        name="rotary_trig",
    )(ang)


def _unit_rms(x):
    return x * lax.rsqrt(jnp.mean(x * x, axis=-1, keepdims=True) + EPS)


def _proj_matmul_stage(x_ref, win_ref, wab_ref, wq_ref, wkv_ref, xab_ref, qf_ref, kvf_ref, kpe_ref):
    for i in range(x_ref.shape[0] // PROJ_ZROWS):
        rows = slice(i * PROJ_ZROWS, (i + 1) * PROJ_ZROWS)
        h = _unit_rms(x_ref[rows, :])
        z = jnp.dot(h.astype(BF16), win_ref[...], preferred_element_type=F32)
        xab_ref[rows, :] = jnp.dot(z[:, :F_WIDTH].astype(BF16), wab_ref[...],
                                   preferred_element_type=F32).astype(BF16)
        cqn = _unit_rms(z[:, F_WIDTH:F_WIDTH + Q_LORA]).astype(BF16)
        ckvn = _unit_rms(z[:, F_WIDTH + Q_LORA:F_WIDTH + Q_LORA + KV_LORA]).astype(BF16)
        kpe_ref[rows, :] = z[:, F_WIDTH + Q_LORA + KV_LORA:]
        qf_ref[rows, :] = jnp.dot(cqn, wq_ref[...], preferred_element_type=F32)
        kvf_ref[rows, :] = jnp.dot(ckvn, wkv_ref[...], preferred_element_type=F32)


def _proj_norm_stage(qf_ref, kvf_ref, kpe_ref, qg_ref, kg_ref, cos_ref, sin_ref, q_ref, k_ref, v_ref):
    qg = qg_ref[0:1, :]
    qg_nope, qg_rope = qg[:, :LANES], qg[:, LANES:]
    q_const = qg_ref[1:2, LANES:]
    kg_rope = kg_ref[0:1, :]
    k_const = kg_ref[1:2, :]
    eps_sum = QK_DIM * EPS
    for j in range(qf_ref.shape[0] // PROJ_SUB):
        rows = slice(j * PROJ_SUB, (j + 1) * PROJ_SUB)
        k_pe = kpe_ref[rows, :]
        cos_t = cos_ref[rows, :]
        sin_t = sin_ref[rows, :]
        k_rot = _rotate(k_pe * kg_rope, cos_t, sin_t)
        ss_pe = jnp.sum(k_pe * k_pe, axis=-1, keepdims=True) + eps_sum
        for hd in range(N_HEADS):
            qh = qf_ref[rows, hd * HEAD_PAD:(hd + 1) * HEAD_PAD]
            rq = lax.rsqrt(jnp.sum(qh * qh, axis=-1, keepdims=True) + eps_sum)
            q_ref[hd, rows, :LANES] = (qh[:, :LANES] * rq * qg_nope).astype(BF16)
            q_rot = _rotate(qh[:, LANES:] * rq * qg_rope, cos_t, sin_t)
            q_ref[hd, rows, LANES:] = (q_rot + q_const).astype(BF16)
            kn = kvf_ref[rows, hd * HEAD_PAD:hd * HEAD_PAD + LANES]
            rk = lax.rsqrt(jnp.sum(kn * kn, axis=-1, keepdims=True) + ss_pe)
            k_ref[hd, rows, :LANES] = (kn * rk).astype(BF16)
            k_ref[hd, rows, LANES:] = (k_rot * rk + k_const).astype(BF16)
            v_ref[hd, rows, :] = kvf_ref[rows, hd * HEAD_PAD + LANES:(hd + 1) * HEAD_PAD].astype(BF16)


def _proj_kernel(x_ref, win_ref, wab_ref, wq_ref, wkv_ref, qg_ref, kg_ref, cos_ref, sin_ref,
                 xab_ref, q_ref, k_ref, v_ref, qf0, kvf0, kpe0, qf1, kvf1, kpe1):
    step = pl.program_id(0)

    @pl.when(step == 0)
    def _():
        qf1[...] = jnp.zeros_like(qf1)
        kvf1[...] = jnp.zeros_like(kvf1)
        kpe1[...] = jnp.zeros_like(kpe1)

    def body(fill, drain):
        _proj_matmul_stage(x_ref, win_ref, wab_ref, wq_ref, wkv_ref, xab_ref, *fill)
        _proj_norm_stage(*drain, qg_ref, kg_ref, cos_ref, sin_ref, q_ref, k_ref, v_ref)

    @pl.when(step % 2 == 0)
    def _():
        body((qf0, kvf0, kpe0), (qf1, kvf1, kpe1))

    @pl.when(step % 2 == 1)
    def _():
        body((qf1, kvf1, kpe1), (qf0, kvf0, kpe0))


def _const_spec(shape):
    return pl.BlockSpec(shape, lambda *_: (0,) * len(shape))


def _project(x2, w_in, w_ab, w_q, w_kv, qg, kg, cos_t, sin_t):
    n = x2.shape[0]
    tm = PROJ_TILE
    nt = n // tm
    fill = lambda w: pl.BlockSpec((tm, w), lambda i: (jnp.minimum(i, nt - 1), 0))
    drain = lambda w: pl.BlockSpec((tm, w), lambda i: (jnp.maximum(i - 1, 0), 0))
    head = lambda w: pl.BlockSpec((N_HEADS, tm, w), lambda i: (0, jnp.maximum(i - 1, 0), 0))
    up_width = N_HEADS * HEAD_PAD
    handoff = [pltpu.VMEM((tm, up_width), F32), pltpu.VMEM((tm, up_width), F32), pltpu.VMEM((tm, LANES), F32)]
    return pl.pallas_call(
        _proj_kernel,
        grid=(nt + 1,),
        in_specs=[
            fill(D_MODEL), _const_spec((D_MODEL, Z_WIDTH)), _const_spec((F_WIDTH, 2 * F_WIDTH)),
            _const_spec((Q_LORA, up_width)), _const_spec((KV_LORA, up_width)),
            _const_spec((2, HEAD_PAD)), _const_spec((2, LANES)), drain(LANES), drain(LANES),
        ],
        out_specs=[fill(2 * F_WIDTH), head(HEAD_PAD), head(HEAD_PAD), head(V_DIM)],
        out_shape=[
            jax.ShapeDtypeStruct((n, 2 * F_WIDTH), BF16),
            jax.ShapeDtypeStruct((N_HEADS, n, HEAD_PAD), BF16),
            jax.ShapeDtypeStruct((N_HEADS, n, HEAD_PAD), BF16),
            jax.ShapeDtypeStruct((N_HEADS, n, V_DIM), BF16),
        ],
        scratch_shapes=handoff + handoff,
        compiler_params=pltpu.CompilerParams(dimension_semantics=("arbitrary",),
                                             vmem_limit_bytes=VMEM_LIMIT),
        name="in_proj",
    )(x2, w_in, w_ab, w_q, w_kv, qg, kg, cos_t, sin_t)


def _fourier_kernel(c_ref, s_ref, lo_ref, rev_ref, g_ref, y_ref):
    lo = lo_ref[...]
    rev = rev_ref[...]
    y = (jnp.dot(c_ref[...], lo[:, :F_WIDTH] + rev[:, :F_WIDTH], preferred_element_type=F32)
         + jnp.dot(s_ref[...], lo[:, F_WIDTH:] - rev[:, F_WIDTH:], preferred_element_type=F32))
    mid = rev[0:1, :F_WIDTH].astype(F32)
    odd = (lax.broadcasted_iota(jnp.int32, y.shape, 0) & 1) == 1
    y = y - jnp.where(odd, 2.0 * mid, 0.0)
    y_ref[...] = _rms(y, g_ref[...]).astype(BF16)


def _fourier(xab, cmat, smat_neg, g, seq):
    n = xab.shape[0]
    batch = n // seq
    half = seq // 2
    rev = jnp.roll(jnp.flip(xab.reshape(batch, seq, 2 * F_WIDTH)[:, half:], axis=1), 1, axis=1)
    rev = rev.reshape(batch * half, 2 * F_WIDTH)
    return pl.pallas_call(
        _fourier_kernel,
        grid=(batch,),
        in_specs=[
            _const_spec((seq, half)), _const_spec((seq, half)),
            pl.BlockSpec((half, 2 * F_WIDTH), lambda b: (2 * b, 0)),
            pl.BlockSpec((half, 2 * F_WIDTH), lambda b: (b, 0)), _const_spec((1, F_WIDTH)),
        ],
        out_specs=pl.BlockSpec((seq, F_WIDTH), lambda b: (b, 0)),
        out_shape=jax.ShapeDtypeStruct((n, F_WIDTH), BF16),
        compiler_params=pltpu.CompilerParams(dimension_semantics=("arbitrary",),
                                             vmem_limit_bytes=VMEM_LIMIT),
        name="seq_dft",
    )(cmat, smat_neg, xab, rev, g)


def _attn_kernel(q_ref, k_ref, v_ref, o_ref, vx_ref, *, heads, row_max):
    seq = q_ref.shape[1]
    for hd in range(heads):
        vx_ref[hd, :, :V_DIM] = v_ref[hd]
        vx_ref[hd, :, V_DIM:] = jnp.ones((seq, V_DIM), BF16)
    for hd in range(heads):
        for j in range(seq // Q_TILE):
            rows = slice(j * Q_TILE, (j + 1) * Q_TILE)
            s = lax.dot_general(q_ref[hd, rows, :], k_ref[hd], (((1,), (1,)), ((), ())),
                                preferred_element_type=F32)
            if row_max:
                s = s - jnp.max(s, axis=-1, keepdims=True)
            p = jnp.exp2(s).astype(BF16)
            ox = jnp.dot(p, vx_ref[hd], preferred_element_type=F32)
            o_ref[rows, hd * V_DIM:(hd + 1) * V_DIM] = (ox[:, :V_DIM] / ox[:, V_DIM:]).astype(BF16)


def _attention(q, k, v, batch, seq, row_max):
    n = q.shape[1]
    g = 1 if row_max else ATTN_HEADS
    return pl.pallas_call(
        functools.partial(_attn_kernel, heads=g, row_max=row_max),
        grid=(batch, N_HEADS // g),
        in_specs=[
            pl.BlockSpec((g, seq, HEAD_PAD), lambda b, h: (h, b, 0)),
            pl.BlockSpec((g, seq, HEAD_PAD), lambda b, h: (h, b, 0)),
            pl.BlockSpec((g, seq, V_DIM), lambda b, h: (h, b, 0)),
        ],
        out_specs=pl.BlockSpec((seq, g * V_DIM), lambda b, h: (b, h)),
        out_shape=jax.ShapeDtypeStruct((n, A_WIDTH), BF16),
        scratch_shapes=[pltpu.VMEM((g, seq, 2 * V_DIM), BF16)],
        compiler_params=pltpu.CompilerParams(
            dimension_semantics=("arbitrary", "arbitrary"),
            vmem_limit_bytes=VMEM_LIMIT),
        name="attention_rowmax" if row_max else "attention",
    )(q, k, v)


def _out_mlp_kernel(x_ref, yf_ref, o_ref, ag_ref, wo_ref, mg_ref, w1_ref, w2_ref, out_ref, act_ref):
    ya = _rms(o_ref[...].astype(F32), ag_ref[...]).astype(BF16)
    x1 = (x_ref[...]
          + jnp.dot(yf_ref[...], wo_ref[:F_WIDTH, :], preferred_element_type=F32)
          + jnp.dot(ya, wo_ref[F_WIDTH:, :], preferred_element_type=F32))
    hn = _rms(x1, mg_ref[...]).astype(BF16)
    for c in range(D_FF // FF_CHUNK):
        cols = slice(c * FF_CHUNK, (c + 1) * FF_CHUNK)
        hm = jnp.dot(hn, w1_ref[:, cols], preferred_element_type=F32)
        act_ref[:, cols] = jnp.square(jnp.maximum(hm, 0.0)).astype(BF16)
    out_ref[...] = x1 + jnp.dot(act_ref[...], w2_ref[...], preferred_element_type=F32)


def _out_mlp(x2, yf, o, ag, w_out, mg, w1, w2):
    n = x2.shape[0]
    tm = ROW_TILE
    row = lambda w: pl.BlockSpec((tm, w), lambda i: (i, 0))
    return pl.pallas_call(
        _out_mlp_kernel,
        grid=(n // tm,),
        in_specs=[
            row(D_MODEL), row(F_WIDTH), row(A_WIDTH), _const_spec((1, A_WIDTH)),
            _const_spec((D_MODEL, D_MODEL)), _const_spec((1, D_MODEL)),
            _const_spec((D_MODEL, D_FF)), _const_spec((D_FF, D_MODEL)),
        ],
        out_specs=row(D_MODEL),
        out_shape=jax.ShapeDtypeStruct((n, D_MODEL), F32),
        scratch_shapes=[pltpu.VMEM((tm, D_FF), BF16)],
        compiler_params=pltpu.CompilerParams(dimension_semantics=("arbitrary",),
                                             vmem_limit_bytes=VMEM_LIMIT),
        name="out_mlp",
    )(x2, yf, o, ag, w_out, mg, w1, w2)


def _dft_matrices(seq):
    lo_n = DFT_SPLIT
    hi_n = seq // lo_n
    half = seq // 2
    n_idx = np.arange(half)[None, :]
    ang_hi = 2.0 * np.pi * ((n_idx * lo_n * np.arange(hi_n)[:, None]) % seq) / seq
    ang_lo = 2.0 * np.pi * ((n_idx * np.arange(lo_n)[:, None]) % seq) / seq
    ch, sh = jnp.asarray(np.cos(ang_hi), F32)[:, None, :], jnp.asarray(np.sin(ang_hi), F32)[:, None, :]
    cl, sl = jnp.asarray(np.cos(ang_lo), F32)[None, :, :], jnp.asarray(np.sin(ang_lo), F32)[None, :, :]
    cmat = (ch * cl - sh * sl).astype(BF16).reshape(seq, half)
    smat_neg = (-(sh * cl + ch * sl)).astype(BF16).reshape(seq, half)
    return cmat, smat_neg


def kernel(x, positions, attn_norm_g, w_in, w_fourier, q_a_g, w_q_up, kv_a_g, w_kv_up, q_norm_g,
           k_norm_g, fourier_out_g, attn_out_g, w_out, mlp_norm_g, w_mlp_in, w_mlp_out):
    batch, seq, _ = x.shape
    depth = w_in.shape[0]
    n = batch * seq

    cos_t, sin_t = _rotary_tables(positions)
    cmat, smat_neg = _dft_matrices(seq)
    w_ab = _fold_fourier_weights(w_fourier, seq).astype(BF16)

    rope_cols = _rope_tile_cols(0)
    z_cols = np.concatenate([np.arange(F_WIDTH + Q_LORA + KV_LORA),
                             np.where(rope_cols >= 0, rope_cols + F_WIDTH + Q_LORA + KV_LORA, -1)])
    head_cols = np.concatenate([np.arange(QK_NOPE), _rope_tile_cols(QK_NOPE)])
    q_cols = np.concatenate([np.where(head_cols >= 0, head_cols + hd * QK_DIM, -1)
                             for hd in range(N_HEADS)])

    root = math.sqrt(QK_DIM)
    q_scale = math.log2(math.e) / root
    nope = np.arange(HEAD_PAD) < QK_NOPE

    x2 = x.reshape(n, D_MODEL)
    for l in range(depth):
        w_in_l = _gather_cols(w_in[l] * attn_norm_g[l][:, None], z_cols).astype(BF16)
        w_q_l = _gather_cols(w_q_up[l] * q_a_g[l][:, None], q_cols).astype(BF16)
        w_kv_l = (w_kv_up[l] * kv_a_g[l][:, None]).astype(BF16)
        qng = _gather_cols(q_norm_g[l][None, :], head_cols)
        kng = _gather_cols(k_norm_g[l][None, :], head_cols)
        qg = jnp.where(jnp.asarray(nope), qng * kng * (root * root * q_scale), qng * (root * q_scale))
        kg = kng[:, LANES:] * root
        bound = BOUND_MARGIN * jnp.maximum(jnp.max(jnp.abs(qg[:, :LANES])),
                                           jnp.max(jnp.abs(qg[:, LANES:])) * jnp.max(jnp.abs(kg)))
        const_lane = jnp.asarray(np.arange(LANES) == HALF)
        q_const = jnp.concatenate([jnp.zeros((1, LANES), F32), jnp.where(const_lane, -bound, 0.0)[None, :]], axis=1)
        k_const = jnp.where(const_lane, 1.0, 0.0)[None, :].astype(F32)
        xab, q, k, v = _project(x2, w_in_l, w_ab[l], w_q_l, w_kv_l, jnp.concatenate([qg, q_const]),
                                jnp.concatenate([kg, k_const]), cos_t, sin_t)
        yf = _fourier(xab, cmat, smat_neg, fourier_out_g[l][None, :], seq)
        o = lax.cond(bound <= SHIFT_SAFE,
                     lambda qkv: _attention(*qkv, batch, seq, row_max=False),
                     lambda qkv: _attention(*qkv, batch, seq, row_max=True), (q, k, v))
        x2 = _out_mlp(x2, yf, o, attn_out_g[l][None, :], w_out[l].astype(BF16),
                      mlp_norm_g[l][None, :], w_mlp_in[l].astype(BF16), w_mlp_out[l].astype(BF16))
    return x2.reshape(batch, seq, D_MODEL)
```

```python
import functools
import math

import numpy as np
import jax
import jax.numpy as jnp
from jax import lax
from jax.experimental import pallas as pl
from jax.experimental.pallas import tpu as pltpu

D_MODEL = 1024
F_GROUPS = 4
F_GROUP_DIM = 64
F_WIDTH = F_GROUPS * F_GROUP_DIM
N_HEADS = 6
Q_LORA = 256
KV_LORA = 256
QK_NOPE = 128
QK_ROPE = 64
V_DIM = 128
QK_DIM = QK_NOPE + QK_ROPE
A_WIDTH = N_HEADS * V_DIM
ROPE_BASE = 10000.0
D_FF = 4 * D_MODEL
EPS = 1e-6

LANES = 128
HEAD_PAD = 2 * LANES
HALF = QK_ROPE // 2
Z_WIDTH = F_WIDTH + Q_LORA + KV_LORA + LANES
VMEM_LIMIT = 56 * 1024 * 1024

ROW_TILE = 512
PROJ_TILE = 512
PROJ_ZROWS = 512
PROJ_SUB = 512
FLIP_BLOCK = 256
DFT_SPLIT = 64
Q_TILE = 256
ATTN_HEADS = 3
FF_CHUNK = 1024
SHIFT_SAFE = 60.0
BOUND_MARGIN = 1.02

BF16 = jnp.bfloat16
F32 = jnp.float32


def _rope_tile_cols(start):
    cols = np.full((LANES,), -1, np.int64)
    cols[0:HALF] = start + np.arange(HALF)
    cols[2 * HALF:3 * HALF] = start + HALF + np.arange(HALF)
    return cols


def _gather_cols(w, cols):
    taken = jnp.take(w, jnp.asarray(np.maximum(cols, 0)), axis=-1)
    return jnp.where(jnp.asarray(cols >= 0), taken, jnp.zeros((), w.dtype))


def _rms(x, g):
    return x * lax.rsqrt(jnp.mean(x * x, axis=-1, keepdims=True) + EPS) * g


def _rotate(t, cos_t, sin_t):
    return t * cos_t + pltpu.roll(t, 2 * HALF, axis=1) * sin_t


def _fold_kernel(cc_ref, sc_ref, w_ref, out_ref):
    out_ref[...] = jnp.zeros_like(out_ref)
    for g in range(F_GROUPS):
        w = w_ref[0, g]
        a = jnp.dot(cc_ref[...], w, preferred_element_type=F32, precision=lax.Precision.HIGHEST)
        b = jnp.dot(sc_ref[...], w, preferred_element_type=F32, precision=lax.Precision.HIGHEST)
        lo, hi = g * F_GROUP_DIM, (g + 1) * F_GROUP_DIM
        out_ref[0, lo:hi, lo:hi] = a
        out_ref[0, lo:hi, F_WIDTH + lo:F_WIDTH + hi] = b


def _fold_fourier_weights(w_fourier, seq):
    depth = w_fourier.shape[0]
    c = np.arange(F_GROUP_DIM)
    ang = 2.0 * np.pi * ((c[:, None] * c[None, :]) % F_GROUP_DIM) / F_GROUP_DIM
    ortho = 1.0 / math.sqrt(seq * F_GROUP_DIM)
    cc = jnp.asarray(np.cos(ang) * ortho, F32)
    sc = jnp.asarray(np.sin(ang) * ortho, F32)
    return pl.pallas_call(
        _fold_kernel,
        grid=(depth,),
        in_specs=[
            pl.BlockSpec((F_GROUP_DIM, F_GROUP_DIM), lambda l: (0, 0)),
            pl.BlockSpec((F_GROUP_DIM, F_GROUP_DIM), lambda l: (0, 0)),
            pl.BlockSpec((1, F_GROUPS, F_GROUP_DIM, F_GROUP_DIM), lambda l: (l, 0, 0, 0)),
        ],
        out_specs=pl.BlockSpec((1, F_WIDTH, 2 * F_WIDTH), lambda l: (l, 0, 0)),
        out_shape=jax.ShapeDtypeStruct((depth, F_WIDTH, 2 * F_WIDTH), F32),
        name="fourier_fold",
    )(cc, sc, w_fourier)


def _trig_kernel(ang_ref, cos_ref, sin_ref):
    a = ang_ref[...]
    lane = lax.broadcasted_iota(jnp.int32, a.shape, 1)
    s = jnp.sin(a)
    cos_ref[...] = jnp.where(lane % (2 * HALF) < HALF, jnp.cos(a), 0.0)
    sin_ref[...] = jnp.where(lane < HALF, -s, jnp.where((lane >= 2 * HALF) & (lane < 3 * HALF), s, 0.0))


def _rotary_tables(positions):
    n = positions.size
    inv_freq = ROPE_BASE ** (-jnp.arange(HALF, dtype=F32) / HALF)
    ang = positions.astype(F32).reshape(n, 1) * jnp.tile(inv_freq, LANES // HALF)[None, :]
    tile = min(2048, n)
    spec = pl.BlockSpec((tile, LANES), lambda i: (i, 0))
    return pl.pallas_call(
        _trig_kernel,
        grid=(n // tile,),
        in_specs=[spec],
        out_specs=[spec, spec],
        out_shape=[jax.ShapeDtypeStruct((n, LANES), F32)] * 2,
        name="rotary_trig",
    )(ang)


def _unit_rms(x):
    return x * lax.rsqrt(jnp.mean(x * x, axis=-1, keepdims=True) + EPS)


def _proj_matmul_stage(x_ref, win_ref, wab_ref, wq_ref, wkv_ref, xab_ref, qf_ref, kvf_ref, kpe_ref):
    for i in range(x_ref.shape[0] // PROJ_ZROWS):
        rows = slice(i * PROJ_ZROWS, (i + 1) * PROJ_ZROWS)
        h = _unit_rms(x_ref[rows, :])
        z = jnp.dot(h.astype(BF16), win_ref[...], preferred_element_type=F32)
        xab_ref[rows, :] = jnp.dot(z[:, :F_WIDTH].astype(BF16), wab_ref[...],
                                   preferred_element_type=F32).astype(BF16)
        cqn = _unit_rms(z[:, F_WIDTH:F_WIDTH + Q_LORA]).astype(BF16)
        ckvn = _unit_rms(z[:, F_WIDTH + Q_LORA:F_WIDTH + Q_LORA + KV_LORA]).astype(BF16)
        kpe_ref[rows, :] = z[:, F_WIDTH + Q_LORA + KV_LORA:]
        qf_ref[rows, :] = jnp.dot(cqn, wq_ref[...], preferred_element_type=F32)
        kvf_ref[rows, :] = jnp.dot(ckvn, wkv_ref[...], preferred_element_type=F32)


def _proj_norm_stage(qf_ref, kvf_ref, kpe_ref, qg_ref, kg_ref, cos_ref, sin_ref, q_ref, k_ref, v_ref):
    qg = qg_ref[0:1, :]
    qg_nope, qg_rope = qg[:, :LANES], qg[:, LANES:]
    q_const = qg_ref[1:2, LANES:]
    kg_rope = kg_ref[0:1, :]
    k_const = kg_ref[1:2, :]
    eps_sum = QK_DIM * EPS
    for j in range(qf_ref.shape[0] // PROJ_SUB):
        rows = slice(j * PROJ_SUB, (j + 1) * PROJ_SUB)
        k_pe = kpe_ref[rows, :]
        cos_t = cos_ref[rows, :]
        sin_t = sin_ref[rows, :]
        k_rot = _rotate(k_pe * kg_rope, cos_t, sin_t)
        ss_pe = jnp.sum(k_pe * k_pe, axis=-1, keepdims=True) + eps_sum
        for hd in range(N_HEADS):
            qh = qf_ref[rows, hd * HEAD_PAD:(hd + 1) * HEAD_PAD]
            rq = lax.rsqrt(jnp.sum(qh * qh, axis=-1, keepdims=True) + eps_sum)
            q_ref[hd, rows, :LANES] = (qh[:, :LANES] * rq * qg_nope).astype(BF16)
            q_rot = _rotate(qh[:, LANES:] * rq * qg_rope, cos_t, sin_t)
            q_ref[hd, rows, LANES:] = (q_rot + q_const).astype(BF16)
            kn = kvf_ref[rows, hd * HEAD_PAD:hd * HEAD_PAD + LANES]
            rk = lax.rsqrt(jnp.sum(kn * kn, axis=-1, keepdims=True) + ss_pe)
            k_ref[hd, rows, :LANES] = (kn * rk).astype(BF16)
            k_ref[hd, rows, LANES:] = (k_rot * rk + k_const).astype(BF16)
            v_ref[hd, rows, :] = kvf_ref[rows, hd * HEAD_PAD + LANES:(hd + 1) * HEAD_PAD].astype(BF16)


def _proj_kernel(x_ref, win_ref, wab_ref, wq_ref, wkv_ref, qg_ref, kg_ref, cos_ref, sin_ref,
                 xab_ref, q_ref, k_ref, v_ref, qf0, kvf0, kpe0, qf1, kvf1, kpe1):
    step = pl.program_id(0)

    @pl.when(step == 0)
    def _():
        qf1[...] = jnp.zeros_like(qf1)
        kvf1[...] = jnp.zeros_like(kvf1)
        kpe1[...] = jnp.zeros_like(kpe1)

    def body(fill, drain):
        _proj_matmul_stage(x_ref, win_ref, wab_ref, wq_ref, wkv_ref, xab_ref, *fill)
        _proj_norm_stage(*drain, qg_ref, kg_ref, cos_ref, sin_ref, q_ref, k_ref, v_ref)

    @pl.when(step % 2 == 0)
    def _():
        body((qf0, kvf0, kpe0), (qf1, kvf1, kpe1))

    @pl.when(step % 2 == 1)
    def _():
        body((qf1, kvf1, kpe1), (qf0, kvf0, kpe0))


def _const_spec(shape):
    return pl.BlockSpec(shape, lambda *_: (0,) * len(shape))


def _project(x2, w_in, w_ab, w_q, w_kv, qg, kg, cos_t, sin_t):
    n = x2.shape[0]
    tm = PROJ_TILE
    nt = n // tm
    fill = lambda w: pl.BlockSpec((tm, w), lambda i: (jnp.minimum(i, nt - 1), 0))
    drain = lambda w: pl.BlockSpec((tm, w), lambda i: (jnp.maximum(i - 1, 0), 0))
    head = lambda w: pl.BlockSpec((N_HEADS, tm, w), lambda i: (0, jnp.maximum(i - 1, 0), 0))
    up_width = N_HEADS * HEAD_PAD
    handoff = [pltpu.VMEM((tm, up_width), F32), pltpu.VMEM((tm, up_width), F32), pltpu.VMEM((tm, LANES), F32)]
    return pl.pallas_call(
        _proj_kernel,
        grid=(nt + 1,),
        in_specs=[
            fill(D_MODEL), _const_spec((D_MODEL, Z_WIDTH)), _const_spec((F_WIDTH, 2 * F_WIDTH)),
            _const_spec((Q_LORA, up_width)), _const_spec((KV_LORA, up_width)),
            _const_spec((2, HEAD_PAD)), _const_spec((2, LANES)), drain(LANES), drain(LANES),
        ],
        out_specs=[fill(2 * F_WIDTH), head(HEAD_PAD), head(HEAD_PAD), head(V_DIM)],
        out_shape=[
            jax.ShapeDtypeStruct((n, 2 * F_WIDTH), BF16),
            jax.ShapeDtypeStruct((N_HEADS, n, HEAD_PAD), BF16),
            jax.ShapeDtypeStruct((N_HEADS, n, HEAD_PAD), BF16),
            jax.ShapeDtypeStruct((N_HEADS, n, V_DIM), BF16),
        ],
        scratch_shapes=handoff + handoff,
        compiler_params=pltpu.CompilerParams(dimension_semantics=("arbitrary",),
                                             vmem_limit_bytes=VMEM_LIMIT),
        name="in_proj",
    )(x2, w_in, w_ab, w_q, w_kv, qg, kg, cos_t, sin_t)


def _fourier_kernel(c_ref, s_ref, flip_ref, lo_ref, hi_ref, g_ref, y_ref):
    half = lo_ref.shape[0]
    nblk = half // FLIP_BLOCK
    flipped = [jnp.dot(flip_ref[...], hi_ref[(nblk - 1 - i) * FLIP_BLOCK:(nblk - i) * FLIP_BLOCK, :],
                       preferred_element_type=F32) for i in range(nblk)]
    rev = pltpu.roll(jnp.concatenate(flipped, axis=0), 1, axis=0)
    lo = lo_ref[...].astype(F32)
    up = (lo[:, :F_WIDTH] + rev[:, :F_WIDTH]).astype(BF16)
    um = (lo[:, F_WIDTH:] - rev[:, F_WIDTH:]).astype(BF16)
    y = (jnp.dot(c_ref[...], up, preferred_element_type=F32)
         + jnp.dot(s_ref[...], um, preferred_element_type=F32))
    mid = rev[0:1, :F_WIDTH]
    odd = (lax.broadcasted_iota(jnp.int32, y.shape, 0) & 1) == 1
    y = y - jnp.where(odd, 2.0 * mid, 0.0)
    y_ref[...] = _rms(y, g_ref[...]).astype(BF16)


def _fourier(xab, cmat, smat_neg, g, seq):
    n = xab.shape[0]
    half = seq // 2
    flip = jnp.asarray(np.eye(FLIP_BLOCK)[::-1], BF16)
    return pl.pallas_call(
        _fourier_kernel,
        grid=(n // seq,),
        in_specs=[
            _const_spec((seq, half)), _const_spec((seq, half)), _const_spec((FLIP_BLOCK, FLIP_BLOCK)),
            pl.BlockSpec((half, 2 * F_WIDTH), lambda b: (2 * b, 0)),
            pl.BlockSpec((half, 2 * F_WIDTH), lambda b: (2 * b + 1, 0)), _const_spec((1, F_WIDTH)),
        ],
        out_specs=pl.BlockSpec((seq, F_WIDTH), lambda b: (b, 0)),
        out_shape=jax.ShapeDtypeStruct((n, F_WIDTH), BF16),
        compiler_params=pltpu.CompilerParams(dimension_semantics=("arbitrary",),
                                             vmem_limit_bytes=VMEM_LIMIT),
        name="seq_dft",
    )(cmat, smat_neg, flip, xab, xab, g)


def _attn_kernel(q_ref, k_ref, v_ref, o_ref, vx_ref, *, heads, row_max):
    seq = q_ref.shape[1]
    for hd in range(heads):
        vx_ref[hd, :, :V_DIM] = v_ref[hd]
        vx_ref[hd, :, V_DIM:] = jnp.ones((seq, V_DIM), BF16)
    for hd in range(heads):
        for j in range(seq // Q_TILE):
            rows = slice(j * Q_TILE, (j + 1) * Q_TILE)
            s = lax.dot_general(q_ref[hd, rows, :], k_ref[hd], (((1,), (1,)), ((), ())),
                                preferred_element_type=F32)
            if row_max:
                s = s - jnp.max(s, axis=-1, keepdims=True)
            p = jnp.exp2(s).astype(BF16)
            ox = jnp.dot(p, vx_ref[hd], preferred_element_type=F32)
            o_ref[rows, hd * V_DIM:(hd + 1) * V_DIM] = (ox[:, :V_DIM] / ox[:, V_DIM:]).astype(BF16)


def _attention(q, k, v, batch, seq, row_max):
    n = q.shape[1]
    g = 1 if row_max else ATTN_HEADS
    return pl.pallas_call(
        functools.partial(_attn_kernel, heads=g, row_max=row_max),
        grid=(batch, N_HEADS // g),
        in_specs=[
            pl.BlockSpec((g, seq, HEAD_PAD), lambda b, h: (h, b, 0)),
            pl.BlockSpec((g, seq, HEAD_PAD), lambda b, h: (h, b, 0)),
            pl.BlockSpec((g, seq, V_DIM), lambda b, h: (h, b, 0)),
        ],
        out_specs=pl.BlockSpec((seq, g * V_DIM), lambda b, h: (b, h)),
        out_shape=jax.ShapeDtypeStruct((n, A_WIDTH), BF16),
        scratch_shapes=[pltpu.VMEM((g, seq, 2 * V_DIM), BF16)],
        compiler_params=pltpu.CompilerParams(
            dimension_semantics=("arbitrary", "arbitrary"),
            vmem_limit_bytes=VMEM_LIMIT),
        name="attention_rowmax" if row_max else "attention",
    )(q, k, v)


def _out_mlp_kernel(x_ref, yf_ref, o_ref, ag_ref, wo_ref, mg_ref, w1_ref, w2_ref, out_ref, act_ref):
    ya = _rms(o_ref[...].astype(F32), ag_ref[...]).astype(BF16)
    x1 = (x_ref[...]
          + jnp.dot(yf_ref[...], wo_ref[:F_WIDTH, :], preferred_element_type=F32)
          + jnp.dot(ya, wo_ref[F_WIDTH:, :], preferred_element_type=F32))
    hn = _rms(x1, mg_ref[...]).astype(BF16)
    for c in range(D_FF // FF_CHUNK):
        cols = slice(c * FF_CHUNK, (c + 1) * FF_CHUNK)
        hm = jnp.dot(hn, w1_ref[:, cols], preferred_element_type=F32)
        act_ref[:, cols] = jnp.square(jnp.maximum(hm, 0.0)).astype(BF16)
    out_ref[...] = x1 + jnp.dot(act_ref[...], w2_ref[...], preferred_element_type=F32)


def _out_mlp(x2, yf, o, ag, w_out, mg, w1, w2):
    n = x2.shape[0]
    tm = ROW_TILE
    row = lambda w: pl.BlockSpec((tm, w), lambda i: (i, 0))
    return pl.pallas_call(
        _out_mlp_kernel,
        grid=(n // tm,),
        in_specs=[
            row(D_MODEL), row(F_WIDTH), row(A_WIDTH), _const_spec((1, A_WIDTH)),
            _const_spec((D_MODEL, D_MODEL)), _const_spec((1, D_MODEL)),
            _const_spec((D_MODEL, D_FF)), _const_spec((D_FF, D_MODEL)),
        ],
        out_specs=row(D_MODEL),
        out_shape=jax.ShapeDtypeStruct((n, D_MODEL), F32),
        scratch_shapes=[pltpu.VMEM((tm, D_FF), BF16)],
        compiler_params=pltpu.CompilerParams(dimension_semantics=("arbitrary",),
                                             vmem_limit_bytes=VMEM_LIMIT),
        name="out_mlp",
    )(x2, yf, o, ag, w_out, mg, w1, w2)


def _dft_matrices(seq):
    lo_n = DFT_SPLIT
    hi_n = seq // lo_n
    half = seq // 2
    n_idx = np.arange(half)[None, :]
    ang_hi = 2.0 * np.pi * ((n_idx * lo_n * np.arange(hi_n)[:, None]) % seq) / seq
    ang_lo = 2.0 * np.pi * ((n_idx * np.arange(lo_n)[:, None]) % seq) / seq
    ch, sh = jnp.asarray(np.cos(ang_hi), F32)[:, None, :], jnp.asarray(np.sin(ang_hi), F32)[:, None, :]
    cl, sl = jnp.asarray(np.cos(ang_lo), F32)[None, :, :], jnp.asarray(np.sin(ang_lo), F32)[None, :, :]
    cmat = (ch * cl - sh * sl).astype(BF16).reshape(seq, half)
    smat_neg = (-(sh * cl + ch * sl)).astype(BF16).reshape(seq, half)
    return cmat, smat_neg


def kernel(x, positions, attn_norm_g, w_in, w_fourier, q_a_g, w_q_up, kv_a_g, w_kv_up, q_norm_g,
           k_norm_g, fourier_out_g, attn_out_g, w_out, mlp_norm_g, w_mlp_in, w_mlp_out):
    batch, seq, _ = x.shape
    depth = w_in.shape[0]
    n = batch * seq

    cos_t, sin_t = _rotary_tables(positions)
    cmat, smat_neg = _dft_matrices(seq)
    w_ab = _fold_fourier_weights(w_fourier, seq).astype(BF16)

    rope_cols = _rope_tile_cols(0)
    z_cols = np.concatenate([np.arange(F_WIDTH + Q_LORA + KV_LORA),
                             np.where(rope_cols >= 0, rope_cols + F_WIDTH + Q_LORA + KV_LORA, -1)])
    head_cols = np.concatenate([np.arange(QK_NOPE), _rope_tile_cols(QK_NOPE)])
    q_cols = np.concatenate([np.where(head_cols >= 0, head_cols + hd * QK_DIM, -1)
                             for hd in range(N_HEADS)])

    root = math.sqrt(QK_DIM)
    q_scale = math.log2(math.e) / root
    nope = np.arange(HEAD_PAD) < QK_NOPE

    x2 = x.reshape(n, D_MODEL)
    for l in range(depth):
        w_in_l = _gather_cols(w_in[l] * attn_norm_g[l][:, None], z_cols).astype(BF16)
        w_q_l = _gather_cols(w_q_up[l] * q_a_g[l][:, None], q_cols).astype(BF16)
        w_kv_l = (w_kv_up[l] * kv_a_g[l][:, None]).astype(BF16)
        qng = _gather_cols(q_norm_g[l][None, :], head_cols)
        kng = _gather_cols(k_norm_g[l][None, :], head_cols)
        qg = jnp.where(jnp.asarray(nope), qng * kng * (root * root * q_scale), qng * (root * q_scale))
        kg = kng[:, LANES:] * root
        bound = BOUND_MARGIN * jnp.maximum(jnp.max(jnp.abs(qg[:, :LANES])),
                                           jnp.max(jnp.abs(qg[:, LANES:])) * jnp.max(jnp.abs(kg)))
        const_lane = jnp.asarray(np.arange(LANES) == HALF)
        q_const = jnp.concatenate([jnp.zeros((1, LANES), F32), jnp.where(const_lane, -bound, 0.0)[None, :]], axis=1)
        k_const = jnp.where(const_lane, 1.0, 0.0)[None, :].astype(F32)
        xab, q, k, v = _project(x2, w_in_l, w_ab[l], w_q_l, w_kv_l, jnp.concatenate([qg, q_const]),
                                jnp.concatenate([kg, k_const]), cos_t, sin_t)
        yf = _fourier(xab, cmat, smat_neg, fourier_out_g[l][None, :], seq)
        o = lax.cond(bound <= SHIFT_SAFE,
                     lambda qkv: _attention(*qkv, batch, seq, row_max=False),
                     lambda qkv: _attention(*qkv, batch, seq, row_max=True), (q, k, v))
        x2 = _out_mlp(x2, yf, o, attn_out_g[l][None, :], w_out[l].astype(BF16),
                      mlp_norm_g[l][None, :], w_mlp_in[l].astype(BF16), w_mlp_out[l].astype(BF16))
    return x2.reshape(batch, seq, D_MODEL)
```

```python
import functools
import math

import numpy as np
import jax
import jax.numpy as jnp
from jax import lax
from jax.experimental import pallas as pl
from jax.experimental.pallas import tpu as pltpu

D_MODEL = 1024
F_GROUPS = 4
F_GROUP_DIM = 64
F_WIDTH = F_GROUPS * F_GROUP_DIM
N_HEADS = 6
Q_LORA = 256
KV_LORA = 256
QK_NOPE = 128
QK_ROPE = 64
V_DIM = 128
QK_DIM = QK_NOPE + QK_ROPE
A_WIDTH = N_HEADS * V_DIM
ROPE_BASE = 10000.0
D_FF = 4 * D_MODEL
EPS = 1e-6

LANES = 128
HEAD_PAD = 2 * LANES
HALF = QK_ROPE // 2
Z_WIDTH = F_WIDTH + Q_LORA + KV_LORA + LANES
VMEM_LIMIT = 56 * 1024 * 1024

ROW_TILE = 512
PROJ_TILE = 512
PROJ_ZROWS = 512
PROJ_SUB = 512
FLIP_BLOCK = 256
DFT_SPLIT = 64
Q_TILE = 256
ATTN_HEADS = 6
FF_CHUNK = 1024
SHIFT_SAFE = 60.0
BOUND_MARGIN = 1.02

BF16 = jnp.bfloat16
F32 = jnp.float32


def _rope_tile_cols(start):
    cols = np.full((LANES,), -1, np.int64)
    cols[0:HALF] = start + np.arange(HALF)
    cols[2 * HALF:3 * HALF] = start + HALF + np.arange(HALF)
    return cols


def _gather_cols(w, cols):
    taken = jnp.take(w, jnp.asarray(np.maximum(cols, 0)), axis=-1)
    return jnp.where(jnp.asarray(cols >= 0), taken, jnp.zeros((), w.dtype))


def _rms(x, g):
    return x * lax.rsqrt(jnp.mean(x * x, axis=-1, keepdims=True) + EPS) * g


def _rotate(t, cos_t, sin_t):
    return t * cos_t + pltpu.roll(t, 2 * HALF, axis=1) * sin_t


def _fold_kernel(cc_ref, sc_ref, w_ref, out_ref):
    out_ref[...] = jnp.zeros_like(out_ref)
    for g in range(F_GROUPS):
        w = w_ref[0, g]
        a = jnp.dot(cc_ref[...], w, preferred_element_type=F32, precision=lax.Precision.HIGHEST)
        b = jnp.dot(sc_ref[...], w, preferred_element_type=F32, precision=lax.Precision.HIGHEST)
        lo, hi = g * F_GROUP_DIM, (g + 1) * F_GROUP_DIM
        out_ref[0, lo:hi, lo:hi] = a
        out_ref[0, lo:hi, F_WIDTH + lo:F_WIDTH + hi] = b


def _fold_fourier_weights(w_fourier, seq):
    depth = w_fourier.shape[0]
    c = np.arange(F_GROUP_DIM)
    ang = 2.0 * np.pi * ((c[:, None] * c[None, :]) % F_GROUP_DIM) / F_GROUP_DIM
    ortho = 1.0 / math.sqrt(seq * F_GROUP_DIM)
    cc = jnp.asarray(np.cos(ang) * ortho, F32)
    sc = jnp.asarray(np.sin(ang) * ortho, F32)
    return pl.pallas_call(
        _fold_kernel,
        grid=(depth,),
        in_specs=[
            pl.BlockSpec((F_GROUP_DIM, F_GROUP_DIM), lambda l: (0, 0)),
            pl.BlockSpec((F_GROUP_DIM, F_GROUP_DIM), lambda l: (0, 0)),
            pl.BlockSpec((1, F_GROUPS, F_GROUP_DIM, F_GROUP_DIM), lambda l: (l, 0, 0, 0)),
        ],
        out_specs=pl.BlockSpec((1, F_WIDTH, 2 * F_WIDTH), lambda l: (l, 0, 0)),
        out_shape=jax.ShapeDtypeStruct((depth, F_WIDTH, 2 * F_WIDTH), F32),
        name="fourier_fold",
    )(cc, sc, w_fourier)


def _trig_kernel(ang_ref, cos_ref, sin_ref):
    a = ang_ref[...]
    lane = lax.broadcasted_iota(jnp.int32, a.shape, 1)
    s = jnp.sin(a)
    cos_ref[...] = jnp.where(lane % (2 * HALF) < HALF, jnp.cos(a), 0.0)
    sin_ref[...] = jnp.where(lane < HALF, -s, jnp.where((lane >= 2 * HALF) & (lane < 3 * HALF), s, 0.0))


def _lane_freqs():
    inv_freq = ROPE_BASE ** (-jnp.arange(HALF, dtype=F32) / HALF)
    return jnp.tile(inv_freq, LANES // HALF)[None, :]


def _rotary_tables(positions):
    n = positions.size
    ang = positions.astype(F32).reshape(n, 1) * _lane_freqs()
    tile = min(2048, n)
    spec = pl.BlockSpec((tile, LANES), lambda i: (i, 0))
    return pl.pallas_call(
        _trig_kernel,
        grid=(n // tile,),
        in_specs=[spec],
        out_specs=[spec, spec],
        out_shape=[jax.ShapeDtypeStruct((n, LANES), F32)] * 2,
        name="rotary_trig",
    )(ang)


def _plain_trig_kernel(ang_ref, cos_ref, sin_ref):
    a = ang_ref[...]
    cos_ref[...] = jnp.cos(a)
    sin_ref[...] = jnp.sin(a)


def _offset_combine_kernel(cs_ref, sn_ref, co_ref, so_ref, cos_ref, sin_ref):
    cs, sn = cs_ref[...], sn_ref[...]
    co, so = co_ref[0], so_ref[0]
    c = cs * co - sn * so
    s = sn * co + cs * so
    lane = lax.broadcasted_iota(jnp.int32, c.shape, 1)
    cos_ref[...] = jnp.where(lane % (2 * HALF) < HALF, c, 0.0)
    sin_ref[...] = jnp.where(lane < HALF, -s, jnp.where((lane >= 2 * HALF) & (lane < 3 * HALF), s, 0.0))


def _rotary_tables_offset(positions):
    batch, seq = positions.shape
    freqs = _lane_freqs()
    step_ang = jnp.arange(seq, dtype=F32)[:, None] * freqs
    pad = -batch % 8
    base_ang = jnp.pad(positions[:, 0].astype(F32), (0, pad))[:, None] * freqs

    def trig(ang):
        return pl.pallas_call(
            _plain_trig_kernel,
            out_shape=[jax.ShapeDtypeStruct(ang.shape, F32)] * 2,
            name="rotary_trig_small",
        )(ang)

    cs, sn = trig(step_ang)
    co, so = trig(base_ang)
    co, so = co[:batch, None, :], so[:batch, None, :]
    table = pl.BlockSpec((seq, LANES), lambda b: (b, 0))
    row = pl.BlockSpec((1, 1, LANES), lambda b: (b, 0, 0))
    return pl.pallas_call(
        _offset_combine_kernel,
        grid=(batch,),
        in_specs=[_const_spec((seq, LANES)), _const_spec((seq, LANES)), row, row],
        out_specs=[table, table],
        out_shape=[jax.ShapeDtypeStruct((batch * seq, LANES), F32)] * 2,
        name="rotary_combine",
    )(cs, sn, co, so)


def _unit_rms(x):
    return x * lax.rsqrt(jnp.mean(x * x, axis=-1, keepdims=True) + EPS)


def _proj_matmul_stage(x_ref, win_ref, wab_ref, wq_ref, wkv_ref, xab_ref, qf_ref, kvf_ref, kpe_ref):
    for i in range(x_ref.shape[0] // PROJ_ZROWS):
        rows = slice(i * PROJ_ZROWS, (i + 1) * PROJ_ZROWS)
        h = _unit_rms(x_ref[rows, :])
        z = jnp.dot(h.astype(BF16), win_ref[...], preferred_element_type=F32)
        xab_ref[rows, :] = jnp.dot(z[:, :F_WIDTH].astype(BF16), wab_ref[...],
                                   preferred_element_type=F32).astype(BF16)
        cqn = _unit_rms(z[:, F_WIDTH:F_WIDTH + Q_LORA]).astype(BF16)
        ckvn = _unit_rms(z[:, F_WIDTH + Q_LORA:F_WIDTH + Q_LORA + KV_LORA]).astype(BF16)
        kpe_ref[rows, :] = z[:, F_WIDTH + Q_LORA + KV_LORA:]
        qf_ref[rows, :] = jnp.dot(cqn, wq_ref[...], preferred_element_type=F32)
        kvf_ref[rows, :] = jnp.dot(ckvn, wkv_ref[...], preferred_element_type=F32)


def _proj_norm_stage(qf_ref, kvf_ref, kpe_ref, qg_ref, kg_ref, cos_ref, sin_ref, q_ref, k_ref, v_ref):
    qg = qg_ref[0:1, :]
    qg_nope, qg_rope = qg[:, :LANES], qg[:, LANES:]
    q_const = qg_ref[1:2, LANES:]
    kg_rope = kg_ref[0:1, :]
    k_const = kg_ref[1:2, :]
    eps_sum = QK_DIM * EPS
    for j in range(qf_ref.shape[0] // PROJ_SUB):
        rows = slice(j * PROJ_SUB, (j + 1) * PROJ_SUB)
        k_pe = kpe_ref[rows, :]
        cos_t = cos_ref[rows, :]
        sin_t = sin_ref[rows, :]
        k_rot = _rotate(k_pe * kg_rope, cos_t, sin_t)
        ss_pe = jnp.sum(k_pe * k_pe, axis=-1, keepdims=True) + eps_sum
        for hd in range(N_HEADS):
            qh = qf_ref[rows, hd * HEAD_PAD:(hd + 1) * HEAD_PAD]
            rq = lax.rsqrt(jnp.sum(qh * qh, axis=-1, keepdims=True) + eps_sum)
            q_ref[hd, rows, :LANES] = (qh[:, :LANES] * rq * qg_nope).astype(BF16)
            q_rot = _rotate(qh[:, LANES:] * rq * qg_rope, cos_t, sin_t)
            q_ref[hd, rows, LANES:] = (q_rot + q_const).astype(BF16)
            kn = kvf_ref[rows, hd * HEAD_PAD:hd * HEAD_PAD + LANES]
            rk = lax.rsqrt(jnp.sum(kn * kn, axis=-1, keepdims=True) + ss_pe)
            k_ref[hd, rows, :LANES] = (kn * rk).astype(BF16)
            k_ref[hd, rows, LANES:] = (k_rot * rk + k_const).astype(BF16)
            v_ref[hd, rows, :] = kvf_ref[rows, hd * HEAD_PAD + LANES:(hd + 1) * HEAD_PAD].astype(BF16)


def _proj_kernel(x_ref, win_ref, wab_ref, wq_ref, wkv_ref, qg_ref, kg_ref, cos_ref, sin_ref,
                 xab_ref, q_ref, k_ref, v_ref, qf0, kvf0, kpe0, qf1, kvf1, kpe1):
    step = pl.program_id(0)

    @pl.when(step == 0)
    def _():
        qf1[...] = jnp.zeros_like(qf1)
        kvf1[...] = jnp.zeros_like(kvf1)
        kpe1[...] = jnp.zeros_like(kpe1)

    def body(fill, drain):
        _proj_matmul_stage(x_ref, win_ref, wab_ref, wq_ref, wkv_ref, xab_ref, *fill)
        _proj_norm_stage(*drain, qg_ref, kg_ref, cos_ref, sin_ref, q_ref, k_ref, v_ref)

    @pl.when(step % 2 == 0)
    def _():
        body((qf0, kvf0, kpe0), (qf1, kvf1, kpe1))

    @pl.when(step % 2 == 1)
    def _():
        body((qf1, kvf1, kpe1), (qf0, kvf0, kpe0))


def _const_spec(shape):
    return pl.BlockSpec(shape, lambda *_: (0,) * len(shape))


def _project(x2, w_in, w_ab, w_q, w_kv, qg, kg, cos_t, sin_t):
    n = x2.shape[0]
    tm = PROJ_TILE
    nt = n // tm
    fill = lambda w: pl.BlockSpec((tm, w), lambda i: (jnp.minimum(i, nt - 1), 0))
    drain = lambda w: pl.BlockSpec((tm, w), lambda i: (jnp.maximum(i - 1, 0), 0))
    head = lambda w: pl.BlockSpec((N_HEADS, tm, w), lambda i: (0, jnp.maximum(i - 1, 0), 0))
    up_width = N_HEADS * HEAD_PAD
    handoff = [pltpu.VMEM((tm, up_width), F32), pltpu.VMEM((tm, up_width), F32), pltpu.VMEM((tm, LANES), F32)]
    return pl.pallas_call(
        _proj_kernel,
        grid=(nt + 1,),
        in_specs=[
            fill(D_MODEL), _const_spec((D_MODEL, Z_WIDTH)), _const_spec((F_WIDTH, 2 * F_WIDTH)),
            _const_spec((Q_LORA, up_width)), _const_spec((KV_LORA, up_width)),
            _const_spec((2, HEAD_PAD)), _const_spec((2, LANES)), drain(LANES), drain(LANES),
        ],
        out_specs=[fill(2 * F_WIDTH), head(HEAD_PAD), head(HEAD_PAD), head(V_DIM)],
        out_shape=[
            jax.ShapeDtypeStruct((n, 2 * F_WIDTH), BF16),
            jax.ShapeDtypeStruct((N_HEADS, n, HEAD_PAD), BF16),
            jax.ShapeDtypeStruct((N_HEADS, n, HEAD_PAD), BF16),
            jax.ShapeDtypeStruct((N_HEADS, n, V_DIM), BF16),
        ],
        scratch_shapes=handoff + handoff,
        compiler_params=pltpu.CompilerParams(dimension_semantics=("arbitrary",),
                                             vmem_limit_bytes=VMEM_LIMIT),
        name="in_proj",
    )(x2, w_in, w_ab, w_q, w_kv, qg, kg, cos_t, sin_t)


def _fourier_kernel(c_ref, s_ref, flip_ref, lo_ref, hi_ref, g_ref, y_ref):
    half = lo_ref.shape[0]
    nblk = half // FLIP_BLOCK
    flipped = [jnp.dot(flip_ref[...], hi_ref[(nblk - 1 - i) * FLIP_BLOCK:(nblk - i) * FLIP_BLOCK, :],
                       preferred_element_type=F32) for i in range(nblk)]
    rev = pltpu.roll(jnp.concatenate(flipped, axis=0), 1, axis=0)
    lo = lo_ref[...].astype(F32)
    up = (lo[:, :F_WIDTH] + rev[:, :F_WIDTH]).astype(BF16)
    um = (lo[:, F_WIDTH:] - rev[:, F_WIDTH:]).astype(BF16)
    y = (jnp.dot(c_ref[...], up, preferred_element_type=F32)
         + jnp.dot(s_ref[...], um, preferred_element_type=F32))
    mid = rev[0:1, :F_WIDTH]
    odd = (lax.broadcasted_iota(jnp.int32, y.shape, 0) & 1) == 1
    y = y - jnp.where(odd, 2.0 * mid, 0.0)
    y_ref[...] = _rms(y, g_ref[...]).astype(BF16)


def _fourier(xab, cmat, smat_neg, g, seq):
    n = xab.shape[0]
    half = seq // 2
    flip = jnp.asarray(np.eye(FLIP_BLOCK)[::-1], BF16)
    return pl.pallas_call(
        _fourier_kernel,
        grid=(n // seq,),
        in_specs=[
            _const_spec((seq, half)), _const_spec((seq, half)), _const_spec((FLIP_BLOCK, FLIP_BLOCK)),
            pl.BlockSpec((half, 2 * F_WIDTH), lambda b: (2 * b, 0)),
            pl.BlockSpec((half, 2 * F_WIDTH), lambda b: (2 * b + 1, 0)), _const_spec((1, F_WIDTH)),
        ],
        out_specs=pl.BlockSpec((seq, F_WIDTH), lambda b: (b, 0)),
        out_shape=jax.ShapeDtypeStruct((n, F_WIDTH), BF16),
        compiler_params=pltpu.CompilerParams(dimension_semantics=("arbitrary",),
                                             vmem_limit_bytes=VMEM_LIMIT),
        name="seq_dft",
    )(cmat, smat_neg, flip, xab, xab, g)


def _attn_kernel(q_ref, k_ref, v_ref, o_ref, vx_ref, *, heads, row_max):
    seq = q_ref.shape[1]
    for hd in range(heads):
        vx_ref[hd, :, :V_DIM] = v_ref[hd]
        vx_ref[hd, :, V_DIM:] = jnp.ones((seq, V_DIM), BF16)
    for hd in range(heads):
        for j in range(seq // Q_TILE):
            rows = slice(j * Q_TILE, (j + 1) * Q_TILE)
            s = lax.dot_general(q_ref[hd, rows, :], k_ref[hd], (((1,), (1,)), ((), ())),
                                preferred_element_type=F32)
            if row_max:
                s = s - jnp.max(s, axis=-1, keepdims=True)
            p = jnp.exp2(s).astype(BF16)
            ox = jnp.dot(p, vx_ref[hd], preferred_element_type=F32)
            o_ref[rows, hd * V_DIM:(hd + 1) * V_DIM] = (ox[:, :V_DIM] / ox[:, V_DIM:]).astype(BF16)


def _attention(q, k, v, batch, seq, row_max):
    n = q.shape[1]
    g = 1 if row_max else ATTN_HEADS
    return pl.pallas_call(
        functools.partial(_attn_kernel, heads=g, row_max=row_max),
        grid=(batch, N_HEADS // g),
        in_specs=[
            pl.BlockSpec((g, seq, HEAD_PAD), lambda b, h: (h, b, 0)),
            pl.BlockSpec((g, seq, HEAD_PAD), lambda b, h: (h, b, 0)),
            pl.BlockSpec((g, seq, V_DIM), lambda b, h: (h, b, 0)),
        ],
        out_specs=pl.BlockSpec((seq, g * V_DIM), lambda b, h: (b, h)),
        out_shape=jax.ShapeDtypeStruct((n, A_WIDTH), BF16),
        scratch_shapes=[pltpu.VMEM((g, seq, 2 * V_DIM), BF16)],
        compiler_params=pltpu.CompilerParams(
            dimension_semantics=("arbitrary", "arbitrary"),
            vmem_limit_bytes=VMEM_LIMIT),
        name="attention_rowmax" if row_max else "attention",
    )(q, k, v)


def _out_mlp_kernel(x_ref, yf_ref, o_ref, ag_ref, wo_ref, mg_ref, w1_ref, w2_ref, out_ref, act_ref):
    ya = _rms(o_ref[...].astype(F32), ag_ref[...]).astype(BF16)
    x1 = (x_ref[...]
          + jnp.dot(yf_ref[...], wo_ref[:F_WIDTH, :], preferred_element_type=F32)
          + jnp.dot(ya, wo_ref[F_WIDTH:, :], preferred_element_type=F32))
    hn = _rms(x1, mg_ref[...]).astype(BF16)
    for c in range(D_FF // FF_CHUNK):
        cols = slice(c * FF_CHUNK, (c + 1) * FF_CHUNK)
        hm = jnp.dot(hn, w1_ref[:, cols], preferred_element_type=F32)
        act_ref[:, cols] = jnp.square(jnp.maximum(hm, 0.0)).astype(BF16)
    out_ref[...] = x1 + jnp.dot(act_ref[...], w2_ref[...], preferred_element_type=F32)


def _out_mlp(x2, yf, o, ag, w_out, mg, w1, w2):
    n = x2.shape[0]
    tm = ROW_TILE
    row = lambda w: pl.BlockSpec((tm, w), lambda i: (i, 0))
    return pl.pallas_call(
        _out_mlp_kernel,
        grid=(n // tm,),
        in_specs=[
            row(D_MODEL), row(F_WIDTH), row(A_WIDTH), _const_spec((1, A_WIDTH)),
            _const_spec((D_MODEL, D_MODEL)), _const_spec((1, D_MODEL)),
            _const_spec((D_MODEL, D_FF)), _const_spec((D_FF, D_MODEL)),
        ],
        out_specs=row(D_MODEL),
        out_shape=jax.ShapeDtypeStruct((n, D_MODEL), F32),
        scratch_shapes=[pltpu.VMEM((tm, D_FF), BF16)],
        compiler_params=pltpu.CompilerParams(dimension_semantics=("arbitrary",),
                                             vmem_limit_bytes=VMEM_LIMIT),
        name="out_mlp",
    )(x2, yf, o, ag, w_out, mg, w1, w2)


def _dft_matrices(seq):
    lo_n = DFT_SPLIT
    hi_n = seq // lo_n
    half = seq // 2
    n_idx = np.arange(half)[None, :]
    ang_hi = 2.0 * np.pi * ((n_idx * lo_n * np.arange(hi_n)[:, None]) % seq) / seq
    ang_lo = 2.0 * np.pi * ((n_idx * np.arange(lo_n)[:, None]) % seq) / seq
    ch, sh = jnp.asarray(np.cos(ang_hi), F32)[:, None, :], jnp.asarray(np.sin(ang_hi), F32)[:, None, :]
    cl, sl = jnp.asarray(np.cos(ang_lo), F32)[None, :, :], jnp.asarray(np.sin(ang_lo), F32)[None, :, :]
    cmat = (ch * cl - sh * sl).astype(BF16).reshape(seq, half)
    smat_neg = (-(sh * cl + ch * sl)).astype(BF16).reshape(seq, half)
    return cmat, smat_neg


def kernel(x, positions, attn_norm_g, w_in, w_fourier, q_a_g, w_q_up, kv_a_g, w_kv_up, q_norm_g,
           k_norm_g, fourier_out_g, attn_out_g, w_out, mlp_norm_g, w_mlp_in, w_mlp_out):
    batch, seq, _ = x.shape
    depth = w_in.shape[0]
    n = batch * seq

    consecutive = jnp.all(positions == positions[:, :1] + jnp.arange(seq, dtype=positions.dtype)[None, :])
    cos_t, sin_t = lax.cond(consecutive, _rotary_tables_offset, _rotary_tables, positions)
    cmat, smat_neg = _dft_matrices(seq)
    w_ab = _fold_fourier_weights(w_fourier, seq).astype(BF16)

    rope_cols = _rope_tile_cols(0)
    z_cols = np.concatenate([np.arange(F_WIDTH + Q_LORA + KV_LORA),
                             np.where(rope_cols >= 0, rope_cols + F_WIDTH + Q_LORA + KV_LORA, -1)])
    head_cols = np.concatenate([np.arange(QK_NOPE), _rope_tile_cols(QK_NOPE)])
    q_cols = np.concatenate([np.where(head_cols >= 0, head_cols + hd * QK_DIM, -1)
                             for hd in range(N_HEADS)])

    root = math.sqrt(QK_DIM)
    q_scale = math.log2(math.e) / root
    nope = np.arange(HEAD_PAD) < QK_NOPE

    x2 = x.reshape(n, D_MODEL)
    for l in range(depth):
        w_in_l = _gather_cols(w_in[l] * attn_norm_g[l][:, None], z_cols).astype(BF16)
        w_q_l = _gather_cols(w_q_up[l] * q_a_g[l][:, None], q_cols).astype(BF16)
        w_kv_l = (w_kv_up[l] * kv_a_g[l][:, None]).astype(BF16)
        qng = _gather_cols(q_norm_g[l][None, :], head_cols)
        kng = _gather_cols(k_norm_g[l][None, :], head_cols)
        qg = jnp.where(jnp.asarray(nope), qng * kng * (root * root * q_scale), qng * (root * q_scale))
        kg = kng[:, LANES:] * root
        bound = BOUND_MARGIN * jnp.maximum(jnp.max(jnp.abs(qg[:, :LANES])),
                                           jnp.max(jnp.abs(qg[:, LANES:])) * jnp.max(jnp.abs(kg)))
        const_lane = jnp.asarray(np.arange(LANES) == HALF)
        q_const = jnp.concatenate([jnp.zeros((1, LANES), F32), jnp.where(const_lane, -bound, 0.0)[None, :]], axis=1)
        k_const = jnp.where(const_lane, 1.0, 0.0)[None, :].astype(F32)
        xab, q, k, v = _project(x2, w_in_l, w_ab[l], w_q_l, w_kv_l, jnp.concatenate([qg, q_const]),
                                jnp.concatenate([kg, k_const]), cos_t, sin_t)
        yf = _fourier(xab, cmat, smat_neg, fourier_out_g[l][None, :], seq)
        o = lax.cond(bound <= SHIFT_SAFE,
                     lambda qkv: _attention(*qkv, batch, seq, row_max=False),
                     lambda qkv: _attention(*qkv, batch, seq, row_max=True), (q, k, v))
        x2 = _out_mlp(x2, yf, o, attn_out_g[l][None, :], w_out[l].astype(BF16),
                      mlp_norm_g[l][None, :], w_mlp_in[l].astype(BF16), w_mlp_out[l].astype(BF16))
    return x2.reshape(batch, seq, D_MODEL)
```

```python
import functools
import math

import numpy as np
import jax
import jax.numpy as jnp
from jax import lax
from jax.experimental import pallas as pl
from jax.experimental.pallas import tpu as pltpu

D_MODEL = 1024
F_GROUPS = 4
F_GROUP_DIM = 64
F_WIDTH = F_GROUPS * F_GROUP_DIM
N_HEADS = 6
Q_LORA = 256
KV_LORA = 256
QK_NOPE = 128
QK_ROPE = 64
V_DIM = 128
QK_DIM = QK_NOPE + QK_ROPE
A_WIDTH = N_HEADS * V_DIM
ROPE_BASE = 10000.0
D_FF = 4 * D_MODEL
EPS = 1e-6

LANES = 128
HEAD_PAD = 2 * LANES
HALF = QK_ROPE // 2
Z_WIDTH = F_WIDTH + Q_LORA + KV_LORA + LANES
VMEM_LIMIT = 56 * 1024 * 1024

ROW_TILE = 512
PROJ_TILE = 512
PROJ_ZROWS = 512
PROJ_SUB = 512
FLIP_BLOCK = 256
DFT_SPLIT = 64
Q_TILE = 256
ATTN_HEADS = 3
FF_CHUNK = 1024
SHIFT_SAFE = 60.0
BOUND_MARGIN = 1.02

BF16 = jnp.bfloat16
F32 = jnp.float32


def _rope_tile_cols(start):
    cols = np.full((LANES,), -1, np.int64)
    cols[0:HALF] = start + np.arange(HALF)
    cols[2 * HALF:3 * HALF] = start + HALF + np.arange(HALF)
    return cols


def _gather_cols(w, cols):
    taken = jnp.take(w, jnp.asarray(np.maximum(cols, 0)), axis=-1)
    return jnp.where(jnp.asarray(cols >= 0), taken, jnp.zeros((), w.dtype))


def _rms(x, g):
    return x * lax.rsqrt(jnp.mean(x * x, axis=-1, keepdims=True) + EPS) * g


def _rotate(t, cos_t, sin_t):
    return t * cos_t + pltpu.roll(t, 2 * HALF, axis=1) * sin_t


def _fold_kernel(cc_ref, sc_ref, w_ref, out_ref):
    out_ref[...] = jnp.zeros_like(out_ref)
    for g in range(F_GROUPS):
        w = w_ref[0, g]
        a = jnp.dot(cc_ref[...], w, preferred_element_type=F32, precision=lax.Precision.HIGHEST)
        b = jnp.dot(sc_ref[...], w, preferred_element_type=F32, precision=lax.Precision.HIGHEST)
        lo, hi = g * F_GROUP_DIM, (g + 1) * F_GROUP_DIM
        out_ref[0, lo:hi, lo:hi] = a
        out_ref[0, lo:hi, F_WIDTH + lo:F_WIDTH + hi] = b


def _fold_fourier_weights(w_fourier, seq):
    depth = w_fourier.shape[0]
    c = np.arange(F_GROUP_DIM)
    ang = 2.0 * np.pi * ((c[:, None] * c[None, :]) % F_GROUP_DIM) / F_GROUP_DIM
    ortho = 1.0 / math.sqrt(seq * F_GROUP_DIM)
    cc = jnp.asarray(np.cos(ang) * ortho, F32)
    sc = jnp.asarray(np.sin(ang) * ortho, F32)
    return pl.pallas_call(
        _fold_kernel,
        grid=(depth,),
        in_specs=[
            pl.BlockSpec((F_GROUP_DIM, F_GROUP_DIM), lambda l: (0, 0)),
            pl.BlockSpec((F_GROUP_DIM, F_GROUP_DIM), lambda l: (0, 0)),
            pl.BlockSpec((1, F_GROUPS, F_GROUP_DIM, F_GROUP_DIM), lambda l: (l, 0, 0, 0)),
        ],
        out_specs=pl.BlockSpec((1, F_WIDTH, 2 * F_WIDTH), lambda l: (l, 0, 0)),
        out_shape=jax.ShapeDtypeStruct((depth, F_WIDTH, 2 * F_WIDTH), F32),
        name="fourier_fold",
    )(cc, sc, w_fourier)


def _trig_kernel(ang_ref, cos_ref, sin_ref):
    a = ang_ref[...]
    lane = lax.broadcasted_iota(jnp.int32, a.shape, 1)
    s = jnp.sin(a)
    cos_ref[...] = jnp.where(lane % (2 * HALF) < HALF, jnp.cos(a), 0.0)
    sin_ref[...] = jnp.where(lane < HALF, -s, jnp.where((lane >= 2 * HALF) & (lane < 3 * HALF), s, 0.0))


def _lane_freqs():
    inv_freq = ROPE_BASE ** (-jnp.arange(HALF, dtype=F32) / HALF)
    return jnp.tile(inv_freq, LANES // HALF)[None, :]


def _rotary_tables(positions):
    n = positions.size
    ang = positions.astype(F32).reshape(n, 1) * _lane_freqs()
    tile = min(2048, n)
    spec = pl.BlockSpec((tile, LANES), lambda i: (i, 0))
    return pl.pallas_call(
        _trig_kernel,
        grid=(n // tile,),
        in_specs=[spec],
        out_specs=[spec, spec],
        out_shape=[jax.ShapeDtypeStruct((n, LANES), F32)] * 2,
        name="rotary_trig",
    )(ang)


def _plain_trig_kernel(ang_ref, cos_ref, sin_ref):
    a = ang_ref[...]
    cos_ref[...] = jnp.cos(a)
    sin_ref[...] = jnp.sin(a)


def _offset_combine_kernel(cs_ref, sn_ref, co_ref, so_ref, cos_ref, sin_ref):
    cs, sn = cs_ref[...], sn_ref[...]
    co, so = co_ref[0], so_ref[0]
    c = cs * co - sn * so
    s = sn * co + cs * so
    lane = lax.broadcasted_iota(jnp.int32, c.shape, 1)
    cos_ref[...] = jnp.where(lane % (2 * HALF) < HALF, c, 0.0)
    sin_ref[...] = jnp.where(lane < HALF, -s, jnp.where((lane >= 2 * HALF) & (lane < 3 * HALF), s, 0.0))


def _rotary_tables_offset(positions):
    batch, seq = positions.shape
    freqs = _lane_freqs()
    step_ang = jnp.arange(seq, dtype=F32)[:, None] * freqs
    pad = -batch % 8
    base_ang = jnp.pad(positions[:, 0].astype(F32), (0, pad))[:, None] * freqs

    def trig(ang):
        return pl.pallas_call(
            _plain_trig_kernel,
            out_shape=[jax.ShapeDtypeStruct(ang.shape, F32)] * 2,
            name="rotary_trig_small",
        )(ang)

    cs, sn = trig(step_ang)
    co, so = trig(base_ang)
    co, so = co[:batch, None, :], so[:batch, None, :]
    table = pl.BlockSpec((seq, LANES), lambda b: (b, 0))
    row = pl.BlockSpec((1, 1, LANES), lambda b: (b, 0, 0))
    return pl.pallas_call(
        _offset_combine_kernel,
        grid=(batch,),
        in_specs=[_const_spec((seq, LANES)), _const_spec((seq, LANES)), row, row],
        out_specs=[table, table],
        out_shape=[jax.ShapeDtypeStruct((batch * seq, LANES), F32)] * 2,
        name="rotary_combine",
    )(cs, sn, co, so)


def _unit_rms(x):
    return x * lax.rsqrt(jnp.mean(x * x, axis=-1, keepdims=True) + EPS)


def _proj_matmul_stage(x_ref, win_ref, wab_ref, wq_ref, wkv_ref, xab_ref, qf_ref, kvf_ref, kpe_ref):
    for i in range(x_ref.shape[0] // PROJ_ZROWS):
        rows = slice(i * PROJ_ZROWS, (i + 1) * PROJ_ZROWS)
        h = _unit_rms(x_ref[rows, :])
        z = jnp.dot(h.astype(BF16), win_ref[...], preferred_element_type=F32)
        xab_ref[rows, :] = jnp.dot(z[:, :F_WIDTH].astype(BF16), wab_ref[...],
                                   preferred_element_type=F32).astype(BF16)
        cqn = _unit_rms(z[:, F_WIDTH:F_WIDTH + Q_LORA]).astype(BF16)
        ckvn = _unit_rms(z[:, F_WIDTH + Q_LORA:F_WIDTH + Q_LORA + KV_LORA]).astype(BF16)
        kpe_ref[rows, :] = z[:, F_WIDTH + Q_LORA + KV_LORA:]
        qf_ref[rows, :] = jnp.dot(cqn, wq_ref[...], preferred_element_type=F32)
        kvf_ref[rows, :] = jnp.dot(ckvn, wkv_ref[...], preferred_element_type=F32)


def _proj_norm_stage(qf_ref, kvf_ref, kpe_ref, qg_ref, kg_ref, cos_ref, sin_ref, q_ref, k_ref, v_ref):
    qg = qg_ref[0:1, :]
    qg_nope, qg_rope = qg[:, :LANES], qg[:, LANES:]
    q_const = qg_ref[1:2, LANES:]
    kg_rope = kg_ref[0:1, :]
    k_const = kg_ref[1:2, :]
    eps_sum = QK_DIM * EPS
    for j in range(qf_ref.shape[0] // PROJ_SUB):
        rows = slice(j * PROJ_SUB, (j + 1) * PROJ_SUB)
        k_pe = kpe_ref[rows, :]
        cos_t = cos_ref[rows, :]
        sin_t = sin_ref[rows, :]
        k_rot = _rotate(k_pe * kg_rope, cos_t, sin_t)
        ss_pe = jnp.sum(k_pe * k_pe, axis=-1, keepdims=True) + eps_sum
        for hd in range(N_HEADS):
            qh = qf_ref[rows, hd * HEAD_PAD:(hd + 1) * HEAD_PAD]
            rq = lax.rsqrt(jnp.sum(qh * qh, axis=-1, keepdims=True) + eps_sum)
            q_ref[hd, rows, :LANES] = (qh[:, :LANES] * rq * qg_nope).astype(BF16)
            q_rot = _rotate(qh[:, LANES:] * rq * qg_rope, cos_t, sin_t)
            q_ref[hd, rows, LANES:] = (q_rot + q_const).astype(BF16)
            kn = kvf_ref[rows, hd * HEAD_PAD:hd * HEAD_PAD + LANES]
            rk = lax.rsqrt(jnp.sum(kn * kn, axis=-1, keepdims=True) + ss_pe)
            k_ref[hd, rows, :LANES] = (kn * rk).astype(BF16)
            k_ref[hd, rows, LANES:] = (k_rot * rk + k_const).astype(BF16)
            v_ref[hd, rows, :] = kvf_ref[rows, hd * HEAD_PAD + LANES:(hd + 1) * HEAD_PAD].astype(BF16)


def _proj_kernel(x_ref, win_ref, wab_ref, wq_ref, wkv_ref, qg_ref, kg_ref, cos_ref, sin_ref,
                 xab_ref, q_ref, k_ref, v_ref, qf0, kvf0, kpe0, qf1, kvf1, kpe1):
    step = pl.program_id(0)

    @pl.when(step == 0)
    def _():
        qf1[...] = jnp.zeros_like(qf1)
        kvf1[...] = jnp.zeros_like(kvf1)
        kpe1[...] = jnp.zeros_like(kpe1)

    def body(fill, drain):
        _proj_matmul_stage(x_ref, win_ref, wab_ref, wq_ref, wkv_ref, xab_ref, *fill)
        _proj_norm_stage(*drain, qg_ref, kg_ref, cos_ref, sin_ref, q_ref, k_ref, v_ref)

    @pl.when(step % 2 == 0)
    def _():
        body((qf0, kvf0, kpe0), (qf1, kvf1, kpe1))

    @pl.when(step % 2 == 1)
    def _():
        body((qf1, kvf1, kpe1), (qf0, kvf0, kpe0))


def _const_spec(shape):
    return pl.BlockSpec(shape, lambda *_: (0,) * len(shape))


def _project(x2, w_in, w_ab, w_q, w_kv, qg, kg, cos_t, sin_t):
    n = x2.shape[0]
    tm = PROJ_TILE
    nt = n // tm
    fill = lambda w: pl.BlockSpec((tm, w), lambda i: (jnp.minimum(i, nt - 1), 0))
    drain = lambda w: pl.BlockSpec((tm, w), lambda i: (jnp.maximum(i - 1, 0), 0))
    head = lambda w: pl.BlockSpec((N_HEADS, tm, w), lambda i: (0, jnp.maximum(i - 1, 0), 0))
    up_width = N_HEADS * HEAD_PAD
    handoff = [pltpu.VMEM((tm, up_width), F32), pltpu.VMEM((tm, up_width), F32), pltpu.VMEM((tm, LANES), F32)]
    return pl.pallas_call(
        _proj_kernel,
        grid=(nt + 1,),
        in_specs=[
            fill(D_MODEL), _const_spec((D_MODEL, Z_WIDTH)), _const_spec((F_WIDTH, 2 * F_WIDTH)),
            _const_spec((Q_LORA, up_width)), _const_spec((KV_LORA, up_width)),
            _const_spec((2, HEAD_PAD)), _const_spec((2, LANES)), drain(LANES), drain(LANES),
        ],
        out_specs=[fill(2 * F_WIDTH), head(HEAD_PAD), head(HEAD_PAD), head(V_DIM)],
        out_shape=[
            jax.ShapeDtypeStruct((n, 2 * F_WIDTH), BF16),
            jax.ShapeDtypeStruct((N_HEADS, n, HEAD_PAD), BF16),
            jax.ShapeDtypeStruct((N_HEADS, n, HEAD_PAD), BF16),
            jax.ShapeDtypeStruct((N_HEADS, n, V_DIM), BF16),
        ],
        scratch_shapes=handoff + handoff,
        compiler_params=pltpu.CompilerParams(dimension_semantics=("arbitrary",),
                                             vmem_limit_bytes=VMEM_LIMIT),
        name="in_proj",
    )(x2, w_in, w_ab, w_q, w_kv, qg, kg, cos_t, sin_t)


def _fourier_kernel(c_ref, s_ref, flip_ref, lo_ref, hi_ref, g_ref, y_ref):
    half = lo_ref.shape[0]
    nblk = half // FLIP_BLOCK
    flipped = [jnp.dot(flip_ref[...], hi_ref[(nblk - 1 - i) * FLIP_BLOCK:(nblk - i) * FLIP_BLOCK, :],
                       preferred_element_type=F32) for i in range(nblk)]
    rev = pltpu.roll(jnp.concatenate(flipped, axis=0), 1, axis=0)
    lo = lo_ref[...].astype(F32)
    up = (lo[:, :F_WIDTH] + rev[:, :F_WIDTH]).astype(BF16)
    um = (lo[:, F_WIDTH:] - rev[:, F_WIDTH:]).astype(BF16)
    y = (jnp.dot(c_ref[...], up, preferred_element_type=F32)
         + jnp.dot(s_ref[...], um, preferred_element_type=F32))
    mid = rev[0:1, :F_WIDTH]
    odd = (lax.broadcasted_iota(jnp.int32, y.shape, 0) & 1) == 1
    y = y - jnp.where(odd, 2.0 * mid, 0.0)
    y_ref[...] = _rms(y, g_ref[...]).astype(BF16)


def _fourier(xab, cmat, smat_neg, g, seq):
    n = xab.shape[0]
    half = seq // 2
    flip = jnp.asarray(np.eye(FLIP_BLOCK)[::-1], BF16)
    return pl.pallas_call(
        _fourier_kernel,
        grid=(n // seq,),
        in_specs=[
            _const_spec((seq, half)), _const_spec((seq, half)), _const_spec((FLIP_BLOCK, FLIP_BLOCK)),
            pl.BlockSpec((half, 2 * F_WIDTH), lambda b: (2 * b, 0)),
            pl.BlockSpec((half, 2 * F_WIDTH), lambda b: (2 * b + 1, 0)), _const_spec((1, F_WIDTH)),
        ],
        out_specs=pl.BlockSpec((seq, F_WIDTH), lambda b: (b, 0)),
        out_shape=jax.ShapeDtypeStruct((n, F_WIDTH), BF16),
        compiler_params=pltpu.CompilerParams(dimension_semantics=("arbitrary",),
                                             vmem_limit_bytes=VMEM_LIMIT),
        name="seq_dft",
    )(cmat, smat_neg, flip, xab, xab, g)


def _attn_kernel(q_ref, k_ref, v_ref, o_ref, vx_ref, *, heads, row_max):
    seq = q_ref.shape[1]
    for hd in range(heads):
        vx_ref[hd, :, :V_DIM] = v_ref[hd]
        vx_ref[hd, :, V_DIM:] = jnp.ones((seq, V_DIM), BF16)
    for hd in range(heads):
        for j in range(seq // Q_TILE):
            rows = slice(j * Q_TILE, (j + 1) * Q_TILE)
            s = lax.dot_general(q_ref[hd, rows, :], k_ref[hd], (((1,), (1,)), ((), ())),
                                preferred_element_type=F32)
            if row_max:
                s = s - jnp.max(s, axis=-1, keepdims=True)
            p = jnp.exp2(s).astype(BF16)
            ox = jnp.dot(p, vx_ref[hd], preferred_element_type=F32)
            o_ref[rows, hd * V_DIM:(hd + 1) * V_DIM] = (ox[:, :V_DIM] / ox[:, V_DIM:]).astype(BF16)


def _attention(q, k, v, batch, seq, row_max):
    n = q.shape[1]
    g = 1 if row_max else ATTN_HEADS
    return pl.pallas_call(
        functools.partial(_attn_kernel, heads=g, row_max=row_max),
        grid=(batch, N_HEADS // g),
        in_specs=[
            pl.BlockSpec((g, seq, HEAD_PAD), lambda b, h: (h, b, 0)),
            pl.BlockSpec((g, seq, HEAD_PAD), lambda b, h: (h, b, 0)),
            pl.BlockSpec((g, seq, V_DIM), lambda b, h: (h, b, 0)),
        ],
        out_specs=pl.BlockSpec((seq, g * V_DIM), lambda b, h: (b, h)),
        out_shape=jax.ShapeDtypeStruct((n, A_WIDTH), BF16),
        scratch_shapes=[pltpu.VMEM((g, seq, 2 * V_DIM), BF16)],
        compiler_params=pltpu.CompilerParams(
            dimension_semantics=("arbitrary", "arbitrary"),
            vmem_limit_bytes=VMEM_LIMIT),
        name="attention_rowmax" if row_max else "attention",
    )(q, k, v)


def _out_mlp_kernel(x_ref, yf_ref, o_ref, ag_ref, wo_ref, mg_ref, w1_ref, w2_ref, out_ref, act_ref):
    ya = _rms(o_ref[...].astype(F32), ag_ref[...]).astype(BF16)
    x1 = (x_ref[...]
          + jnp.dot(yf_ref[...], wo_ref[:F_WIDTH, :], preferred_element_type=F32)
          + jnp.dot(ya, wo_ref[F_WIDTH:, :], preferred_element_type=F32))
    hn = _rms(x1, mg_ref[...]).astype(BF16)
    for c in range(D_FF // FF_CHUNK):
        cols = slice(c * FF_CHUNK, (c + 1) * FF_CHUNK)
        hm = jnp.dot(hn, w1_ref[:, cols], preferred_element_type=F32)
        act_ref[:, cols] = jnp.square(jnp.maximum(hm, 0.0)).astype(BF16)
    out_ref[...] = x1 + jnp.dot(act_ref[...], w2_ref[...], preferred_element_type=F32)


def _out_mlp(x2, yf, o, ag, w_out, mg, w1, w2):
    n = x2.shape[0]
    tm = ROW_TILE
    row = lambda w: pl.BlockSpec((tm, w), lambda i: (i, 0))
    return pl.pallas_call(
        _out_mlp_kernel,
        grid=(n // tm,),
        in_specs=[
            row(D_MODEL), row(F_WIDTH), row(A_WIDTH), _const_spec((1, A_WIDTH)),
            _const_spec((D_MODEL, D_MODEL)), _const_spec((1, D_MODEL)),
            _const_spec((D_MODEL, D_FF)), _const_spec((D_FF, D_MODEL)),
        ],
        out_specs=row(D_MODEL),
        out_shape=jax.ShapeDtypeStruct((n, D_MODEL), F32),
        scratch_shapes=[pltpu.VMEM((tm, D_FF), BF16)],
        compiler_params=pltpu.CompilerParams(dimension_semantics=("arbitrary",),
                                             vmem_limit_bytes=VMEM_LIMIT),
        name="out_mlp",
    )(x2, yf, o, ag, w_out, mg, w1, w2)


def _dft_matrices(seq):
    lo_n = DFT_SPLIT
    hi_n = seq // lo_n
    half = seq // 2
    n_idx = np.arange(half)[None, :]
    ang_hi = 2.0 * np.pi * ((n_idx * lo_n * np.arange(hi_n)[:, None]) % seq) / seq
    ang_lo = 2.0 * np.pi * ((n_idx * np.arange(lo_n)[:, None]) % seq) / seq
    ch, sh = jnp.asarray(np.cos(ang_hi), F32)[:, None, :], jnp.asarray(np.sin(ang_hi), F32)[:, None, :]
    cl, sl = jnp.asarray(np.cos(ang_lo), F32)[None, :, :], jnp.asarray(np.sin(ang_lo), F32)[None, :, :]
    cmat = (ch * cl - sh * sl).astype(BF16).reshape(seq, half)
    smat_neg = (-(sh * cl + ch * sl)).astype(BF16).reshape(seq, half)
    return cmat, smat_neg


def kernel(x, positions, attn_norm_g, w_in, w_fourier, q_a_g, w_q_up, kv_a_g, w_kv_up, q_norm_g,
           k_norm_g, fourier_out_g, attn_out_g, w_out, mlp_norm_g, w_mlp_in, w_mlp_out):
    batch, seq, _ = x.shape
    depth = w_in.shape[0]
    n = batch * seq

    consecutive = jnp.all(positions == positions[:, :1] + jnp.arange(seq, dtype=positions.dtype)[None, :])
    cos_t, sin_t = lax.cond(consecutive, _rotary_tables_offset, _rotary_tables, positions)
    cmat, smat_neg = _dft_matrices(seq)
    w_ab = _fold_fourier_weights(w_fourier, seq).astype(BF16)

    rope_cols = _rope_tile_cols(0)
    z_cols = np.concatenate([np.arange(F_WIDTH + Q_LORA + KV_LORA),
                             np.where(rope_cols >= 0, rope_cols + F_WIDTH + Q_LORA + KV_LORA, -1)])
    head_cols = np.concatenate([np.arange(QK_NOPE), _rope_tile_cols(QK_NOPE)])
    q_cols = np.concatenate([np.where(head_cols >= 0, head_cols + hd * QK_DIM, -1)
                             for hd in range(N_HEADS)])

    root = math.sqrt(QK_DIM)
    q_scale = math.log2(math.e) / root
    nope = np.arange(HEAD_PAD) < QK_NOPE

    x2 = x.reshape(n, D_MODEL)
    for l in range(depth):
        w_in_l = _gather_cols(w_in[l] * attn_norm_g[l][:, None], z_cols).astype(BF16)
        w_q_l = _gather_cols(w_q_up[l] * q_a_g[l][:, None], q_cols).astype(BF16)
        w_kv_l = (w_kv_up[l] * kv_a_g[l][:, None]).astype(BF16)
        qng = _gather_cols(q_norm_g[l][None, :], head_cols)
        kng = _gather_cols(k_norm_g[l][None, :], head_cols)
        qg = jnp.where(jnp.asarray(nope), qng * kng * (root * root * q_scale), qng * (root * q_scale))
        kg = kng[:, LANES:] * root
        bound = BOUND_MARGIN * jnp.maximum(jnp.max(jnp.abs(qg[:, :LANES])),
                                           jnp.max(jnp.abs(qg[:, LANES:])) * jnp.max(jnp.abs(kg)))
        const_lane = jnp.asarray(np.arange(LANES) == HALF)
        q_const = jnp.concatenate([jnp.zeros((1, LANES), F32), jnp.where(const_lane, -bound, 0.0)[None, :]], axis=1)
        k_const = jnp.where(const_lane, 1.0, 0.0)[None, :].astype(F32)
        xab, q, k, v = _project(x2, w_in_l, w_ab[l], w_q_l, w_kv_l, jnp.concatenate([qg, q_const]),
                                jnp.concatenate([kg, k_const]), cos_t, sin_t)
        yf = _fourier(xab, cmat, smat_neg, fourier_out_g[l][None, :], seq)
        o = lax.cond(bound <= SHIFT_SAFE,
                     lambda qkv: _attention(*qkv, batch, seq, row_max=False),
                     lambda qkv: _attention(*qkv, batch, seq, row_max=True), (q, k, v))
        x2 = _out_mlp(x2, yf, o, attn_out_g[l][None, :], w_out[l],
                      mlp_norm_g[l][None, :], w_mlp_in[l], w_mlp_out[l])
    return x2.reshape(batch, seq, D_MODEL)
```

```python
import functools
import math

import numpy as np
import jax
import jax.numpy as jnp
from jax import lax
from jax.experimental import pallas as pl
from jax.experimental.pallas import tpu as pltpu

D_MODEL = 1024
F_GROUPS = 4
F_GROUP_DIM = 64
F_WIDTH = F_GROUPS * F_GROUP_DIM
N_HEADS = 6
Q_LORA = 256
KV_LORA = 256
QK_NOPE = 128
QK_ROPE = 64
V_DIM = 128
QK_DIM = QK_NOPE + QK_ROPE
A_WIDTH = N_HEADS * V_DIM
ROPE_BASE = 10000.0
D_FF = 4 * D_MODEL
EPS = 1e-6

LANES = 128
HEAD_PAD = 2 * LANES
HALF = QK_ROPE // 2
Z_WIDTH = F_WIDTH + Q_LORA + KV_LORA + LANES
VMEM_LIMIT = 56 * 1024 * 1024

ROW_TILE = 512
PROJ_TILE = 512
PROJ_ZROWS = 512
PROJ_SUB = 512
FLIP_BLOCK = 256
DFT_SPLIT = 64
Q_TILE = 256
ATTN_HEADS = 3
FF_CHUNK = 1024
SHIFT_SAFE = 60.0
BOUND_MARGIN = 1.02

BF16 = jnp.bfloat16
F32 = jnp.float32


def _rope_tile_cols(start):
    cols = np.full((LANES,), -1, np.int64)
    cols[0:HALF] = start + np.arange(HALF)
    cols[2 * HALF:3 * HALF] = start + HALF + np.arange(HALF)
    return cols


def _gather_cols(w, cols):
    taken = jnp.take(w, jnp.asarray(np.maximum(cols, 0)), axis=-1)
    return jnp.where(jnp.asarray(cols >= 0), taken, jnp.zeros((), w.dtype))


def _rms(x, g):
    return x * lax.rsqrt(jnp.mean(x * x, axis=-1, keepdims=True) + EPS) * g


def _rotate(t, cos_t, sin_t):
    return t * cos_t + pltpu.roll(t, 2 * HALF, axis=1) * sin_t


def _fold_kernel(cc_ref, sc_ref, w_ref, out_ref):
    out_ref[...] = jnp.zeros_like(out_ref)
    for g in range(F_GROUPS):
        w = w_ref[0, g]
        a = jnp.dot(cc_ref[...], w, preferred_element_type=F32, precision=lax.Precision.HIGHEST)
        b = jnp.dot(sc_ref[...], w, preferred_element_type=F32, precision=lax.Precision.HIGHEST)
        lo, hi = g * F_GROUP_DIM, (g + 1) * F_GROUP_DIM
        out_ref[0, lo:hi, lo:hi] = a
        out_ref[0, lo:hi, F_WIDTH + lo:F_WIDTH + hi] = b


def _fold_fourier_weights(w_fourier, seq):
    depth = w_fourier.shape[0]
    c = np.arange(F_GROUP_DIM)
    ang = 2.0 * np.pi * ((c[:, None] * c[None, :]) % F_GROUP_DIM) / F_GROUP_DIM
    ortho = 1.0 / math.sqrt(seq * F_GROUP_DIM)
    cc = jnp.asarray(np.cos(ang) * ortho, F32)
    sc = jnp.asarray(np.sin(ang) * ortho, F32)
    return pl.pallas_call(
        _fold_kernel,
        grid=(depth,),
        in_specs=[
            pl.BlockSpec((F_GROUP_DIM, F_GROUP_DIM), lambda l: (0, 0)),
            pl.BlockSpec((F_GROUP_DIM, F_GROUP_DIM), lambda l: (0, 0)),
            pl.BlockSpec((1, F_GROUPS, F_GROUP_DIM, F_GROUP_DIM), lambda l: (l, 0, 0, 0)),
        ],
        out_specs=pl.BlockSpec((1, F_WIDTH, 2 * F_WIDTH), lambda l: (l, 0, 0)),
        out_shape=jax.ShapeDtypeStruct((depth, F_WIDTH, 2 * F_WIDTH), F32),
        name="fourier_fold",
    )(cc, sc, w_fourier)


def _trig_kernel(ang_ref, cos_ref, sin_ref):
    a = ang_ref[...]
    lane = lax.broadcasted_iota(jnp.int32, a.shape, 1)
    s = jnp.sin(a)
    cos_ref[...] = jnp.where(lane % (2 * HALF) < HALF, jnp.cos(a), 0.0)
    sin_ref[...] = jnp.where(lane < HALF, -s, jnp.where((lane >= 2 * HALF) & (lane < 3 * HALF), s, 0.0))


def _lane_freqs():
    inv_freq = ROPE_BASE ** (-jnp.arange(HALF, dtype=F32) / HALF)
    return jnp.tile(inv_freq, LANES // HALF)[None, :]


def _rotary_tables(positions):
    n = positions.size
    ang = positions.astype(F32).reshape(n, 1) * _lane_freqs()
    tile = min(2048, n)
    spec = pl.BlockSpec((tile, LANES), lambda i: (i, 0))
    return pl.pallas_call(
        _trig_kernel,
        grid=(n // tile,),
        in_specs=[spec],
        out_specs=[spec, spec],
        out_shape=[jax.ShapeDtypeStruct((n, LANES), F32)] * 2,
        name="rotary_trig",
    )(ang)


def _plain_trig_kernel(ang_ref, cos_ref, sin_ref):
    a = ang_ref[...]
    cos_ref[...] = jnp.cos(a)
    sin_ref[...] = jnp.sin(a)


def _offset_combine_kernel(cs_ref, sn_ref, co_ref, so_ref, cos_ref, sin_ref):
    cs, sn = cs_ref[...], sn_ref[...]
    co, so = co_ref[0], so_ref[0]
    c = cs * co - sn * so
    s = sn * co + cs * so
    lane = lax.broadcasted_iota(jnp.int32, c.shape, 1)
    cos_ref[...] = jnp.where(lane % (2 * HALF) < HALF, c, 0.0)
    sin_ref[...] = jnp.where(lane < HALF, -s, jnp.where((lane >= 2 * HALF) & (lane < 3 * HALF), s, 0.0))


def _rotary_tables_offset(positions):
    batch, seq = positions.shape
    freqs = _lane_freqs()
    step_ang = jnp.arange(seq, dtype=F32)[:, None] * freqs
    pad = -batch % 8
    base_ang = jnp.pad(positions[:, 0].astype(F32), (0, pad))[:, None] * freqs

    def trig(ang):
        return pl.pallas_call(
            _plain_trig_kernel,
            out_shape=[jax.ShapeDtypeStruct(ang.shape, F32)] * 2,
            name="rotary_trig_small",
        )(ang)

    cs, sn = trig(step_ang)
    co, so = trig(base_ang)
    co, so = co[:batch, None, :], so[:batch, None, :]
    table = pl.BlockSpec((seq, LANES), lambda b: (b, 0))
    row = pl.BlockSpec((1, 1, LANES), lambda b: (b, 0, 0))
    return pl.pallas_call(
        _offset_combine_kernel,
        grid=(batch,),
        in_specs=[_const_spec((seq, LANES)), _const_spec((seq, LANES)), row, row],
        out_specs=[table, table],
        out_shape=[jax.ShapeDtypeStruct((batch * seq, LANES), F32)] * 2,
        name="rotary_combine",
    )(cs, sn, co, so)


def _unit_rms(x):
    return x * lax.rsqrt(jnp.mean(x * x, axis=-1, keepdims=True) + EPS)


def _proj_matmul_stage(x_ref, win_ref, wab_ref, wq_ref, wkv_ref, xab_ref, qf_ref, kvf_ref, kpe_ref):
    for i in range(x_ref.shape[0] // PROJ_ZROWS):
        rows = slice(i * PROJ_ZROWS, (i + 1) * PROJ_ZROWS)
        h = _unit_rms(x_ref[rows, :])
        z = jnp.dot(h.astype(BF16), win_ref[...], preferred_element_type=F32)
        xab_ref[rows, :] = jnp.dot(z[:, :F_WIDTH].astype(BF16), wab_ref[...],
                                   preferred_element_type=F32).astype(BF16)
        cqn = _unit_rms(z[:, F_WIDTH:F_WIDTH + Q_LORA]).astype(BF16)
        ckvn = _unit_rms(z[:, F_WIDTH + Q_LORA:F_WIDTH + Q_LORA + KV_LORA]).astype(BF16)
        kpe_ref[rows, :] = z[:, F_WIDTH + Q_LORA + KV_LORA:]
        qf_ref[rows, :] = jnp.dot(cqn, wq_ref[...], preferred_element_type=F32)
        kvf_ref[rows, :] = jnp.dot(ckvn, wkv_ref[...], preferred_element_type=F32)


def _proj_norm_stage(qf_ref, kvf_ref, kpe_ref, qg_ref, kg_ref, cos_ref, sin_ref, q_ref, k_ref, v_ref):
    qg = qg_ref[0:1, :]
    qg_nope, qg_rope = qg[:, :LANES], qg[:, LANES:]
    q_const = qg_ref[1:2, LANES:]
    kg_rope = kg_ref[0:1, :]
    k_const = kg_ref[1:2, :]
    eps_sum = QK_DIM * EPS
    for j in range(qf_ref.shape[0] // PROJ_SUB):
        rows = slice(j * PROJ_SUB, (j + 1) * PROJ_SUB)
        k_pe = kpe_ref[rows, :]
        cos_t = cos_ref[rows, :]
        sin_t = sin_ref[rows, :]
        k_rot = _rotate(k_pe * kg_rope, cos_t, sin_t)
        ss_pe = jnp.sum(k_pe * k_pe, axis=-1, keepdims=True) + eps_sum
        for hd in range(N_HEADS):
            qh = qf_ref[rows, hd * HEAD_PAD:(hd + 1) * HEAD_PAD]
            rq = lax.rsqrt(jnp.sum(qh * qh, axis=-1, keepdims=True) + eps_sum)
            q_ref[hd, rows, :LANES] = (qh[:, :LANES] * rq * qg_nope).astype(BF16)
            q_rot = _rotate(qh[:, LANES:] * rq * qg_rope, cos_t, sin_t)
            q_ref[hd, rows, LANES:] = (q_rot + q_const).astype(BF16)
            kn = kvf_ref[rows, hd * HEAD_PAD:hd * HEAD_PAD + LANES]
            rk = lax.rsqrt(jnp.sum(kn * kn, axis=-1, keepdims=True) + ss_pe)
            k_ref[hd, rows, :LANES] = (kn * rk).astype(BF16)
            k_ref[hd, rows, LANES:] = (k_rot * rk + k_const).astype(BF16)
            v_ref[hd, rows, :] = kvf_ref[rows, hd * HEAD_PAD + LANES:(hd + 1) * HEAD_PAD].astype(BF16)


def _proj_kernel(x_ref, win_ref, wab_ref, wq_ref, wkv_ref, qg_ref, kg_ref, cos_ref, sin_ref,
                 xab_ref, q_ref, k_ref, v_ref, qf0, kvf0, kpe0, qf1, kvf1, kpe1):
    step = pl.program_id(0)

    @pl.when(step == 0)
    def _():
        qf1[...] = jnp.zeros_like(qf1)
        kvf1[...] = jnp.zeros_like(kvf1)
        kpe1[...] = jnp.zeros_like(kpe1)

    def body(fill, drain):
        _proj_matmul_stage(x_ref, win_ref, wab_ref, wq_ref, wkv_ref, xab_ref, *fill)
        _proj_norm_stage(*drain, qg_ref, kg_ref, cos_ref, sin_ref, q_ref, k_ref, v_ref)

    @pl.when(step % 2 == 0)
    def _():
        body((qf0, kvf0, kpe0), (qf1, kvf1, kpe1))

    @pl.when(step % 2 == 1)
    def _():
        body((qf1, kvf1, kpe1), (qf0, kvf0, kpe0))


def _const_spec(shape):
    return pl.BlockSpec(shape, lambda *_: (0,) * len(shape))


def _project(x2, w_in, w_ab, w_q, w_kv, qg, kg, cos_t, sin_t):
    n = x2.shape[0]
    tm = PROJ_TILE
    nt = n // tm
    fill = lambda w: pl.BlockSpec((tm, w), lambda i: (jnp.minimum(i, nt - 1), 0))
    drain = lambda w: pl.BlockSpec((tm, w), lambda i: (jnp.maximum(i - 1, 0), 0))
    head = lambda w: pl.BlockSpec((N_HEADS, tm, w), lambda i: (0, jnp.maximum(i - 1, 0), 0))
    up_width = N_HEADS * HEAD_PAD
    handoff = [pltpu.VMEM((tm, up_width), F32), pltpu.VMEM((tm, up_width), F32), pltpu.VMEM((tm, LANES), F32)]
    return pl.pallas_call(
        _proj_kernel,
        grid=(nt + 1,),
        in_specs=[
            fill(D_MODEL), _const_spec((D_MODEL, Z_WIDTH)), _const_spec((F_WIDTH, 2 * F_WIDTH)),
            _const_spec((Q_LORA, up_width)), _const_spec((KV_LORA, up_width)),
            _const_spec((2, HEAD_PAD)), _const_spec((2, LANES)), drain(LANES), drain(LANES),
        ],
        out_specs=[fill(2 * F_WIDTH), head(HEAD_PAD), head(HEAD_PAD), head(V_DIM)],
        out_shape=[
            jax.ShapeDtypeStruct((n, 2 * F_WIDTH), BF16),
            jax.ShapeDtypeStruct((N_HEADS, n, HEAD_PAD), BF16),
            jax.ShapeDtypeStruct((N_HEADS, n, HEAD_PAD), BF16),
            jax.ShapeDtypeStruct((N_HEADS, n, V_DIM), BF16),
        ],
        scratch_shapes=handoff + handoff,
        compiler_params=pltpu.CompilerParams(dimension_semantics=("arbitrary",),
                                             vmem_limit_bytes=VMEM_LIMIT),
        name="in_proj",
    )(x2, w_in, w_ab, w_q, w_kv, qg, kg, cos_t, sin_t)


def _fourier_kernel(c_ref, s_ref, flip_ref, lo_ref, hi_ref, g_ref, y_ref):
    half = lo_ref.shape[0]
    nblk = half // FLIP_BLOCK
    flipped = [jnp.dot(flip_ref[...], hi_ref[(nblk - 1 - i) * FLIP_BLOCK:(nblk - i) * FLIP_BLOCK, :],
                       preferred_element_type=F32) for i in range(nblk)]
    rev = pltpu.roll(jnp.concatenate(flipped, axis=0), 1, axis=0)
    lo = lo_ref[...].astype(F32)
    up = (lo[:, :F_WIDTH] + rev[:, :F_WIDTH]).astype(BF16)
    um = (lo[:, F_WIDTH:] - rev[:, F_WIDTH:]).astype(BF16)
    y = (jnp.dot(c_ref[...], up, preferred_element_type=F32)
         + jnp.dot(s_ref[...], um, preferred_element_type=F32))
    mid = rev[0:1, :F_WIDTH]
    odd = (lax.broadcasted_iota(jnp.int32, y.shape, 0) & 1) == 1
    y = y - jnp.where(odd, 2.0 * mid, 0.0)
    y_ref[...] = _rms(y, g_ref[...]).astype(BF16)


def _fourier(xab, cmat, smat_neg, g, seq):
    n = xab.shape[0]
    half = seq // 2
    flip = jnp.asarray(np.eye(FLIP_BLOCK)[::-1], BF16)
    return pl.pallas_call(
        _fourier_kernel,
        grid=(n // seq,),
        in_specs=[
            _const_spec((seq, half)), _const_spec((seq, half)), _const_spec((FLIP_BLOCK, FLIP_BLOCK)),
            pl.BlockSpec((half, 2 * F_WIDTH), lambda b: (2 * b, 0)),
            pl.BlockSpec((half, 2 * F_WIDTH), lambda b: (2 * b + 1, 0)), _const_spec((1, F_WIDTH)),
        ],
        out_specs=pl.BlockSpec((seq, F_WIDTH), lambda b: (b, 0)),
        out_shape=jax.ShapeDtypeStruct((n, F_WIDTH), BF16),
        compiler_params=pltpu.CompilerParams(dimension_semantics=("arbitrary",),
                                             vmem_limit_bytes=VMEM_LIMIT),
        name="seq_dft",
    )(cmat, smat_neg, flip, xab, xab, g)


def _attn_kernel(q_ref, k_ref, v_ref, o_ref, vx_ref, *, heads, row_max):
    seq = q_ref.shape[1]
    for hd in range(heads):
        vx_ref[hd, :, :V_DIM] = v_ref[hd]
        vx_ref[hd, :, V_DIM:] = jnp.ones((seq, V_DIM), BF16)
    for hd in range(heads):
        for j in range(seq // Q_TILE):
            rows = slice(j * Q_TILE, (j + 1) * Q_TILE)
            s = lax.dot_general(q_ref[hd, rows, :], k_ref[hd], (((1,), (1,)), ((), ())),
                                preferred_element_type=F32)
            if row_max:
                s = s - jnp.max(s, axis=-1, keepdims=True)
            p = jnp.exp2(s).astype(BF16)
            ox = jnp.dot(p, vx_ref[hd], preferred_element_type=F32)
            o_ref[rows, hd * V_DIM:(hd + 1) * V_DIM] = (ox[:, :V_DIM] / ox[:, V_DIM:]).astype(BF16)


def _attention(q, k, v, batch, seq, row_max):
    n = q.shape[1]
    g = 1 if row_max else ATTN_HEADS
    return pl.pallas_call(
        functools.partial(_attn_kernel, heads=g, row_max=row_max),
        grid=(batch, N_HEADS // g),
        in_specs=[
            pl.BlockSpec((g, seq, HEAD_PAD), lambda b, h: (h, b, 0)),
            pl.BlockSpec((g, seq, HEAD_PAD), lambda b, h: (h, b, 0)),
            pl.BlockSpec((g, seq, V_DIM), lambda b, h: (h, b, 0)),
        ],
        out_specs=pl.BlockSpec((seq, g * V_DIM), lambda b, h: (b, h)),
        out_shape=jax.ShapeDtypeStruct((n, A_WIDTH), BF16),
        scratch_shapes=[pltpu.VMEM((g, seq, 2 * V_DIM), BF16)],
        compiler_params=pltpu.CompilerParams(
            dimension_semantics=("arbitrary", "arbitrary"),
            vmem_limit_bytes=VMEM_LIMIT),
        name="attention_rowmax" if row_max else "attention",
    )(q, k, v)


def _out_mlp_kernel(x_ref, yf_ref, o_ref, ag_ref, wo_ref, mg_ref, w1_ref, w2_ref, out_ref, act_ref):
    ya = _rms(o_ref[...].astype(F32), ag_ref[...]).astype(BF16)
    x1 = (x_ref[...]
          + jnp.dot(yf_ref[...], wo_ref[:F_WIDTH, :], preferred_element_type=F32)
          + jnp.dot(ya, wo_ref[F_WIDTH:, :], preferred_element_type=F32))
    hn = _rms(x1, mg_ref[...]).astype(BF16)
    for c in range(D_FF // FF_CHUNK):
        cols = slice(c * FF_CHUNK, (c + 1) * FF_CHUNK)
        hm = jnp.dot(hn, w1_ref[:, cols], preferred_element_type=F32)
        act_ref[:, cols] = jnp.square(jnp.maximum(hm, 0.0)).astype(BF16)
    out_ref[...] = x1 + jnp.dot(act_ref[...], w2_ref[...], preferred_element_type=F32)


def _out_mlp(x2, yf, o, ag, w_out, mg, w1, w2, layer):
    n = x2.shape[0]
    tm = ROW_TILE
    row = lambda w: pl.BlockSpec((tm, w), lambda i: (i, 0))
    layer_spec = lambda r, c: pl.BlockSpec((None, r, c), lambda i: (layer, 0, 0))
    return pl.pallas_call(
        _out_mlp_kernel,
        grid=(n // tm,),
        in_specs=[
            row(D_MODEL), row(F_WIDTH), row(A_WIDTH), _const_spec((1, A_WIDTH)),
            layer_spec(D_MODEL, D_MODEL), _const_spec((1, D_MODEL)),
            layer_spec(D_MODEL, D_FF), layer_spec(D_FF, D_MODEL),
        ],
        out_specs=row(D_MODEL),
        out_shape=jax.ShapeDtypeStruct((n, D_MODEL), F32),
        scratch_shapes=[pltpu.VMEM((tm, D_FF), BF16)],
        compiler_params=pltpu.CompilerParams(dimension_semantics=("arbitrary",),
                                             vmem_limit_bytes=VMEM_LIMIT),
        name="out_mlp",
    )(x2, yf, o, ag, w_out, mg, w1, w2)


def _dft_matrices(seq):
    lo_n = DFT_SPLIT
    hi_n = seq // lo_n
    half = seq // 2
    n_idx = np.arange(half)[None, :]
    ang_hi = 2.0 * np.pi * ((n_idx * lo_n * np.arange(hi_n)[:, None]) % seq) / seq
    ang_lo = 2.0 * np.pi * ((n_idx * np.arange(lo_n)[:, None]) % seq) / seq
    ch, sh = jnp.asarray(np.cos(ang_hi), F32)[:, None, :], jnp.asarray(np.sin(ang_hi), F32)[:, None, :]
    cl, sl = jnp.asarray(np.cos(ang_lo), F32)[None, :, :], jnp.asarray(np.sin(ang_lo), F32)[None, :, :]
    cmat = (ch * cl - sh * sl).astype(BF16).reshape(seq, half)
    smat_neg = (-(sh * cl + ch * sl)).astype(BF16).reshape(seq, half)
    return cmat, smat_neg


def kernel(x, positions, attn_norm_g, w_in, w_fourier, q_a_g, w_q_up, kv_a_g, w_kv_up, q_norm_g,
           k_norm_g, fourier_out_g, attn_out_g, w_out, mlp_norm_g, w_mlp_in, w_mlp_out):
    batch, seq, _ = x.shape
    depth = w_in.shape[0]
    n = batch * seq

    consecutive = jnp.all(positions == positions[:, :1] + jnp.arange(seq, dtype=positions.dtype)[None, :])
    cos_t, sin_t = lax.cond(consecutive, _rotary_tables_offset, _rotary_tables, positions)
    cmat, smat_neg = _dft_matrices(seq)
    w_ab = _fold_fourier_weights(w_fourier, seq).astype(BF16)

    rope_cols = _rope_tile_cols(0)
    z_cols = np.concatenate([np.arange(F_WIDTH + Q_LORA + KV_LORA),
                             np.where(rope_cols >= 0, rope_cols + F_WIDTH + Q_LORA + KV_LORA, -1)])
    head_cols = np.concatenate([np.arange(QK_NOPE), _rope_tile_cols(QK_NOPE)])
    q_cols = np.concatenate([np.where(head_cols >= 0, head_cols + hd * QK_DIM, -1)
                             for hd in range(N_HEADS)])

    root = math.sqrt(QK_DIM)
    q_scale = math.log2(math.e) / root
    nope = np.arange(HEAD_PAD) < QK_NOPE

    x2 = x.reshape(n, D_MODEL)
    for l in range(depth):
        w_in_l = _gather_cols(w_in[l] * attn_norm_g[l][:, None], z_cols).astype(BF16)
        w_q_l = _gather_cols(w_q_up[l] * q_a_g[l][:, None], q_cols).astype(BF16)
        w_kv_l = (w_kv_up[l] * kv_a_g[l][:, None]).astype(BF16)
        qng = _gather_cols(q_norm_g[l][None, :], head_cols)
        kng = _gather_cols(k_norm_g[l][None, :], head_cols)
        qg = jnp.where(jnp.asarray(nope), qng * kng * (root * root * q_scale), qng * (root * q_scale))
        kg = kng[:, LANES:] * root
        bound = BOUND_MARGIN * jnp.maximum(jnp.max(jnp.abs(qg[:, :LANES])),
                                           jnp.max(jnp.abs(qg[:, LANES:])) * jnp.max(jnp.abs(kg)))
        const_lane = jnp.asarray(np.arange(LANES) == HALF)
        q_const = jnp.concatenate([jnp.zeros((1, LANES), F32), jnp.where(const_lane, -bound, 0.0)[None, :]], axis=1)
        k_const = jnp.where(const_lane, 1.0, 0.0)[None, :].astype(F32)
        xab, q, k, v = _project(x2, w_in_l, w_ab[l], w_q_l, w_kv_l, jnp.concatenate([qg, q_const]),
                                jnp.concatenate([kg, k_const]), cos_t, sin_t)
        yf = _fourier(xab, cmat, smat_neg, fourier_out_g[l][None, :], seq)
        o = lax.cond(bound <= SHIFT_SAFE,
                     lambda qkv: _attention(*qkv, batch, seq, row_max=False),
                     lambda qkv: _attention(*qkv, batch, seq, row_max=True), (q, k, v))
        x2 = _out_mlp(x2, yf, o, attn_out_g[l][None, :], w_out, mlp_norm_g[l][None, :], w_mlp_in, w_mlp_out, l)
    return x2.reshape(batch, seq, D_MODEL)
```

```python
import functools
import math

import numpy as np
import jax
import jax.numpy as jnp
from jax import lax
from jax.experimental import pallas as pl
from jax.experimental.pallas import tpu as pltpu

D_MODEL = 1024
F_GROUPS = 4
F_GROUP_DIM = 64
F_WIDTH = F_GROUPS * F_GROUP_DIM
N_HEADS = 6
Q_LORA = 256
KV_LORA = 256
QK_NOPE = 128
QK_ROPE = 64
V_DIM = 128
QK_DIM = QK_NOPE + QK_ROPE
A_WIDTH = N_HEADS * V_DIM
ROPE_BASE = 10000.0
D_FF = 4 * D_MODEL
EPS = 1e-6

LANES = 128
SUBLANES = 8
HEAD_PAD = 2 * LANES
HALF = QK_ROPE // 2
Z_WIDTH = F_WIDTH + Q_LORA + KV_LORA + LANES
VMEM_LIMIT = 56 * 1024 * 1024

TRIG_TILE = 2048
ROW_TILE = 512
PROJ_TILE = 512
PROJ_ZROWS = 512
PROJ_SUB = 512
FLIP_BLOCK = 256
DFT_SPLIT = 64
Q_TILE = 256
ATTN_HEADS = 3
FF_CHUNK = 1024
SHIFT_SAFE = 60.0
BOUND_MARGIN = 1.02

BF16 = jnp.bfloat16
F32 = jnp.float32


def _rope_tile_cols(start):
    cols = np.full((LANES,), -1, np.int64)
    cols[0:HALF] = start + np.arange(HALF)
    cols[2 * HALF:3 * HALF] = start + HALF + np.arange(HALF)
    return cols


def _gather_cols(w, cols):
    taken = jnp.take(w, jnp.asarray(np.maximum(cols, 0)), axis=-1)
    return jnp.where(jnp.asarray(cols >= 0), taken, jnp.zeros((), w.dtype))


def _rms(x, g):
    return x * lax.rsqrt(jnp.mean(x * x, axis=-1, keepdims=True) + EPS) * g


def _rotate(t, cos_t, sin_t):
    return t * cos_t + pltpu.roll(t, 2 * HALF, axis=1) * sin_t


def _fold_kernel(cc_ref, sc_ref, w_ref, out_ref):
    out_ref[...] = jnp.zeros_like(out_ref)
    for g in range(F_GROUPS):
        w = w_ref[0, g]
        a = jnp.dot(cc_ref[...], w, preferred_element_type=F32, precision=lax.Precision.HIGHEST)
        b = jnp.dot(sc_ref[...], w, preferred_element_type=F32, precision=lax.Precision.HIGHEST)
        lo, hi = g * F_GROUP_DIM, (g + 1) * F_GROUP_DIM
        out_ref[0, lo:hi, lo:hi] = a
        out_ref[0, lo:hi, F_WIDTH + lo:F_WIDTH + hi] = b


def _fold_fourier_weights(w_fourier, seq):
    depth = w_fourier.shape[0]
    c = np.arange(F_GROUP_DIM)
    ang = 2.0 * np.pi * ((c[:, None] * c[None, :]) % F_GROUP_DIM) / F_GROUP_DIM
    ortho = 1.0 / math.sqrt(seq * F_GROUP_DIM)
    cc = jnp.asarray(np.cos(ang) * ortho, F32)
    sc = jnp.asarray(np.sin(ang) * ortho, F32)
    return pl.pallas_call(
        _fold_kernel,
        grid=(depth,),
        in_specs=[
            pl.BlockSpec((F_GROUP_DIM, F_GROUP_DIM), lambda l: (0, 0)),
            pl.BlockSpec((F_GROUP_DIM, F_GROUP_DIM), lambda l: (0, 0)),
            pl.BlockSpec((1, F_GROUPS, F_GROUP_DIM, F_GROUP_DIM), lambda l: (l, 0, 0, 0)),
        ],
        out_specs=pl.BlockSpec((1, F_WIDTH, 2 * F_WIDTH), lambda l: (l, 0, 0)),
        out_shape=jax.ShapeDtypeStruct((depth, F_WIDTH, 2 * F_WIDTH), F32),
        name="fourier_fold",
    )(cc, sc, w_fourier)


def _trig_kernel(ang_ref, cos_ref, sin_ref):
    a = ang_ref[...]
    lane = lax.broadcasted_iota(jnp.int32, a.shape, 1)
    s = jnp.sin(a)
    cos_ref[...] = jnp.where(lane % (2 * HALF) < HALF, jnp.cos(a), 0.0)
    sin_ref[...] = jnp.where(lane < HALF, -s, jnp.where((lane >= 2 * HALF) & (lane < 3 * HALF), s, 0.0))


def _lane_freqs():
    inv_freq = ROPE_BASE ** (-jnp.arange(HALF, dtype=F32) / HALF)
    return jnp.tile(inv_freq, LANES // HALF)[None, :]


def _rotary_tables(positions):
    n = positions.size
    ang = positions.astype(F32).reshape(n, 1) * _lane_freqs()
    tile = min(TRIG_TILE, n)
    spec = pl.BlockSpec((tile, LANES), lambda i: (i, 0))
    return pl.pallas_call(
        _trig_kernel,
        grid=(n // tile,),
        in_specs=[spec],
        out_specs=[spec, spec],
        out_shape=[jax.ShapeDtypeStruct((n, LANES), F32)] * 2,
        name="rotary_trig",
    )(ang)


def _plain_trig_kernel(ang_ref, cos_ref, sin_ref):
    a = ang_ref[...]
    cos_ref[...] = jnp.cos(a)
    sin_ref[...] = jnp.sin(a)


def _offset_combine_kernel(cs_ref, sn_ref, co_ref, so_ref, cos_ref, sin_ref):
    cs, sn = cs_ref[...], sn_ref[...]
    co, so = co_ref[0], so_ref[0]
    c = cs * co - sn * so
    s = sn * co + cs * so
    lane = lax.broadcasted_iota(jnp.int32, c.shape, 1)
    cos_ref[...] = jnp.where(lane % (2 * HALF) < HALF, c, 0.0)
    sin_ref[...] = jnp.where(lane < HALF, -s, jnp.where((lane >= 2 * HALF) & (lane < 3 * HALF), s, 0.0))


def _rotary_tables_offset(positions):
    batch, seq = positions.shape
    freqs = _lane_freqs()
    step_ang = jnp.arange(seq, dtype=F32)[:, None] * freqs
    pad = -batch % SUBLANES
    base_ang = jnp.pad(positions[:, 0].astype(F32), (0, pad))[:, None] * freqs

    def trig(ang):
        return pl.pallas_call(
            _plain_trig_kernel,
            out_shape=[jax.ShapeDtypeStruct(ang.shape, F32)] * 2,
            name="rotary_trig_small",
        )(ang)

    cs, sn = trig(step_ang)
    co, so = trig(base_ang)
    co, so = co[:batch, None, :], so[:batch, None, :]
    table = pl.BlockSpec((seq, LANES), lambda b: (b, 0))
    row = pl.BlockSpec((1, 1, LANES), lambda b: (b, 0, 0))
    return pl.pallas_call(
        _offset_combine_kernel,
        grid=(batch,),
        in_specs=[_const_spec((seq, LANES)), _const_spec((seq, LANES)), row, row],
        out_specs=[table, table],
        out_shape=[jax.ShapeDtypeStruct((batch * seq, LANES), F32)] * 2,
        name="rotary_combine",
    )(cs, sn, co, so)


def _unit_rms(x):
    return x * lax.rsqrt(jnp.mean(x * x, axis=-1, keepdims=True) + EPS)


def _proj_matmul_stage(x_ref, win_ref, wab_ref, wq_ref, wkv_ref, xab_ref, qf_ref, kvf_ref, kpe_ref):
    for i in range(x_ref.shape[0] // PROJ_ZROWS):
        rows = slice(i * PROJ_ZROWS, (i + 1) * PROJ_ZROWS)
        h = _unit_rms(x_ref[rows, :])
        z = jnp.dot(h.astype(BF16), win_ref[...], preferred_element_type=F32)
        xab_ref[rows, :] = jnp.dot(z[:, :F_WIDTH].astype(BF16), wab_ref[...],
                                   preferred_element_type=F32).astype(BF16)
        cqn = _unit_rms(z[:, F_WIDTH:F_WIDTH + Q_LORA]).astype(BF16)
        ckvn = _unit_rms(z[:, F_WIDTH + Q_LORA:F_WIDTH + Q_LORA + KV_LORA]).astype(BF16)
        kpe_ref[rows, :] = z[:, F_WIDTH + Q_LORA + KV_LORA:]
        qf_ref[rows, :] = jnp.dot(cqn, wq_ref[...], preferred_element_type=F32)
        kvf_ref[rows, :] = jnp.dot(ckvn, wkv_ref[...], preferred_element_type=F32)


def _proj_norm_stage(qf_ref, kvf_ref, kpe_ref, qg_ref, kg_ref, cos_ref, sin_ref, q_ref, k_ref, v_ref):
    qg = qg_ref[0:1, :]
    qg_nope, qg_rope = qg[:, :LANES], qg[:, LANES:]
    q_const = qg_ref[1:2, LANES:]
    kg_rope = kg_ref[0:1, :]
    k_const = kg_ref[1:2, :]
    eps_sum = QK_DIM * EPS
    for j in range(qf_ref.shape[0] // PROJ_SUB):
        rows = slice(j * PROJ_SUB, (j + 1) * PROJ_SUB)
        k_pe = kpe_ref[rows, :]
        cos_t = cos_ref[rows, :]
        sin_t = sin_ref[rows, :]
        k_rot = _rotate(k_pe * kg_rope, cos_t, sin_t)
        ss_pe = jnp.sum(k_pe * k_pe, axis=-1, keepdims=True) + eps_sum
        for hd in range(N_HEADS):
            qh = qf_ref[rows, hd * HEAD_PAD:(hd + 1) * HEAD_PAD]
            rq = lax.rsqrt(jnp.sum(qh * qh, axis=-1, keepdims=True) + eps_sum)
            q_ref[hd, rows, :LANES] = (qh[:, :LANES] * rq * qg_nope).astype(BF16)
            q_rot = _rotate(qh[:, LANES:] * rq * qg_rope, cos_t, sin_t)
            q_ref[hd, rows, LANES:] = (q_rot + q_const).astype(BF16)
            kn = kvf_ref[rows, hd * HEAD_PAD:hd * HEAD_PAD + LANES]
            rk = lax.rsqrt(jnp.sum(kn * kn, axis=-1, keepdims=True) + ss_pe)
            k_ref[hd, rows, :LANES] = (kn * rk).astype(BF16)
            k_ref[hd, rows, LANES:] = (k_rot * rk + k_const).astype(BF16)
            v_ref[hd, rows, :] = kvf_ref[rows, hd * HEAD_PAD + LANES:(hd + 1) * HEAD_PAD].astype(BF16)


def _proj_kernel(x_ref, win_ref, wab_ref, wq_ref, wkv_ref, qg_ref, kg_ref, cos_ref, sin_ref,
                 xab_ref, q_ref, k_ref, v_ref, qf0, kvf0, kpe0, qf1, kvf1, kpe1):
    step = pl.program_id(0)

    @pl.when(step == 0)
    def _():
        qf1[...] = jnp.zeros_like(qf1)
        kvf1[...] = jnp.zeros_like(kvf1)
        kpe1[...] = jnp.zeros_like(kpe1)

    def body(fill, drain):
        _proj_matmul_stage(x_ref, win_ref, wab_ref, wq_ref, wkv_ref, xab_ref, *fill)
        _proj_norm_stage(*drain, qg_ref, kg_ref, cos_ref, sin_ref, q_ref, k_ref, v_ref)

    @pl.when(step % 2 == 0)
    def _():
        body((qf0, kvf0, kpe0), (qf1, kvf1, kpe1))

    @pl.when(step % 2 == 1)
    def _():
        body((qf1, kvf1, kpe1), (qf0, kvf0, kpe0))


def _const_spec(shape):
    return pl.BlockSpec(shape, lambda *_: (0,) * len(shape))


def _layer_spec(shape, layer):
    return pl.BlockSpec((None,) + tuple(shape), lambda *_: (layer,) + (0,) * len(shape))


def _project(x2, w_in, w_ab, w_q, w_kv, qg, kg, cos_t, sin_t, layer):
    n = x2.shape[0]
    tm = PROJ_TILE
    nt = n // tm
    fill = lambda w: pl.BlockSpec((tm, w), lambda i: (jnp.minimum(i, nt - 1), 0))
    drain = lambda w: pl.BlockSpec((tm, w), lambda i: (jnp.maximum(i - 1, 0), 0))
    head = lambda w: pl.BlockSpec((N_HEADS, tm, w), lambda i: (0, jnp.maximum(i - 1, 0), 0))
    up_width = N_HEADS * HEAD_PAD
    handoff = [pltpu.VMEM((tm, up_width), F32), pltpu.VMEM((tm, up_width), F32), pltpu.VMEM((tm, LANES), F32)]
    return pl.pallas_call(
        _proj_kernel,
        grid=(nt + 1,),
        in_specs=[
            fill(D_MODEL), _layer_spec((D_MODEL, Z_WIDTH), layer), _layer_spec((F_WIDTH, 2 * F_WIDTH), layer),
            _layer_spec((Q_LORA, up_width), layer), _layer_spec((KV_LORA, up_width), layer),
            _layer_spec((2, HEAD_PAD), layer), _layer_spec((2, LANES), layer), drain(LANES), drain(LANES),
        ],
        out_specs=[fill(2 * F_WIDTH), head(HEAD_PAD), head(HEAD_PAD), head(V_DIM)],
        out_shape=[
            jax.ShapeDtypeStruct((n, 2 * F_WIDTH), BF16),
            jax.ShapeDtypeStruct((N_HEADS, n, HEAD_PAD), BF16),
            jax.ShapeDtypeStruct((N_HEADS, n, HEAD_PAD), BF16),
            jax.ShapeDtypeStruct((N_HEADS, n, V_DIM), BF16),
        ],
        scratch_shapes=handoff + handoff,
        compiler_params=pltpu.CompilerParams(dimension_semantics=("arbitrary",),
                                             vmem_limit_bytes=VMEM_LIMIT),
        name="in_proj",
    )(x2, w_in, w_ab, w_q, w_kv, qg, kg, cos_t, sin_t)


def _fourier_kernel(c_ref, s_ref, flip_ref, lo_ref, hi_ref, g_ref, y_ref):
    half = lo_ref.shape[0]
    nblk = half // FLIP_BLOCK
    flipped = [jnp.dot(flip_ref[...], hi_ref[(nblk - 1 - i) * FLIP_BLOCK:(nblk - i) * FLIP_BLOCK, :],
                       preferred_element_type=F32) for i in range(nblk)]
    rev = pltpu.roll(jnp.concatenate(flipped, axis=0), 1, axis=0)
    lo = lo_ref[...].astype(F32)
    up = (lo[:, :F_WIDTH] + rev[:, :F_WIDTH]).astype(BF16)
    um = (lo[:, F_WIDTH:] - rev[:, F_WIDTH:]).astype(BF16)
    y = (jnp.dot(c_ref[...], up, preferred_element_type=F32)
         + jnp.dot(s_ref[...], um, preferred_element_type=F32))
    mid = rev[0:1, :F_WIDTH]
    odd = (lax.broadcasted_iota(jnp.int32, y.shape, 0) & 1) == 1
    y = y - jnp.where(odd, 2.0 * mid, 0.0)
    y_ref[...] = _rms(y, g_ref[...]).astype(BF16)


def _fourier(xab, cmat, smat_neg, g, seq, layer):
    n = xab.shape[0]
    half = seq // 2
    flip = jnp.asarray(np.eye(FLIP_BLOCK)[::-1], BF16)
    return pl.pallas_call(
        _fourier_kernel,
        grid=(n // seq,),
        in_specs=[
            _const_spec((seq, half)), _const_spec((seq, half)), _const_spec((FLIP_BLOCK, FLIP_BLOCK)),
            pl.BlockSpec((half, 2 * F_WIDTH), lambda b: (2 * b, 0)),
            pl.BlockSpec((half, 2 * F_WIDTH), lambda b: (2 * b + 1, 0)), _layer_spec((1, F_WIDTH), layer),
        ],
        out_specs=pl.BlockSpec((seq, F_WIDTH), lambda b: (b, 0)),
        out_shape=jax.ShapeDtypeStruct((n, F_WIDTH), BF16),
        compiler_params=pltpu.CompilerParams(dimension_semantics=("arbitrary",),
                                             vmem_limit_bytes=VMEM_LIMIT),
        name="seq_dft",
    )(cmat, smat_neg, flip, xab, xab, g)


def _attn_kernel(q_ref, k_ref, v_ref, o_ref, vx_ref, *, heads, row_max):
    seq = q_ref.shape[1]
    for hd in range(heads):
        vx_ref[hd, :, :V_DIM] = v_ref[hd]
        vx_ref[hd, :, V_DIM:] = jnp.ones((seq, V_DIM), BF16)
    for hd in range(heads):
        for j in range(seq // Q_TILE):
            rows = slice(j * Q_TILE, (j + 1) * Q_TILE)
            s = lax.dot_general(q_ref[hd, rows, :], k_ref[hd], (((1,), (1,)), ((), ())),
                                preferred_element_type=F32)
            if row_max:
                s = s - jnp.max(s, axis=-1, keepdims=True)
            p = jnp.exp2(s).astype(BF16)
            ox = jnp.dot(p, vx_ref[hd], preferred_element_type=F32)
            o_ref[rows, hd * V_DIM:(hd + 1) * V_DIM] = (ox[:, :V_DIM] / ox[:, V_DIM:]).astype(BF16)


def _attention(q, k, v, batch, seq, row_max):
    n = q.shape[1]
    g = 1 if row_max else ATTN_HEADS
    return pl.pallas_call(
        functools.partial(_attn_kernel, heads=g, row_max=row_max),
        grid=(batch, N_HEADS // g),
        in_specs=[
            pl.BlockSpec((g, seq, HEAD_PAD), lambda b, h: (h, b, 0)),
            pl.BlockSpec((g, seq, HEAD_PAD), lambda b, h: (h, b, 0)),
            pl.BlockSpec((g, seq, V_DIM), lambda b, h: (h, b, 0)),
        ],
        out_specs=pl.BlockSpec((seq, g * V_DIM), lambda b, h: (b, h)),
        out_shape=jax.ShapeDtypeStruct((n, A_WIDTH), BF16),
        scratch_shapes=[pltpu.VMEM((g, seq, 2 * V_DIM), BF16)],
        compiler_params=pltpu.CompilerParams(
            dimension_semantics=("arbitrary", "arbitrary"),
            vmem_limit_bytes=VMEM_LIMIT),
        name="attention_rowmax" if row_max else "attention",
    )(q, k, v)


def _out_mlp_kernel(x_ref, yf_ref, o_ref, ag_ref, wo_ref, mg_ref, w1_ref, w2_ref, out_ref, act_ref):
    ya = _rms(o_ref[...].astype(F32), ag_ref[...]).astype(BF16)
    x1 = (x_ref[...]
          + jnp.dot(yf_ref[...], wo_ref[:F_WIDTH, :], preferred_element_type=F32)
          + jnp.dot(ya, wo_ref[F_WIDTH:, :], preferred_element_type=F32))
    hn = _rms(x1, mg_ref[...]).astype(BF16)
    for c in range(D_FF // FF_CHUNK):
        cols = slice(c * FF_CHUNK, (c + 1) * FF_CHUNK)
        hm = jnp.dot(hn, w1_ref[:, cols], preferred_element_type=F32)
        act_ref[:, cols] = jnp.square(jnp.maximum(hm, 0.0)).astype(BF16)
    out_ref[...] = x1 + jnp.dot(act_ref[...], w2_ref[...], preferred_element_type=F32)


def _out_mlp(x2, yf, o, ag, w_out, mg, w1, w2, layer):
    n = x2.shape[0]
    tm = ROW_TILE
    row = lambda w: pl.BlockSpec((tm, w), lambda i: (i, 0))
    return pl.pallas_call(
        _out_mlp_kernel,
        grid=(n // tm,),
        in_specs=[
            row(D_MODEL), row(F_WIDTH), row(A_WIDTH), _layer_spec((1, A_WIDTH), layer),
            _layer_spec((D_MODEL, D_MODEL), layer), _layer_spec((1, D_MODEL), layer),
            _layer_spec((D_MODEL, D_FF), layer), _layer_spec((D_FF, D_MODEL), layer),
        ],
        out_specs=row(D_MODEL),
        out_shape=jax.ShapeDtypeStruct((n, D_MODEL), F32),
        scratch_shapes=[pltpu.VMEM((tm, D_FF), BF16)],
        compiler_params=pltpu.CompilerParams(dimension_semantics=("arbitrary",),
                                             vmem_limit_bytes=VMEM_LIMIT),
        name="out_mlp",
    )(x2, yf, o, ag, w_out, mg, w1, w2)


def _dft_matrices(seq):
    lo_n = DFT_SPLIT
    hi_n = seq // lo_n
    half = seq // 2
    n_idx = np.arange(half)[None, :]
    ang_hi = 2.0 * np.pi * ((n_idx * lo_n * np.arange(hi_n)[:, None]) % seq) / seq
    ang_lo = 2.0 * np.pi * ((n_idx * np.arange(lo_n)[:, None]) % seq) / seq
    ch, sh = jnp.asarray(np.cos(ang_hi), F32)[:, None, :], jnp.asarray(np.sin(ang_hi), F32)[:, None, :]
    cl, sl = jnp.asarray(np.cos(ang_lo), F32)[None, :, :], jnp.asarray(np.sin(ang_lo), F32)[None, :, :]
    cmat = (ch * cl - sh * sl).astype(BF16).reshape(seq, half)
    smat_neg = (-(sh * cl + ch * sl)).astype(BF16).reshape(seq, half)
    return cmat, smat_neg


def kernel(x, positions, attn_norm_g, w_in, w_fourier, q_a_g, w_q_up, kv_a_g, w_kv_up, q_norm_g,
           k_norm_g, fourier_out_g, attn_out_g, w_out, mlp_norm_g, w_mlp_in, w_mlp_out):
    batch, seq, _ = x.shape
    depth = w_in.shape[0]
    n = batch * seq

    consecutive = jnp.all(positions == positions[:, :1] + jnp.arange(seq, dtype=positions.dtype)[None, :])
    cos_t, sin_t = lax.cond(consecutive, _rotary_tables_offset, _rotary_tables, positions)
    cmat, smat_neg = _dft_matrices(seq)
    w_ab = _fold_fourier_weights(w_fourier, seq).astype(BF16)

    rope_cols = _rope_tile_cols(0)
    z_cols = np.concatenate([np.arange(F_WIDTH + Q_LORA + KV_LORA),
                             np.where(rope_cols >= 0, rope_cols + F_WIDTH + Q_LORA + KV_LORA, -1)])
    head_cols = np.concatenate([np.arange(QK_NOPE), _rope_tile_cols(QK_NOPE)])
    q_cols = np.concatenate([np.where(head_cols >= 0, head_cols + hd * QK_DIM, -1)
                             for hd in range(N_HEADS)])

    root = math.sqrt(QK_DIM)
    q_scale = math.log2(math.e) / root
    nope = np.arange(HEAD_PAD) < QK_NOPE

    w_in_p = _gather_cols(w_in * attn_norm_g[:, :, None], z_cols).astype(BF16)
    w_q_p = _gather_cols(w_q_up * q_a_g[:, :, None], q_cols).astype(BF16)
    w_kv_p = (w_kv_up * kv_a_g[:, :, None]).astype(BF16)
    qng = _gather_cols(q_norm_g, head_cols)
    kng = _gather_cols(k_norm_g, head_cols)
    qg = jnp.where(jnp.asarray(nope), qng * kng * (root * root * q_scale), qng * (root * q_scale))
    kg = kng[:, LANES:] * root
    bound = BOUND_MARGIN * jnp.maximum(
        jnp.max(jnp.abs(qg[:, :LANES]), axis=1),
        jnp.max(jnp.abs(qg[:, LANES:]), axis=1) * jnp.max(jnp.abs(kg), axis=1))
    const_lane = jnp.asarray(np.arange(LANES) == HALF)
    q_const = jnp.where(jnp.asarray(np.arange(HEAD_PAD) == LANES + HALF), -bound[:, None], 0.0)
    k_const = jnp.broadcast_to(jnp.where(const_lane, 1.0, 0.0).astype(F32), kg.shape)
    qg2 = jnp.stack([qg, q_const], axis=1)
    kg2 = jnp.stack([kg, k_const], axis=1)
    fg, ag, mg = fourier_out_g[:, None, :], attn_out_g[:, None, :], mlp_norm_g[:, None, :]

    x2 = x.reshape(n, D_MODEL)
    for l in range(depth):
        xab, q, k, v = _project(x2, w_in_p, w_ab, w_q_p, w_kv_p, qg2, kg2, cos_t, sin_t, l)
        yf = _fourier(xab, cmat, smat_neg, fg, seq, l)
        o = lax.cond(bound[l] <= SHIFT_SAFE,
                     lambda qkv: _attention(*qkv, batch, seq, row_max=False),
                     lambda qkv: _attention(*qkv, batch, seq, row_max=True), (q, k, v))
        x2 = _out_mlp(x2, yf, o, ag, w_out, mg, w_mlp_in, w_mlp_out, l)
    return x2.reshape(batch, seq, D_MODEL)
```

```python
import functools
import math

import numpy as np
import jax
import jax.numpy as jnp
from jax import lax
from jax.experimental import pallas as pl
from jax.experimental.pallas import tpu as pltpu

D_MODEL = 1024
F_GROUPS = 4
F_GROUP_DIM = 64
F_WIDTH = F_GROUPS * F_GROUP_DIM
N_HEADS = 6
Q_LORA = 256
KV_LORA = 256
QK_NOPE = 128
QK_ROPE = 64
V_DIM = 128
QK_DIM = QK_NOPE + QK_ROPE
A_WIDTH = N_HEADS * V_DIM
ROPE_BASE = 10000.0
D_FF = 4 * D_MODEL
EPS = 1e-6

LANES = 128
SUBLANES = 8
HEAD_PAD = 2 * LANES
HALF = QK_ROPE // 2
Z_WIDTH = F_WIDTH + Q_LORA + KV_LORA + LANES
VMEM_LIMIT = 56 * 1024 * 1024

TRIG_TILE = 2048
ROW_TILE = 512
PROJ_TILE = 512
PROJ_ZROWS = 512
PROJ_SUB = 512
FLIP_BLOCK = 256
DFT_SPLIT = 64
Q_TILE = 256
ATTN_HEADS = 3
FF_CHUNK = 1024
SHIFT_SAFE = 60.0
BOUND_MARGIN = 1.02

BF16 = jnp.bfloat16
F32 = jnp.float32


def _rope_tile(w, start):
    zeros = jnp.zeros(w.shape[:-1] + (HALF,), w.dtype)
    return jnp.concatenate([w[..., start:start + HALF], zeros, w[..., start + HALF:start + 2 * HALF], zeros],
                           axis=-1)


def _head_layout(w, heads):
    parts = []
    for hd in range(heads):
        base = hd * QK_DIM
        parts += [w[..., base:base + QK_NOPE], _rope_tile(w, base + QK_NOPE)]
    return jnp.concatenate(parts, axis=-1)


def _rms(x, g):
    return x * lax.rsqrt(jnp.mean(x * x, axis=-1, keepdims=True) + EPS) * g


def _rotate(t, cos_t, sin_t):
    return t * cos_t + pltpu.roll(t, 2 * HALF, axis=1) * sin_t


def _fold_kernel(cc_ref, sc_ref, w_ref, out_ref):
    out_ref[...] = jnp.zeros_like(out_ref)
    for g in range(F_GROUPS):
        w = w_ref[0, g]
        a = jnp.dot(cc_ref[...], w, preferred_element_type=F32, precision=lax.Precision.HIGHEST)
        b = jnp.dot(sc_ref[...], w, preferred_element_type=F32, precision=lax.Precision.HIGHEST)
        lo, hi = g * F_GROUP_DIM, (g + 1) * F_GROUP_DIM
        out_ref[0, lo:hi, lo:hi] = a
        out_ref[0, lo:hi, F_WIDTH + lo:F_WIDTH + hi] = b


def _fold_fourier_weights(w_fourier, seq):
    depth = w_fourier.shape[0]
    c = np.arange(F_GROUP_DIM)
    ang = 2.0 * np.pi * ((c[:, None] * c[None, :]) % F_GROUP_DIM) / F_GROUP_DIM
    ortho = 1.0 / math.sqrt(seq * F_GROUP_DIM)
    cc = jnp.asarray(np.cos(ang) * ortho, F32)
    sc = jnp.asarray(np.sin(ang) * ortho, F32)
    return pl.pallas_call(
        _fold_kernel,
        grid=(depth,),
        in_specs=[
            pl.BlockSpec((F_GROUP_DIM, F_GROUP_DIM), lambda l: (0, 0)),
            pl.BlockSpec((F_GROUP_DIM, F_GROUP_DIM), lambda l: (0, 0)),
            pl.BlockSpec((1, F_GROUPS, F_GROUP_DIM, F_GROUP_DIM), lambda l: (l, 0, 0, 0)),
        ],
        out_specs=pl.BlockSpec((1, F_WIDTH, 2 * F_WIDTH), lambda l: (l, 0, 0)),
        out_shape=jax.ShapeDtypeStruct((depth, F_WIDTH, 2 * F_WIDTH), F32),
        name="fourier_fold",
    )(cc, sc, w_fourier)


def _trig_kernel(ang_ref, cos_ref, sin_ref):
    a = ang_ref[...]
    lane = lax.broadcasted_iota(jnp.int32, a.shape, 1)
    s = jnp.sin(a)
    cos_ref[...] = jnp.where(lane % (2 * HALF) < HALF, jnp.cos(a), 0.0)
    sin_ref[...] = jnp.where(lane < HALF, -s, jnp.where((lane >= 2 * HALF) & (lane < 3 * HALF), s, 0.0))


def _lane_freqs():
    inv_freq = ROPE_BASE ** (-jnp.arange(HALF, dtype=F32) / HALF)
    return jnp.tile(inv_freq, LANES // HALF)[None, :]


def _rotary_tables(positions):
    n = positions.size
    ang = positions.astype(F32).reshape(n, 1) * _lane_freqs()
    tile = min(TRIG_TILE, n)
    spec = pl.BlockSpec((tile, LANES), lambda i: (i, 0))
    return pl.pallas_call(
        _trig_kernel,
        grid=(n // tile,),
        in_specs=[spec],
        out_specs=[spec, spec],
        out_shape=[jax.ShapeDtypeStruct((n, LANES), F32)] * 2,
        name="rotary_trig",
    )(ang)


def _plain_trig_kernel(ang_ref, cos_ref, sin_ref):
    a = ang_ref[...]
    cos_ref[...] = jnp.cos(a)
    sin_ref[...] = jnp.sin(a)


def _offset_combine_kernel(cs_ref, sn_ref, co_ref, so_ref, cos_ref, sin_ref):
    cs, sn = cs_ref[...], sn_ref[...]
    co, so = co_ref[0], so_ref[0]
    c = cs * co - sn * so
    s = sn * co + cs * so
    lane = lax.broadcasted_iota(jnp.int32, c.shape, 1)
    cos_ref[...] = jnp.where(lane % (2 * HALF) < HALF, c, 0.0)
    sin_ref[...] = jnp.where(lane < HALF, -s, jnp.where((lane >= 2 * HALF) & (lane < 3 * HALF), s, 0.0))


def _rotary_tables_offset(positions):
    batch, seq = positions.shape
    freqs = _lane_freqs()
    step_ang = jnp.arange(seq, dtype=F32)[:, None] * freqs
    pad = -batch % SUBLANES
    base_ang = jnp.pad(positions[:, 0].astype(F32), (0, pad))[:, None] * freqs

    def trig(ang):
        return pl.pallas_call(
            _plain_trig_kernel,
            out_shape=[jax.ShapeDtypeStruct(ang.shape, F32)] * 2,
            name="rotary_trig_small",
        )(ang)

    cs, sn = trig(step_ang)
    co, so = trig(base_ang)
    co, so = co[:batch, None, :], so[:batch, None, :]
    table = pl.BlockSpec((seq, LANES), lambda b: (b, 0))
    row = pl.BlockSpec((1, 1, LANES), lambda b: (b, 0, 0))
    return pl.pallas_call(
        _offset_combine_kernel,
        grid=(batch,),
        in_specs=[_const_spec((seq, LANES)), _const_spec((seq, LANES)), row, row],
        out_specs=[table, table],
        out_shape=[jax.ShapeDtypeStruct((batch * seq, LANES), F32)] * 2,
        name="rotary_combine",
    )(cs, sn, co, so)


def _unit_rms(x):
    return x * lax.rsqrt(jnp.mean(x * x, axis=-1, keepdims=True) + EPS)


def _proj_matmul_stage(x_ref, win_ref, wab_ref, wq_ref, wkv_ref, xab_ref, qf_ref, kvf_ref, kpe_ref):
    for i in range(x_ref.shape[0] // PROJ_ZROWS):
        rows = slice(i * PROJ_ZROWS, (i + 1) * PROJ_ZROWS)
        h = _unit_rms(x_ref[rows, :])
        z = jnp.dot(h.astype(BF16), win_ref[...], preferred_element_type=F32)
        xab_ref[rows, :] = jnp.dot(z[:, :F_WIDTH].astype(BF16), wab_ref[...],
                                   preferred_element_type=F32).astype(BF16)
        cqn = _unit_rms(z[:, F_WIDTH:F_WIDTH + Q_LORA]).astype(BF16)
        ckvn = _unit_rms(z[:, F_WIDTH + Q_LORA:F_WIDTH + Q_LORA + KV_LORA]).astype(BF16)
        kpe_ref[rows, :] = z[:, F_WIDTH + Q_LORA + KV_LORA:]
        qf_ref[rows, :] = jnp.dot(cqn, wq_ref[...], preferred_element_type=F32)
        kvf_ref[rows, :] = jnp.dot(ckvn, wkv_ref[...], preferred_element_type=F32)


def _proj_norm_stage(qf_ref, kvf_ref, kpe_ref, qg_ref, kg_ref, cos_ref, sin_ref, q_ref, k_ref, v_ref):
    qg = qg_ref[0:1, :]
    qg_nope, qg_rope = qg[:, :LANES], qg[:, LANES:]
    q_const = qg_ref[1:2, LANES:]
    kg_rope = kg_ref[0:1, :]
    k_const = kg_ref[1:2, :]
    eps_sum = QK_DIM * EPS
    for j in range(qf_ref.shape[0] // PROJ_SUB):
        rows = slice(j * PROJ_SUB, (j + 1) * PROJ_SUB)
        k_pe = kpe_ref[rows, :]
        cos_t = cos_ref[rows, :]
        sin_t = sin_ref[rows, :]
        k_rot = _rotate(k_pe * kg_rope, cos_t, sin_t)
        ss_pe = jnp.sum(k_pe * k_pe, axis=-1, keepdims=True) + eps_sum
        for hd in range(N_HEADS):
            qh = qf_ref[rows, hd * HEAD_PAD:(hd + 1) * HEAD_PAD]
            rq = lax.rsqrt(jnp.sum(qh * qh, axis=-1, keepdims=True) + eps_sum)
            q_ref[hd, rows, :LANES] = (qh[:, :LANES] * rq * qg_nope).astype(BF16)
            q_rot = _rotate(qh[:, LANES:] * rq * qg_rope, cos_t, sin_t)
            q_ref[hd, rows, LANES:] = (q_rot + q_const).astype(BF16)
            kn = kvf_ref[rows, hd * HEAD_PAD:hd * HEAD_PAD + LANES]
            rk = lax.rsqrt(jnp.sum(kn * kn, axis=-1, keepdims=True) + ss_pe)
            k_ref[hd, rows, :LANES] = (kn * rk).astype(BF16)
            k_ref[hd, rows, LANES:] = (k_rot * rk + k_const).astype(BF16)
            v_ref[hd, rows, :] = kvf_ref[rows, hd * HEAD_PAD + LANES:(hd + 1) * HEAD_PAD].astype(BF16)


def _proj_kernel(x_ref, win_ref, wab_ref, wq_ref, wkv_ref, qg_ref, kg_ref, cos_ref, sin_ref,
                 xab_ref, q_ref, k_ref, v_ref, qf0, kvf0, kpe0, qf1, kvf1, kpe1):
    step = pl.program_id(0)

    @pl.when(step == 0)
    def _():
        qf1[...] = jnp.zeros_like(qf1)
        kvf1[...] = jnp.zeros_like(kvf1)
        kpe1[...] = jnp.zeros_like(kpe1)

    def body(fill, drain):
        _proj_matmul_stage(x_ref, win_ref, wab_ref, wq_ref, wkv_ref, xab_ref, *fill)
        _proj_norm_stage(*drain, qg_ref, kg_ref, cos_ref, sin_ref, q_ref, k_ref, v_ref)

    @pl.when(step % 2 == 0)
    def _():
        body((qf0, kvf0, kpe0), (qf1, kvf1, kpe1))

    @pl.when(step % 2 == 1)
    def _():
        body((qf1, kvf1, kpe1), (qf0, kvf0, kpe0))


def _const_spec(shape):
    return pl.BlockSpec(shape, lambda *_: (0,) * len(shape))


def _layer_spec(shape, layer):
    return pl.BlockSpec((None,) + tuple(shape), lambda *_: (layer,) + (0,) * len(shape))


def _project(x2, w_in, w_ab, w_q, w_kv, qg, kg, cos_t, sin_t, layer):
    n = x2.shape[0]
    tm = PROJ_TILE
    nt = n // tm
    fill = lambda w: pl.BlockSpec((tm, w), lambda i: (jnp.minimum(i, nt - 1), 0))
    drain = lambda w: pl.BlockSpec((tm, w), lambda i: (jnp.maximum(i - 1, 0), 0))
    head = lambda w: pl.BlockSpec((N_HEADS, tm, w), lambda i: (0, jnp.maximum(i - 1, 0), 0))
    up_width = N_HEADS * HEAD_PAD
    handoff = [pltpu.VMEM((tm, up_width), F32), pltpu.VMEM((tm, up_width), F32), pltpu.VMEM((tm, LANES), F32)]
    return pl.pallas_call(
        _proj_kernel,
        grid=(nt + 1,),
        in_specs=[
            fill(D_MODEL), _layer_spec((D_MODEL, Z_WIDTH), layer), _layer_spec((F_WIDTH, 2 * F_WIDTH), layer),
            _layer_spec((Q_LORA, up_width), layer), _layer_spec((KV_LORA, up_width), layer),
            _layer_spec((2, HEAD_PAD), layer), _layer_spec((2, LANES), layer), drain(LANES), drain(LANES),
        ],
        out_specs=[fill(2 * F_WIDTH), head(HEAD_PAD), head(HEAD_PAD), head(V_DIM)],
        out_shape=[
            jax.ShapeDtypeStruct((n, 2 * F_WIDTH), BF16),
            jax.ShapeDtypeStruct((N_HEADS, n, HEAD_PAD), BF16),
            jax.ShapeDtypeStruct((N_HEADS, n, HEAD_PAD), BF16),
            jax.ShapeDtypeStruct((N_HEADS, n, V_DIM), BF16),
        ],
        scratch_shapes=handoff + handoff,
        compiler_params=pltpu.CompilerParams(dimension_semantics=("arbitrary",),
                                             vmem_limit_bytes=VMEM_LIMIT),
        name="in_proj",
    )(x2, w_in, w_ab, w_q, w_kv, qg, kg, cos_t, sin_t)


def _fourier_kernel(c_ref, s_ref, flip_ref, lo_ref, hi_ref, g_ref, y_ref):
    half = lo_ref.shape[0]
    nblk = half // FLIP_BLOCK
    flipped = [jnp.dot(flip_ref[...], hi_ref[(nblk - 1 - i) * FLIP_BLOCK:(nblk - i) * FLIP_BLOCK, :],
                       preferred_element_type=F32) for i in range(nblk)]
    rev = pltpu.roll(jnp.concatenate(flipped, axis=0), 1, axis=0)
    lo = lo_ref[...].astype(F32)
    up = (lo[:, :F_WIDTH] + rev[:, :F_WIDTH]).astype(BF16)
    um = (lo[:, F_WIDTH:] - rev[:, F_WIDTH:]).astype(BF16)
    y = (jnp.dot(c_ref[...], up, preferred_element_type=F32)
         + jnp.dot(s_ref[...], um, preferred_element_type=F32))
    mid = rev[0:1, :F_WIDTH]
    odd = (lax.broadcasted_iota(jnp.int32, y.shape, 0) & 1) == 1
    y = y - jnp.where(odd, 2.0 * mid, 0.0)
    y_ref[...] = _rms(y, g_ref[...]).astype(BF16)


def _fourier(xab, cmat, smat_neg, g, seq, layer):
    n = xab.shape[0]
    half = seq // 2
    flip = jnp.asarray(np.eye(FLIP_BLOCK)[::-1], BF16)
    return pl.pallas_call(
        _fourier_kernel,
        grid=(n // seq,),
        in_specs=[
            _const_spec((seq, half)), _const_spec((seq, half)), _const_spec((FLIP_BLOCK, FLIP_BLOCK)),
            pl.BlockSpec((half, 2 * F_WIDTH), lambda b: (2 * b, 0)),
            pl.BlockSpec((half, 2 * F_WIDTH), lambda b: (2 * b + 1, 0)), _layer_spec((1, F_WIDTH), layer),
        ],
        out_specs=pl.BlockSpec((seq, F_WIDTH), lambda b: (b, 0)),
        out_shape=jax.ShapeDtypeStruct((n, F_WIDTH), BF16),
        compiler_params=pltpu.CompilerParams(dimension_semantics=("arbitrary",),
                                             vmem_limit_bytes=VMEM_LIMIT),
        name="seq_dft",
    )(cmat, smat_neg, flip, xab, xab, g)


def _attn_kernel(q_ref, k_ref, v_ref, o_ref, vx_ref, *, heads, row_max):
    seq = q_ref.shape[1]
    for hd in range(heads):
        vx_ref[hd, :, :V_DIM] = v_ref[hd]
        vx_ref[hd, :, V_DIM:] = jnp.ones((seq, V_DIM), BF16)
    for hd in range(heads):
        for j in range(seq // Q_TILE):
            rows = slice(j * Q_TILE, (j + 1) * Q_TILE)
            s = lax.dot_general(q_ref[hd, rows, :], k_ref[hd], (((1,), (1,)), ((), ())),
                                preferred_element_type=F32)
            if row_max:
                s = s - jnp.max(s, axis=-1, keepdims=True)
            p = jnp.exp2(s).astype(BF16)
            ox = jnp.dot(p, vx_ref[hd], preferred_element_type=F32)
            o_ref[rows, hd * V_DIM:(hd + 1) * V_DIM] = (ox[:, :V_DIM] / ox[:, V_DIM:]).astype(BF16)


def _attention(q, k, v, batch, seq, row_max):
    n = q.shape[1]
    g = 1 if row_max else ATTN_HEADS
    return pl.pallas_call(
        functools.partial(_attn_kernel, heads=g, row_max=row_max),
        grid=(batch, N_HEADS // g),
        in_specs=[
            pl.BlockSpec((g, seq, HEAD_PAD), lambda b, h: (h, b, 0)),
            pl.BlockSpec((g, seq, HEAD_PAD), lambda b, h: (h, b, 0)),
            pl.BlockSpec((g, seq, V_DIM), lambda b, h: (h, b, 0)),
        ],
        out_specs=pl.BlockSpec((seq, g * V_DIM), lambda b, h: (b, h)),
        out_shape=jax.ShapeDtypeStruct((n, A_WIDTH), BF16),
        scratch_shapes=[pltpu.VMEM((g, seq, 2 * V_DIM), BF16)],
        compiler_params=pltpu.CompilerParams(
            dimension_semantics=("arbitrary", "arbitrary"),
            vmem_limit_bytes=VMEM_LIMIT),
        name="attention_rowmax" if row_max else "attention",
    )(q, k, v)


def _out_mlp_kernel(x_ref, yf_ref, o_ref, ag_ref, wo_ref, mg_ref, w1_ref, w2_ref, out_ref, act_ref):
    ya = _rms(o_ref[...].astype(F32), ag_ref[...]).astype(BF16)
    x1 = (x_ref[...]
          + jnp.dot(yf_ref[...], wo_ref[:F_WIDTH, :], preferred_element_type=F32)
          + jnp.dot(ya, wo_ref[F_WIDTH:, :], preferred_element_type=F32))
    hn = _rms(x1, mg_ref[...]).astype(BF16)
    for c in range(D_FF // FF_CHUNK):
        cols = slice(c * FF_CHUNK, (c + 1) * FF_CHUNK)
        hm = jnp.dot(hn, w1_ref[:, cols], preferred_element_type=F32)
        act_ref[:, cols] = jnp.square(jnp.maximum(hm, 0.0)).astype(BF16)
    out_ref[...] = x1 + jnp.dot(act_ref[...], w2_ref[...], preferred_element_type=F32)


def _out_mlp(x2, yf, o, ag, w_out, mg, w1, w2, layer):
    n = x2.shape[0]
    tm = ROW_TILE
    row = lambda w: pl.BlockSpec((tm, w), lambda i: (i, 0))
    return pl.pallas_call(
        _out_mlp_kernel,
        grid=(n // tm,),
        in_specs=[
            row(D_MODEL), row(F_WIDTH), row(A_WIDTH), _layer_spec((1, A_WIDTH), layer),
            _layer_spec((D_MODEL, D_MODEL), layer), _layer_spec((1, D_MODEL), layer),
            _layer_spec((D_MODEL, D_FF), layer), _layer_spec((D_FF, D_MODEL), layer),
        ],
        out_specs=row(D_MODEL),
        out_shape=jax.ShapeDtypeStruct((n, D_MODEL), F32),
        scratch_shapes=[pltpu.VMEM((tm, D_FF), BF16)],
        compiler_params=pltpu.CompilerParams(dimension_semantics=("arbitrary",),
                                             vmem_limit_bytes=VMEM_LIMIT),
        name="out_mlp",
    )(x2, yf, o, ag, w_out, mg, w1, w2)


def _dft_matrices(seq):
    lo_n = DFT_SPLIT
    hi_n = seq // lo_n
    half = seq // 2
    n_idx = np.arange(half)[None, :]
    ang_hi = 2.0 * np.pi * ((n_idx * lo_n * np.arange(hi_n)[:, None]) % seq) / seq
    ang_lo = 2.0 * np.pi * ((n_idx * np.arange(lo_n)[:, None]) % seq) / seq
    ch, sh = jnp.asarray(np.cos(ang_hi), F32)[:, None, :], jnp.asarray(np.sin(ang_hi), F32)[:, None, :]
    cl, sl = jnp.asarray(np.cos(ang_lo), F32)[None, :, :], jnp.asarray(np.sin(ang_lo), F32)[None, :, :]
    cmat = (ch * cl - sh * sl).astype(BF16).reshape(seq, half)
    smat_neg = (-(sh * cl + ch * sl)).astype(BF16).reshape(seq, half)
    return cmat, smat_neg


def kernel(x, positions, attn_norm_g, w_in, w_fourier, q_a_g, w_q_up, kv_a_g, w_kv_up, q_norm_g,
           k_norm_g, fourier_out_g, attn_out_g, w_out, mlp_norm_g, w_mlp_in, w_mlp_out):
    batch, seq, _ = x.shape
    depth = w_in.shape[0]
    n = batch * seq

    consecutive = jnp.all(positions == positions[:, :1] + jnp.arange(seq, dtype=positions.dtype)[None, :])
    cos_t, sin_t = lax.cond(consecutive, _rotary_tables_offset, _rotary_tables, positions)
    cmat, smat_neg = _dft_matrices(seq)
    w_ab = _fold_fourier_weights(w_fourier, seq).astype(BF16)

    root = math.sqrt(QK_DIM)
    q_scale = math.log2(math.e) / root
    nope = np.arange(HEAD_PAD) < QK_NOPE

    latent = F_WIDTH + Q_LORA + KV_LORA
    w_in_g = w_in * attn_norm_g[:, :, None]
    w_in_p = jnp.concatenate([w_in_g[..., :latent], _rope_tile(w_in_g, latent)], axis=-1).astype(BF16)
    w_q_p = _head_layout(w_q_up * q_a_g[:, :, None], N_HEADS).astype(BF16)
    w_kv_p = (w_kv_up * kv_a_g[:, :, None]).astype(BF16)
    qng = _head_layout(q_norm_g, 1)
    kng = _head_layout(k_norm_g, 1)
    qg = jnp.where(jnp.asarray(nope), qng * kng * (root * root * q_scale), qng * (root * q_scale))
    kg = kng[:, LANES:] * root
    bound = BOUND_MARGIN * jnp.maximum(
        jnp.max(jnp.abs(qg[:, :LANES]), axis=1),
        jnp.max(jnp.abs(qg[:, LANES:]), axis=1) * jnp.max(jnp.abs(kg), axis=1))
    const_lane = jnp.asarray(np.arange(LANES) == HALF)
    q_const = jnp.where(jnp.asarray(np.arange(HEAD_PAD) == LANES + HALF), -bound[:, None], 0.0)
    k_const = jnp.broadcast_to(jnp.where(const_lane, 1.0, 0.0).astype(F32), kg.shape)
    qg2 = jnp.stack([qg, q_const], axis=1)
    kg2 = jnp.stack([kg, k_const], axis=1)
    fg, ag, mg = fourier_out_g[:, None, :], attn_out_g[:, None, :], mlp_norm_g[:, None, :]

    x2 = x.reshape(n, D_MODEL)
    for l in range(depth):
        xab, q, k, v = _project(x2, w_in_p, w_ab, w_q_p, w_kv_p, qg2, kg2, cos_t, sin_t, l)
        yf = _fourier(xab, cmat, smat_neg, fg, seq, l)
        o = lax.cond(bound[l] <= SHIFT_SAFE,
                     lambda qkv: _attention(*qkv, batch, seq, row_max=False),
                     lambda qkv: _attention(*qkv, batch, seq, row_max=True), (q, k, v))
        x2 = _out_mlp(x2, yf, o, ag, w_out, mg, w_mlp_in, w_mlp_out, l)
    return x2.reshape(batch, seq, D_MODEL)
```

```python
import functools
import math

import numpy as np
import jax
import jax.numpy as jnp
from jax import lax
from jax.experimental import pallas as pl
from jax.experimental.pallas import tpu as pltpu

D_MODEL = 1024
F_GROUPS = 4
F_GROUP_DIM = 64
F_WIDTH = F_GROUPS * F_GROUP_DIM
N_HEADS = 6
Q_LORA = 256
KV_LORA = 256
QK_NOPE = 128
QK_ROPE = 64
V_DIM = 128
QK_DIM = QK_NOPE + QK_ROPE
A_WIDTH = N_HEADS * V_DIM
ROPE_BASE = 10000.0
D_FF = 4 * D_MODEL
EPS = 1e-6

LANES = 128
SUBLANES = 8
HEAD_PAD = 2 * LANES
HALF = QK_ROPE // 2
Z_WIDTH = F_WIDTH + Q_LORA + KV_LORA + LANES
VMEM_LIMIT = 56 * 1024 * 1024

TRIG_TILE = 2048
ROW_TILE = 512
PROJ_TILE = 512
FLIP_BLOCK = 256
DFT_SPLIT = 64
Q_TILE = 256
ATTN_HEADS = 3
FF_CHUNK = 1024
SHIFT_SAFE = 60.0
BOUND_MARGIN = 1.02

BF16 = jnp.bfloat16
F32 = jnp.float32


def _rope_tile(w, start):
    zeros = jnp.zeros(w.shape[:-1] + (HALF,), w.dtype)
    return jnp.concatenate([w[..., start:start + HALF], zeros, w[..., start + HALF:start + 2 * HALF], zeros],
                           axis=-1)


def _head_layout(w, heads):
    parts = []
    for hd in range(heads):
        base = hd * QK_DIM
        parts += [w[..., base:base + QK_NOPE], _rope_tile(w, base + QK_NOPE)]
    return jnp.concatenate(parts, axis=-1)


def _rms(x, g):
    return x * lax.rsqrt(jnp.mean(x * x, axis=-1, keepdims=True) + EPS) * g


def _rotate(t, cos_t, sin_t):
    return t * cos_t + pltpu.roll(t, 2 * HALF, axis=1) * sin_t


def _fold_kernel(cc_ref, sc_ref, w_ref, out_ref):
    out_ref[...] = jnp.zeros_like(out_ref)
    for g in range(F_GROUPS):
        w = w_ref[0, g]
        a = jnp.dot(cc_ref[...], w, preferred_element_type=F32, precision=lax.Precision.HIGHEST)
        b = jnp.dot(sc_ref[...], w, preferred_element_type=F32, precision=lax.Precision.HIGHEST)
        lo, hi = g * F_GROUP_DIM, (g + 1) * F_GROUP_DIM
        out_ref[0, lo:hi, lo:hi] = a
        out_ref[0, lo:hi, F_WIDTH + lo:F_WIDTH + hi] = b


def _fold_fourier_weights(w_fourier, seq):
    depth = w_fourier.shape[0]
    c = np.arange(F_GROUP_DIM)
    ang = 2.0 * np.pi * ((c[:, None] * c[None, :]) % F_GROUP_DIM) / F_GROUP_DIM
    ortho = 1.0 / math.sqrt(seq * F_GROUP_DIM)
    cc = jnp.asarray(np.cos(ang) * ortho, F32)
    sc = jnp.asarray(np.sin(ang) * ortho, F32)
    return pl.pallas_call(
        _fold_kernel,
        grid=(depth,),
        in_specs=[
            pl.BlockSpec((F_GROUP_DIM, F_GROUP_DIM), lambda l: (0, 0)),
            pl.BlockSpec((F_GROUP_DIM, F_GROUP_DIM), lambda l: (0, 0)),
            pl.BlockSpec((1, F_GROUPS, F_GROUP_DIM, F_GROUP_DIM), lambda l: (l, 0, 0, 0)),
        ],
        out_specs=pl.BlockSpec((1, F_WIDTH, 2 * F_WIDTH), lambda l: (l, 0, 0)),
        out_shape=jax.ShapeDtypeStruct((depth, F_WIDTH, 2 * F_WIDTH), F32),
        name="fourier_fold",
    )(cc, sc, w_fourier)


def _trig_kernel(ang_ref, cos_ref, sin_ref):
    a = ang_ref[...]
    lane = lax.broadcasted_iota(jnp.int32, a.shape, 1)
    s = jnp.sin(a)
    cos_ref[...] = jnp.where(lane % (2 * HALF) < HALF, jnp.cos(a), 0.0)
    sin_ref[...] = jnp.where(lane < HALF, -s, jnp.where((lane >= 2 * HALF) & (lane < 3 * HALF), s, 0.0))


def _lane_freqs():
    inv_freq = ROPE_BASE ** (-jnp.arange(HALF, dtype=F32) / HALF)
    return jnp.tile(inv_freq, LANES // HALF)[None, :]


def _rotary_tables(positions):
    n = positions.size
    ang = positions.astype(F32).reshape(n, 1) * _lane_freqs()
    tile = min(TRIG_TILE, n)
    spec = pl.BlockSpec((tile, LANES), lambda i: (i, 0))
    return pl.pallas_call(
        _trig_kernel,
        grid=(n // tile,),
        in_specs=[spec],
        out_specs=[spec, spec],
        out_shape=[jax.ShapeDtypeStruct((n, LANES), F32)] * 2,
        name="rotary_trig",
    )(ang)


def _plain_trig_kernel(ang_ref, cos_ref, sin_ref):
    a = ang_ref[...]
    cos_ref[...] = jnp.cos(a)
    sin_ref[...] = jnp.sin(a)


def _offset_combine_kernel(cs_ref, sn_ref, co_ref, so_ref, cos_ref, sin_ref):
    cs, sn = cs_ref[...], sn_ref[...]
    co, so = co_ref[0], so_ref[0]
    c = cs * co - sn * so
    s = sn * co + cs * so
    lane = lax.broadcasted_iota(jnp.int32, c.shape, 1)
    cos_ref[...] = jnp.where(lane % (2 * HALF) < HALF, c, 0.0)
    sin_ref[...] = jnp.where(lane < HALF, -s, jnp.where((lane >= 2 * HALF) & (lane < 3 * HALF), s, 0.0))


def _rotary_tables_offset(positions):
    batch, seq = positions.shape
    freqs = _lane_freqs()
    step_ang = jnp.arange(seq, dtype=F32)[:, None] * freqs
    pad = -batch % SUBLANES
    base_ang = jnp.pad(positions[:, 0].astype(F32), (0, pad))[:, None] * freqs

    def trig(ang):
        return pl.pallas_call(
            _plain_trig_kernel,
            out_shape=[jax.ShapeDtypeStruct(ang.shape, F32)] * 2,
            name="rotary_trig_small",
        )(ang)

    cs, sn = trig(step_ang)
    co, so = trig(base_ang)
    co, so = co[:batch, None, :], so[:batch, None, :]
    table = pl.BlockSpec((seq, LANES), lambda b: (b, 0))
    row = pl.BlockSpec((1, 1, LANES), lambda b: (b, 0, 0))
    return pl.pallas_call(
        _offset_combine_kernel,
        grid=(batch,),
        in_specs=[_const_spec((seq, LANES)), _const_spec((seq, LANES)), row, row],
        out_specs=[table, table],
        out_shape=[jax.ShapeDtypeStruct((batch * seq, LANES), F32)] * 2,
        name="rotary_combine",
    )(cs, sn, co, so)


def _unit_rms(x):
    return x * lax.rsqrt(jnp.mean(x * x, axis=-1, keepdims=True) + EPS)


def _proj_matmul_stage(x_ref, win_ref, wab_ref, wq_ref, wkv_ref, xab_ref, qf_ref, kvf_ref, kpe_ref):
    h = _unit_rms(x_ref[...])
    z = jnp.dot(h.astype(BF16), win_ref[...], preferred_element_type=F32)
    xab_ref[...] = jnp.dot(z[:, :F_WIDTH].astype(BF16), wab_ref[...],
                           preferred_element_type=F32).astype(BF16)
    cqn = _unit_rms(z[:, F_WIDTH:F_WIDTH + Q_LORA]).astype(BF16)
    ckvn = _unit_rms(z[:, F_WIDTH + Q_LORA:F_WIDTH + Q_LORA + KV_LORA]).astype(BF16)
    kpe_ref[...] = z[:, F_WIDTH + Q_LORA + KV_LORA:]
    qf_ref[...] = jnp.dot(cqn, wq_ref[...], preferred_element_type=F32)
    kvf_ref[...] = jnp.dot(ckvn, wkv_ref[...], preferred_element_type=F32)


def _proj_norm_stage(qf_ref, kvf_ref, kpe_ref, qg_ref, kg_ref, cos_ref, sin_ref, q_ref, k_ref, v_ref):
    qg = qg_ref[0:1, :]
    qg_nope, qg_rope = qg[:, :LANES], qg[:, LANES:]
    q_const = qg_ref[1:2, LANES:]
    kg_rope = kg_ref[0:1, :]
    k_const = kg_ref[1:2, :]
    eps_sum = QK_DIM * EPS
    k_pe = kpe_ref[...]
    cos_t = cos_ref[...]
    sin_t = sin_ref[...]
    k_rot = _rotate(k_pe * kg_rope, cos_t, sin_t)
    ss_pe = jnp.sum(k_pe * k_pe, axis=-1, keepdims=True) + eps_sum
    for hd in range(N_HEADS):
        qh = qf_ref[:, hd * HEAD_PAD:(hd + 1) * HEAD_PAD]
        rq = lax.rsqrt(jnp.sum(qh * qh, axis=-1, keepdims=True) + eps_sum)
        q_ref[hd, :, :LANES] = (qh[:, :LANES] * rq * qg_nope).astype(BF16)
        q_rot = _rotate(qh[:, LANES:] * rq * qg_rope, cos_t, sin_t)
        q_ref[hd, :, LANES:] = (q_rot + q_const).astype(BF16)
        kn = kvf_ref[:, hd * HEAD_PAD:hd * HEAD_PAD + LANES]
        rk = lax.rsqrt(jnp.sum(kn * kn, axis=-1, keepdims=True) + ss_pe)
        k_ref[hd, :, :LANES] = (kn * rk).astype(BF16)
        k_ref[hd, :, LANES:] = (k_rot * rk + k_const).astype(BF16)
        v_ref[hd] = kvf_ref[:, hd * HEAD_PAD + LANES:(hd + 1) * HEAD_PAD].astype(BF16)


def _proj_kernel(x_ref, win_ref, wab_ref, wq_ref, wkv_ref, qg_ref, kg_ref, cos_ref, sin_ref,
                 xab_ref, q_ref, k_ref, v_ref, qf0, kvf0, kpe0, qf1, kvf1, kpe1):
    step = pl.program_id(0)

    @pl.when(step == 0)
    def _():
        qf1[...] = jnp.zeros_like(qf1)
        kvf1[...] = jnp.zeros_like(kvf1)
        kpe1[...] = jnp.zeros_like(kpe1)

    def body(fill, drain):
        _proj_matmul_stage(x_ref, win_ref, wab_ref, wq_ref, wkv_ref, xab_ref, *fill)
        _proj_norm_stage(*drain, qg_ref, kg_ref, cos_ref, sin_ref, q_ref, k_ref, v_ref)

    @pl.when(step % 2 == 0)
    def _():
        body((qf0, kvf0, kpe0), (qf1, kvf1, kpe1))

    @pl.when(step % 2 == 1)
    def _():
        body((qf1, kvf1, kpe1), (qf0, kvf0, kpe0))


def _const_spec(shape):
    return pl.BlockSpec(shape, lambda *_: (0,) * len(shape))


def _layer_spec(shape, layer):
    return pl.BlockSpec((None,) + tuple(shape), lambda *_: (layer,) + (0,) * len(shape))


def _project(x2, w_in, w_ab, w_q, w_kv, qg, kg, cos_t, sin_t, layer):
    n = x2.shape[0]
    tm = PROJ_TILE
    nt = n // tm
    fill = lambda w: pl.BlockSpec((tm, w), lambda i: (jnp.minimum(i, nt - 1), 0))
    drain = lambda w: pl.BlockSpec((tm, w), lambda i: (jnp.maximum(i - 1, 0), 0))
    head = lambda w: pl.BlockSpec((N_HEADS, tm, w), lambda i: (0, jnp.maximum(i - 1, 0), 0))
    up_width = N_HEADS * HEAD_PAD
    handoff = [pltpu.VMEM((tm, up_width), F32), pltpu.VMEM((tm, up_width), F32), pltpu.VMEM((tm, LANES), F32)]
    return pl.pallas_call(
        _proj_kernel,
        grid=(nt + 1,),
        in_specs=[
            fill(D_MODEL), _layer_spec((D_MODEL, Z_WIDTH), layer), _layer_spec((F_WIDTH, 2 * F_WIDTH), layer),
            _layer_spec((Q_LORA, up_width), layer), _layer_spec((KV_LORA, up_width), layer),
            _layer_spec((2, HEAD_PAD), layer), _layer_spec((2, LANES), layer), drain(LANES), drain(LANES),
        ],
        out_specs=[fill(2 * F_WIDTH), head(HEAD_PAD), head(HEAD_PAD), head(V_DIM)],
        out_shape=[
            jax.ShapeDtypeStruct((n, 2 * F_WIDTH), BF16),
            jax.ShapeDtypeStruct((N_HEADS, n, HEAD_PAD), BF16),
            jax.ShapeDtypeStruct((N_HEADS, n, HEAD_PAD), BF16),
            jax.ShapeDtypeStruct((N_HEADS, n, V_DIM), BF16),
        ],
        scratch_shapes=handoff + handoff,
        compiler_params=pltpu.CompilerParams(dimension_semantics=("arbitrary",),
                                             vmem_limit_bytes=VMEM_LIMIT),
        name="in_proj",
    )(x2, w_in, w_ab, w_q, w_kv, qg, kg, cos_t, sin_t)


def _fourier_kernel(c_ref, s_ref, flip_ref, lo_ref, hi_ref, g_ref, y_ref):
    half = lo_ref.shape[0]
    nblk = half // FLIP_BLOCK
    flipped = [jnp.dot(flip_ref[...], hi_ref[(nblk - 1 - i) * FLIP_BLOCK:(nblk - i) * FLIP_BLOCK, :],
                       preferred_element_type=F32) for i in range(nblk)]
    rev = pltpu.roll(jnp.concatenate(flipped, axis=0), 1, axis=0)
    lo = lo_ref[...].astype(F32)
    up = (lo[:, :F_WIDTH] + rev[:, :F_WIDTH]).astype(BF16)
    um = (lo[:, F_WIDTH:] - rev[:, F_WIDTH:]).astype(BF16)
    y = (jnp.dot(c_ref[...], up, preferred_element_type=F32)
         + jnp.dot(s_ref[...], um, preferred_element_type=F32))
    mid = rev[0:1, :F_WIDTH]
    odd = (lax.broadcasted_iota(jnp.int32, y.shape, 0) & 1) == 1
    y = y - jnp.where(odd, 2.0 * mid, 0.0)
    y_ref[...] = _rms(y, g_ref[...]).astype(BF16)


def _fourier(xab, cmat, smat_neg, g, seq, layer):
    n = xab.shape[0]
    half = seq // 2
    flip = jnp.asarray(np.eye(FLIP_BLOCK)[::-1], BF16)
    return pl.pallas_call(
        _fourier_kernel,
        grid=(n // seq,),
        in_specs=[
            _const_spec((seq, half)), _const_spec((seq, half)), _const_spec((FLIP_BLOCK, FLIP_BLOCK)),
            pl.BlockSpec((half, 2 * F_WIDTH), lambda b: (2 * b, 0)),
            pl.BlockSpec((half, 2 * F_WIDTH), lambda b: (2 * b + 1, 0)), _layer_spec((1, F_WIDTH), layer),
        ],
        out_specs=pl.BlockSpec((seq, F_WIDTH), lambda b: (b, 0)),
        out_shape=jax.ShapeDtypeStruct((n, F_WIDTH), BF16),
        compiler_params=pltpu.CompilerParams(dimension_semantics=("arbitrary",),
                                             vmem_limit_bytes=VMEM_LIMIT),
        name="seq_dft",
    )(cmat, smat_neg, flip, xab, xab, g)


def _attn_kernel(q_ref, k_ref, v_ref, o_ref, vx_ref, *, heads, row_max):
    seq = q_ref.shape[1]
    for hd in range(heads):
        vx_ref[hd, :, :V_DIM] = v_ref[hd]
        vx_ref[hd, :, V_DIM:] = jnp.ones((seq, V_DIM), BF16)
    for hd in range(heads):
        for j in range(seq // Q_TILE):
            rows = slice(j * Q_TILE, (j + 1) * Q_TILE)
            s = lax.dot_general(q_ref[hd, rows, :], k_ref[hd], (((1,), (1,)), ((), ())),
                                preferred_element_type=F32)
            if row_max:
                s = s - jnp.max(s, axis=-1, keepdims=True)
            p = jnp.exp2(s).astype(BF16)
            ox = jnp.dot(p, vx_ref[hd], preferred_element_type=F32)
            o_ref[rows, hd * V_DIM:(hd + 1) * V_DIM] = (ox[:, :V_DIM] / ox[:, V_DIM:]).astype(BF16)


def _attention(q, k, v, batch, seq, row_max):
    n = q.shape[1]
    g = 1 if row_max else ATTN_HEADS
    return pl.pallas_call(
        functools.partial(_attn_kernel, heads=g, row_max=row_max),
        grid=(batch, N_HEADS // g),
        in_specs=[
            pl.BlockSpec((g, seq, HEAD_PAD), lambda b, h: (h, b, 0)),
            pl.BlockSpec((g, seq, HEAD_PAD), lambda b, h: (h, b, 0)),
            pl.BlockSpec((g, seq, V_DIM), lambda b, h: (h, b, 0)),
        ],
        out_specs=pl.BlockSpec((seq, g * V_DIM), lambda b, h: (b, h)),
        out_shape=jax.ShapeDtypeStruct((n, A_WIDTH), BF16),
        scratch_shapes=[pltpu.VMEM((g, seq, 2 * V_DIM), BF16)],
        compiler_params=pltpu.CompilerParams(
            dimension_semantics=("arbitrary", "arbitrary"),
            vmem_limit_bytes=VMEM_LIMIT),
        name="attention_rowmax" if row_max else "attention",
    )(q, k, v)


def _out_mlp_kernel(x_ref, yf_ref, o_ref, ag_ref, wo_ref, mg_ref, w1_ref, w2_ref, out_ref, act_ref):
    ya = _rms(o_ref[...].astype(F32), ag_ref[...]).astype(BF16)
    x1 = (x_ref[...]
          + jnp.dot(yf_ref[...], wo_ref[:F_WIDTH, :], preferred_element_type=F32)
          + jnp.dot(ya, wo_ref[F_WIDTH:, :], preferred_element_type=F32))
    hn = _rms(x1, mg_ref[...]).astype(BF16)
    for c in range(D_FF // FF_CHUNK):
        cols = slice(c * FF_CHUNK, (c + 1) * FF_CHUNK)
        hm = jnp.dot(hn, w1_ref[:, cols], preferred_element_type=F32)
        act_ref[:, cols] = jnp.square(jnp.maximum(hm, 0.0)).astype(BF16)
    out_ref[...] = x1 + jnp.dot(act_ref[...], w2_ref[...], preferred_element_type=F32)


def _out_mlp(x2, yf, o, ag, w_out, mg, w1, w2, layer):
    n = x2.shape[0]
    tm = ROW_TILE
    row = lambda w: pl.BlockSpec((tm, w), lambda i: (i, 0))
    return pl.pallas_call(
        _out_mlp_kernel,
        grid=(n // tm,),
        in_specs=[
            row(D_MODEL), row(F_WIDTH), row(A_WIDTH), _layer_spec((1, A_WIDTH), layer),
            _layer_spec((D_MODEL, D_MODEL), layer), _layer_spec((1, D_MODEL), layer),
            _layer_spec((D_MODEL, D_FF), layer), _layer_spec((D_FF, D_MODEL), layer),
        ],
        out_specs=row(D_MODEL),
        out_shape=jax.ShapeDtypeStruct((n, D_MODEL), F32),
        scratch_shapes=[pltpu.VMEM((tm, D_FF), BF16)],
        compiler_params=pltpu.CompilerParams(dimension_semantics=("arbitrary",),
                                             vmem_limit_bytes=VMEM_LIMIT),
        name="out_mlp",
    )(x2, yf, o, ag, w_out, mg, w1, w2)


def _dft_matrices(seq):
    lo_n = DFT_SPLIT
    hi_n = seq // lo_n
    half = seq // 2
    n_idx = np.arange(half)[None, :]
    ang_hi = 2.0 * np.pi * ((n_idx * lo_n * np.arange(hi_n)[:, None]) % seq) / seq
    ang_lo = 2.0 * np.pi * ((n_idx * np.arange(lo_n)[:, None]) % seq) / seq
    ch, sh = jnp.asarray(np.cos(ang_hi), F32)[:, None, :], jnp.asarray(np.sin(ang_hi), F32)[:, None, :]
    cl, sl = jnp.asarray(np.cos(ang_lo), F32)[None, :, :], jnp.asarray(np.sin(ang_lo), F32)[None, :, :]
    cmat = (ch * cl - sh * sl).astype(BF16).reshape(seq, half)
    smat_neg = (-(sh * cl + ch * sl)).astype(BF16).reshape(seq, half)
    return cmat, smat_neg


def kernel(x, positions, attn_norm_g, w_in, w_fourier, q_a_g, w_q_up, kv_a_g, w_kv_up, q_norm_g,
           k_norm_g, fourier_out_g, attn_out_g, w_out, mlp_norm_g, w_mlp_in, w_mlp_out):
    batch, seq, _ = x.shape
    depth = w_in.shape[0]
    n = batch * seq

    consecutive = jnp.all(positions == positions[:, :1] + jnp.arange(seq, dtype=positions.dtype)[None, :])
    cos_t, sin_t = lax.cond(consecutive, _rotary_tables_offset, _rotary_tables, positions)
    cmat, smat_neg = _dft_matrices(seq)
    w_ab = _fold_fourier_weights(w_fourier, seq).astype(BF16)

    root = math.sqrt(QK_DIM)
    q_scale = math.log2(math.e) / root
    nope = np.arange(HEAD_PAD) < QK_NOPE

    latent = F_WIDTH + Q_LORA + KV_LORA
    w_in_g = w_in * attn_norm_g[:, :, None]
    w_in_p = jnp.concatenate([w_in_g[..., :latent], _rope_tile(w_in_g, latent)], axis=-1).astype(BF16)
    w_q_p = _head_layout(w_q_up * q_a_g[:, :, None], N_HEADS).astype(BF16)
    w_kv_p = (w_kv_up * kv_a_g[:, :, None]).astype(BF16)
    qng = _head_layout(q_norm_g, 1)
    kng = _head_layout(k_norm_g, 1)
    qg = jnp.where(jnp.asarray(nope), qng * kng * (root * root * q_scale), qng * (root * q_scale))
    kg = kng[:, LANES:] * root
    bound = BOUND_MARGIN * jnp.maximum(
        jnp.max(jnp.abs(qg[:, :LANES]), axis=1),
        jnp.max(jnp.abs(qg[:, LANES:]), axis=1) * jnp.max(jnp.abs(kg), axis=1))
    const_lane = jnp.asarray(np.arange(LANES) == HALF)
    q_const = jnp.where(jnp.asarray(np.arange(HEAD_PAD) == LANES + HALF), -bound[:, None], 0.0)
    k_const = jnp.broadcast_to(jnp.where(const_lane, 1.0, 0.0).astype(F32), kg.shape)
    qg2 = jnp.stack([qg, q_const], axis=1)
    kg2 = jnp.stack([kg, k_const], axis=1)
    fg, ag, mg = fourier_out_g[:, None, :], attn_out_g[:, None, :], mlp_norm_g[:, None, :]

    x2 = x.reshape(n, D_MODEL)
    for l in range(depth):
        xab, q, k, v = _project(x2, w_in_p, w_ab, w_q_p, w_kv_p, qg2, kg2, cos_t, sin_t, l)
        yf = _fourier(xab, cmat, smat_neg, fg, seq, l)
        o = lax.cond(bound[l] <= SHIFT_SAFE,
                     lambda qkv: _attention(*qkv, batch, seq, row_max=False),
                     lambda qkv: _attention(*qkv, batch, seq, row_max=True), (q, k, v))
        x2 = _out_mlp(x2, yf, o, ag, w_out, mg, w_mlp_in, w_mlp_out, l)
    return x2.reshape(batch, seq, D_MODEL)
```

```python
import functools
import math

import numpy as np
import jax
import jax.numpy as jnp
from jax import lax
from jax.experimental import pallas as pl
from jax.experimental.pallas import tpu as pltpu

D_MODEL = 1024
F_GROUPS = 4
F_GROUP_DIM = 64
F_WIDTH = F_GROUPS * F_GROUP_DIM
N_HEADS = 6
Q_LORA = 256
KV_LORA = 256
QK_NOPE = 128
QK_ROPE = 64
V_DIM = 128
QK_DIM = QK_NOPE + QK_ROPE
A_WIDTH = N_HEADS * V_DIM
ROPE_BASE = 10000.0
D_FF = 4 * D_MODEL
EPS = 1e-6

LANES = 128
SUBLANES = 8
HEAD_PAD = 2 * LANES
HALF = QK_ROPE // 2
Z_WIDTH = F_WIDTH + Q_LORA + KV_LORA + LANES
VMEM_LIMIT = 56 * 1024 * 1024

TRIG_TILE = 2048
ROW_TILE = 512
PROJ_TILE = 512
FLIP_BLOCK = 256
DFT_SPLIT = 64
Q_TILE = 256
ATTN_HEADS = 2
FF_CHUNK = 1024
SHIFT_SAFE = 60.0
BOUND_MARGIN = 1.02

BF16 = jnp.bfloat16
F32 = jnp.float32


def _rope_tile(w, start):
    zeros = jnp.zeros(w.shape[:-1] + (HALF,), w.dtype)
    return jnp.concatenate([w[..., start:start + HALF], zeros, w[..., start + HALF:start + 2 * HALF], zeros],
                           axis=-1)


def _head_layout(w, heads):
    parts = []
    for hd in range(heads):
        base = hd * QK_DIM
        parts += [w[..., base:base + QK_NOPE], _rope_tile(w, base + QK_NOPE)]
    return jnp.concatenate(parts, axis=-1)


def _rms(x, g):
    return x * lax.rsqrt(jnp.mean(x * x, axis=-1, keepdims=True) + EPS) * g


def _rotate(t, cos_t, sin_t):
    return t * cos_t + pltpu.roll(t, 2 * HALF, axis=1) * sin_t


def _fold_kernel(cc_ref, sc_ref, w_ref, out_ref):
    out_ref[...] = jnp.zeros_like(out_ref)
    for g in range(F_GROUPS):
        w = w_ref[0, g]
        a = jnp.dot(cc_ref[...], w, preferred_element_type=F32, precision=lax.Precision.HIGHEST)
        b = jnp.dot(sc_ref[...], w, preferred_element_type=F32, precision=lax.Precision.HIGHEST)
        lo, hi = g * F_GROUP_DIM, (g + 1) * F_GROUP_DIM
        out_ref[0, lo:hi, lo:hi] = a
        out_ref[0, lo:hi, F_WIDTH + lo:F_WIDTH + hi] = b


def _fold_fourier_weights(w_fourier, seq):
    depth = w_fourier.shape[0]
    c = np.arange(F_GROUP_DIM)
    ang = 2.0 * np.pi * ((c[:, None] * c[None, :]) % F_GROUP_DIM) / F_GROUP_DIM
    ortho = 1.0 / math.sqrt(seq * F_GROUP_DIM)
    cc = jnp.asarray(np.cos(ang) * ortho, F32)
    sc = jnp.asarray(np.sin(ang) * ortho, F32)
    return pl.pallas_call(
        _fold_kernel,
        grid=(depth,),
        in_specs=[
            pl.BlockSpec((F_GROUP_DIM, F_GROUP_DIM), lambda l: (0, 0)),
            pl.BlockSpec((F_GROUP_DIM, F_GROUP_DIM), lambda l: (0, 0)),
            pl.BlockSpec((1, F_GROUPS, F_GROUP_DIM, F_GROUP_DIM), lambda l: (l, 0, 0, 0)),
        ],
        out_specs=pl.BlockSpec((1, F_WIDTH, 2 * F_WIDTH), lambda l: (l, 0, 0)),
        out_shape=jax.ShapeDtypeStruct((depth, F_WIDTH, 2 * F_WIDTH), F32),
        name="fourier_fold",
    )(cc, sc, w_fourier)


def _trig_kernel(ang_ref, cos_ref, sin_ref):
    a = ang_ref[...]
    lane = lax.broadcasted_iota(jnp.int32, a.shape, 1)
    s = jnp.sin(a)
    cos_ref[...] = jnp.where(lane % (2 * HALF) < HALF, jnp.cos(a), 0.0)
    sin_ref[...] = jnp.where(lane < HALF, -s, jnp.where((lane >= 2 * HALF) & (lane < 3 * HALF), s, 0.0))


def _lane_freqs():
    inv_freq = ROPE_BASE ** (-jnp.arange(HALF, dtype=F32) / HALF)
    return jnp.tile(inv_freq, LANES // HALF)[None, :]


def _rotary_tables(positions):
    n = positions.size
    ang = positions.astype(F32).reshape(n, 1) * _lane_freqs()
    tile = min(TRIG_TILE, n)
    spec = pl.BlockSpec((tile, LANES), lambda i: (i, 0))
    return pl.pallas_call(
        _trig_kernel,
        grid=(n // tile,),
        in_specs=[spec],
        out_specs=[spec, spec],
        out_shape=[jax.ShapeDtypeStruct((n, LANES), F32)] * 2,
        name="rotary_trig",
    )(ang)


def _plain_trig_kernel(ang_ref, cos_ref, sin_ref):
    a = ang_ref[...]
    cos_ref[...] = jnp.cos(a)
    sin_ref[...] = jnp.sin(a)


def _offset_combine_kernel(cs_ref, sn_ref, co_ref, so_ref, cos_ref, sin_ref):
    cs, sn = cs_ref[...], sn_ref[...]
    co, so = co_ref[0], so_ref[0]
    c = cs * co - sn * so
    s = sn * co + cs * so
    lane = lax.broadcasted_iota(jnp.int32, c.shape, 1)
    cos_ref[...] = jnp.where(lane % (2 * HALF) < HALF, c, 0.0)
    sin_ref[...] = jnp.where(lane < HALF, -s, jnp.where((lane >= 2 * HALF) & (lane < 3 * HALF), s, 0.0))


def _rotary_tables_offset(positions):
    batch, seq = positions.shape
    freqs = _lane_freqs()
    step_ang = jnp.arange(seq, dtype=F32)[:, None] * freqs
    pad = -batch % SUBLANES
    base_ang = jnp.pad(positions[:, 0].astype(F32), (0, pad))[:, None] * freqs

    def trig(ang):
        return pl.pallas_call(
            _plain_trig_kernel,
            out_shape=[jax.ShapeDtypeStruct(ang.shape, F32)] * 2,
            name="rotary_trig_small",
        )(ang)

    cs, sn = trig(step_ang)
    co, so = trig(base_ang)
    co, so = co[:batch, None, :], so[:batch, None, :]
    table = pl.BlockSpec((seq, LANES), lambda b: (b, 0))
    row = pl.BlockSpec((1, 1, LANES), lambda b: (b, 0, 0))
    return pl.pallas_call(
        _offset_combine_kernel,
        grid=(batch,),
        in_specs=[_const_spec((seq, LANES)), _const_spec((seq, LANES)), row, row],
        out_specs=[table, table],
        out_shape=[jax.ShapeDtypeStruct((batch * seq, LANES), F32)] * 2,
        name="rotary_combine",
    )(cs, sn, co, so)


def _unit_rms(x):
    return x * lax.rsqrt(jnp.mean(x * x, axis=-1, keepdims=True) + EPS)


def _proj_matmul_stage(x_ref, win_ref, wab_ref, wq_ref, wkv_ref, xab_ref, qf_ref, kvf_ref, kpe_ref):
    h = _unit_rms(x_ref[...])
    z = jnp.dot(h.astype(BF16), win_ref[...], preferred_element_type=F32)
    xab_ref[...] = jnp.dot(z[:, :F_WIDTH].astype(BF16), wab_ref[...],
                           preferred_element_type=F32).astype(BF16)
    cqn = _unit_rms(z[:, F_WIDTH:F_WIDTH + Q_LORA]).astype(BF16)
    ckvn = _unit_rms(z[:, F_WIDTH + Q_LORA:F_WIDTH + Q_LORA + KV_LORA]).astype(BF16)
    kpe_ref[...] = z[:, F_WIDTH + Q_LORA + KV_LORA:]
    qf_ref[...] = jnp.dot(cqn, wq_ref[...], preferred_element_type=F32)
    kvf_ref[...] = jnp.dot(ckvn, wkv_ref[...], preferred_element_type=F32)


def _proj_norm_stage(qf_ref, kvf_ref, kpe_ref, qg_ref, kg_ref, cos_ref, sin_ref, q_ref, k_ref, v_ref):
    qg = qg_ref[0:1, :]
    qg_nope, qg_rope = qg[:, :LANES], qg[:, LANES:]
    q_const = qg_ref[1:2, LANES:]
    kg_rope = kg_ref[0:1, :]
    k_const = kg_ref[1:2, :]
    eps_sum = QK_DIM * EPS
    k_pe = kpe_ref[...]
    cos_t = cos_ref[...]
    sin_t = sin_ref[...]
    k_rot = _rotate(k_pe * kg_rope, cos_t, sin_t)
    ss_pe = jnp.sum(k_pe * k_pe, axis=-1, keepdims=True) + eps_sum
    for hd in range(N_HEADS):
        qh = qf_ref[:, hd * HEAD_PAD:(hd + 1) * HEAD_PAD]
        rq = lax.rsqrt(jnp.sum(qh * qh, axis=-1, keepdims=True) + eps_sum)
        q_ref[hd, :, :LANES] = (qh[:, :LANES] * rq * qg_nope).astype(BF16)
        q_rot = _rotate(qh[:, LANES:] * rq * qg_rope, cos_t, sin_t)
        q_ref[hd, :, LANES:] = (q_rot + q_const).astype(BF16)
        kn = kvf_ref[:, hd * HEAD_PAD:hd * HEAD_PAD + LANES]
        rk = lax.rsqrt(jnp.sum(kn * kn, axis=-1, keepdims=True) + ss_pe)
        k_ref[hd, :, :LANES] = (kn * rk).astype(BF16)
        k_ref[hd, :, LANES:] = (k_rot * rk + k_const).astype(BF16)
        v_ref[hd] = kvf_ref[:, hd * HEAD_PAD + LANES:(hd + 1) * HEAD_PAD].astype(BF16)


def _proj_kernel(x_ref, win_ref, wab_ref, wq_ref, wkv_ref, qg_ref, kg_ref, cos_ref, sin_ref,
                 xab_ref, q_ref, k_ref, v_ref, qf0, kvf0, kpe0, qf1, kvf1, kpe1):
    step = pl.program_id(0)

    @pl.when(step == 0)
    def _():
        qf1[...] = jnp.zeros_like(qf1)
        kvf1[...] = jnp.zeros_like(kvf1)
        kpe1[...] = jnp.zeros_like(kpe1)

    def body(fill, drain):
        _proj_matmul_stage(x_ref, win_ref, wab_ref, wq_ref, wkv_ref, xab_ref, *fill)
        _proj_norm_stage(*drain, qg_ref, kg_ref, cos_ref, sin_ref, q_ref, k_ref, v_ref)

    @pl.when(step % 2 == 0)
    def _():
        body((qf0, kvf0, kpe0), (qf1, kvf1, kpe1))

    @pl.when(step % 2 == 1)
    def _():
        body((qf1, kvf1, kpe1), (qf0, kvf0, kpe0))


def _const_spec(shape):
    return pl.BlockSpec(shape, lambda *_: (0,) * len(shape))


def _layer_spec(shape, layer):
    return pl.BlockSpec((None,) + tuple(shape), lambda *_: (layer,) + (0,) * len(shape))


def _project(x2, w_in, w_ab, w_q, w_kv, qg, kg, cos_t, sin_t, layer):
    n = x2.shape[0]
    tm = PROJ_TILE
    nt = n // tm
    fill = lambda w: pl.BlockSpec((tm, w), lambda i: (jnp.minimum(i, nt - 1), 0))
    drain = lambda w: pl.BlockSpec((tm, w), lambda i: (jnp.maximum(i - 1, 0), 0))
    head = lambda w: pl.BlockSpec((N_HEADS, tm, w), lambda i: (0, jnp.maximum(i - 1, 0), 0))
    up_width = N_HEADS * HEAD_PAD
    handoff = [pltpu.VMEM((tm, up_width), F32), pltpu.VMEM((tm, up_width), F32), pltpu.VMEM((tm, LANES), F32)]
    return pl.pallas_call(
        _proj_kernel,
        grid=(nt + 1,),
        in_specs=[
            fill(D_MODEL), _layer_spec((D_MODEL, Z_WIDTH), layer), _layer_spec((F_WIDTH, 2 * F_WIDTH), layer),
            _layer_spec((Q_LORA, up_width), layer), _layer_spec((KV_LORA, up_width), layer),
            _layer_spec((2, HEAD_PAD), layer), _layer_spec((2, LANES), layer), drain(LANES), drain(LANES),
        ],
        out_specs=[fill(2 * F_WIDTH), head(HEAD_PAD), head(HEAD_PAD), head(V_DIM)],
        out_shape=[
            jax.ShapeDtypeStruct((n, 2 * F_WIDTH), BF16),
            jax.ShapeDtypeStruct((N_HEADS, n, HEAD_PAD), BF16),
            jax.ShapeDtypeStruct((N_HEADS, n, HEAD_PAD), BF16),
            jax.ShapeDtypeStruct((N_HEADS, n, V_DIM), BF16),
        ],
        scratch_shapes=handoff + handoff,
        compiler_params=pltpu.CompilerParams(dimension_semantics=("arbitrary",),
                                             vmem_limit_bytes=VMEM_LIMIT),
        name="in_proj",
    )(x2, w_in, w_ab, w_q, w_kv, qg, kg, cos_t, sin_t)


def _fourier_kernel(c_ref, s_ref, flip_ref, lo_ref, hi_ref, g_ref, y_ref):
    half = lo_ref.shape[0]
    nblk = half // FLIP_BLOCK
    flipped = [jnp.dot(flip_ref[...], hi_ref[(nblk - 1 - i) * FLIP_BLOCK:(nblk - i) * FLIP_BLOCK, :],
                       preferred_element_type=F32) for i in range(nblk)]
    rev = pltpu.roll(jnp.concatenate(flipped, axis=0), 1, axis=0)
    lo = lo_ref[...].astype(F32)
    up = (lo[:, :F_WIDTH] + rev[:, :F_WIDTH]).astype(BF16)
    um = (lo[:, F_WIDTH:] - rev[:, F_WIDTH:]).astype(BF16)
    y = (jnp.dot(c_ref[...], up, preferred_element_type=F32)
         + jnp.dot(s_ref[...], um, preferred_element_type=F32))
    mid = rev[0:1, :F_WIDTH]
    odd = (lax.broadcasted_iota(jnp.int32, y.shape, 0) & 1) == 1
    y = y - jnp.where(odd, 2.0 * mid, 0.0)
    y_ref[...] = _rms(y, g_ref[...]).astype(BF16)


def _fourier(xab, cmat, smat_neg, g, seq, layer):
    n = xab.shape[0]
    half = seq // 2
    flip = jnp.asarray(np.eye(FLIP_BLOCK)[::-1], BF16)
    return pl.pallas_call(
        _fourier_kernel,
        grid=(n // seq,),
        in_specs=[
            _const_spec((seq, half)), _const_spec((seq, half)), _const_spec((FLIP_BLOCK, FLIP_BLOCK)),
            pl.BlockSpec((half, 2 * F_WIDTH), lambda b: (2 * b, 0)),
            pl.BlockSpec((half, 2 * F_WIDTH), lambda b: (2 * b + 1, 0)), _layer_spec((1, F_WIDTH), layer),
        ],
        out_specs=pl.BlockSpec((seq, F_WIDTH), lambda b: (b, 0)),
        out_shape=jax.ShapeDtypeStruct((n, F_WIDTH), BF16),
        compiler_params=pltpu.CompilerParams(dimension_semantics=("arbitrary",),
                                             vmem_limit_bytes=VMEM_LIMIT),
        name="seq_dft",
    )(cmat, smat_neg, flip, xab, xab, g)


def _attn_kernel(q_ref, k_ref, v_ref, o_ref, vx_ref, *, heads, row_max):
    seq = q_ref.shape[1]
    for hd in range(heads):
        vx_ref[hd, :, :V_DIM] = v_ref[hd]
        vx_ref[hd, :, V_DIM:] = jnp.ones((seq, V_DIM), BF16)
    for hd in range(heads):
        for j in range(seq // Q_TILE):
            rows = slice(j * Q_TILE, (j + 1) * Q_TILE)
            s = lax.dot_general(q_ref[hd, rows, :], k_ref[hd], (((1,), (1,)), ((), ())),
                                preferred_element_type=F32)
            if row_max:
                s = s - jnp.max(s, axis=-1, keepdims=True)
            p = jnp.exp2(s).astype(BF16)
            ox = jnp.dot(p, vx_ref[hd], preferred_element_type=F32)
            o_ref[rows, hd * V_DIM:(hd + 1) * V_DIM] = (ox[:, :V_DIM] / ox[:, V_DIM:]).astype(BF16)


def _attention(q, k, v, batch, seq, row_max):
    n = q.shape[1]
    g = 1 if row_max else ATTN_HEADS
    return pl.pallas_call(
        functools.partial(_attn_kernel, heads=g, row_max=row_max),
        grid=(batch, N_HEADS // g),
        in_specs=[
            pl.BlockSpec((g, seq, HEAD_PAD), lambda b, h: (h, b, 0)),
            pl.BlockSpec((g, seq, HEAD_PAD), lambda b, h: (h, b, 0)),
            pl.BlockSpec((g, seq, V_DIM), lambda b, h: (h, b, 0)),
        ],
        out_specs=pl.BlockSpec((seq, g * V_DIM), lambda b, h: (b, h)),
        out_shape=jax.ShapeDtypeStruct((n, A_WIDTH), BF16),
        scratch_shapes=[pltpu.VMEM((g, seq, 2 * V_DIM), BF16)],
        compiler_params=pltpu.CompilerParams(
            dimension_semantics=("arbitrary", "arbitrary"),
            vmem_limit_bytes=VMEM_LIMIT),
        name="attention_rowmax" if row_max else "attention",
    )(q, k, v)


def _out_mlp_kernel(x_ref, yf_ref, o_ref, ag_ref, wo_ref, mg_ref, w1_ref, w2_ref, out_ref, act_ref):
    ya = _rms(o_ref[...].astype(F32), ag_ref[...]).astype(BF16)
    x1 = (x_ref[...]
          + jnp.dot(yf_ref[...], wo_ref[:F_WIDTH, :], preferred_element_type=F32)
          + jnp.dot(ya, wo_ref[F_WIDTH:, :], preferred_element_type=F32))
    hn = _rms(x1, mg_ref[...]).astype(BF16)
    for c in range(D_FF // FF_CHUNK):
        cols = slice(c * FF_CHUNK, (c + 1) * FF_CHUNK)
        hm = jnp.dot(hn, w1_ref[:, cols], preferred_element_type=F32)
        act_ref[:, cols] = jnp.square(jnp.maximum(hm, 0.0)).astype(BF16)
    out_ref[...] = x1 + jnp.dot(act_ref[...], w2_ref[...], preferred_element_type=F32)


def _out_mlp(x2, yf, o, ag, w_out, mg, w1, w2, layer):
    n = x2.shape[0]
    tm = ROW_TILE
    row = lambda w: pl.BlockSpec((tm, w), lambda i: (i, 0))
    return pl.pallas_call(
        _out_mlp_kernel,
        grid=(n // tm,),
        in_specs=[
            row(D_MODEL), row(F_WIDTH), row(A_WIDTH), _layer_spec((1, A_WIDTH), layer),
            _layer_spec((D_MODEL, D_MODEL), layer), _layer_spec((1, D_MODEL), layer),
            _layer_spec((D_MODEL, D_FF), layer), _layer_spec((D_FF, D_MODEL), layer),
        ],
        out_specs=row(D_MODEL),
        out_shape=jax.ShapeDtypeStruct((n, D_MODEL), F32),
        scratch_shapes=[pltpu.VMEM((tm, D_FF), BF16)],
        compiler_params=pltpu.CompilerParams(dimension_semantics=("arbitrary",),
                                             vmem_limit_bytes=VMEM_LIMIT),
        name="out_mlp",
    )(x2, yf, o, ag, w_out, mg, w1, w2)


def _dft_matrices(seq):
    lo_n = DFT_SPLIT
    hi_n = seq // lo_n
    half = seq // 2
    n_idx = np.arange(half)[None, :]
    ang_hi = 2.0 * np.pi * ((n_idx * lo_n * np.arange(hi_n)[:, None]) % seq) / seq
    ang_lo = 2.0 * np.pi * ((n_idx * np.arange(lo_n)[:, None]) % seq) / seq
    ch, sh = jnp.asarray(np.cos(ang_hi), F32)[:, None, :], jnp.asarray(np.sin(ang_hi), F32)[:, None, :]
    cl, sl = jnp.asarray(np.cos(ang_lo), F32)[None, :, :], jnp.asarray(np.sin(ang_lo), F32)[None, :, :]
    cmat = (ch * cl - sh * sl).astype(BF16).reshape(seq, half)
    smat_neg = (-(sh * cl + ch * sl)).astype(BF16).reshape(seq, half)
    return cmat, smat_neg


def kernel(x, positions, attn_norm_g, w_in, w_fourier, q_a_g, w_q_up, kv_a_g, w_kv_up, q_norm_g,
           k_norm_g, fourier_out_g, attn_out_g, w_out, mlp_norm_g, w_mlp_in, w_mlp_out):
    batch, seq, _ = x.shape
    depth = w_in.shape[0]
    n = batch * seq

    consecutive = jnp.all(positions == positions[:, :1] + jnp.arange(seq, dtype=positions.dtype)[None, :])
    cos_t, sin_t = lax.cond(consecutive, _rotary_tables_offset, _rotary_tables, positions)
    cmat, smat_neg = _dft_matrices(seq)
    w_ab = _fold_fourier_weights(w_fourier, seq).astype(BF16)

    root = math.sqrt(QK_DIM)
    q_scale = math.log2(math.e) / root
    nope = np.arange(HEAD_PAD) < QK_NOPE

    latent = F_WIDTH + Q_LORA + KV_LORA
    w_in_g = w_in * attn_norm_g[:, :, None]
    w_in_p = jnp.concatenate([w_in_g[..., :latent], _rope_tile(w_in_g, latent)], axis=-1).astype(BF16)
    w_q_p = _head_layout(w_q_up * q_a_g[:, :, None], N_HEADS).astype(BF16)
    w_kv_p = (w_kv_up * kv_a_g[:, :, None]).astype(BF16)
    qng = _head_layout(q_norm_g, 1)
    kng = _head_layout(k_norm_g, 1)
    qg = jnp.where(jnp.asarray(nope), qng * kng * (root * root * q_scale), qng * (root * q_scale))
    kg = kng[:, LANES:] * root
    bound = BOUND_MARGIN * jnp.maximum(
        jnp.max(jnp.abs(qg[:, :LANES]), axis=1),
        jnp.max(jnp.abs(qg[:, LANES:]), axis=1) * jnp.max(jnp.abs(kg), axis=1))
    const_lane = jnp.asarray(np.arange(LANES) == HALF)
    q_const = jnp.where(jnp.asarray(np.arange(HEAD_PAD) == LANES + HALF), -bound[:, None], 0.0)
    k_const = jnp.broadcast_to(jnp.where(const_lane, 1.0, 0.0).astype(F32), kg.shape)
    qg2 = jnp.stack([qg, q_const], axis=1)
    kg2 = jnp.stack([kg, k_const], axis=1)
    fg, ag, mg = fourier_out_g[:, None, :], attn_out_g[:, None, :], mlp_norm_g[:, None, :]

    x2 = x.reshape(n, D_MODEL)
    for l in range(depth):
        xab, q, k, v = _project(x2, w_in_p, w_ab, w_q_p, w_kv_p, qg2, kg2, cos_t, sin_t, l)
        yf = _fourier(xab, cmat, smat_neg, fg, seq, l)
        o = lax.cond(bound[l] <= SHIFT_SAFE,
                     lambda qkv: _attention(*qkv, batch, seq, row_max=False),
                     lambda qkv: _attention(*qkv, batch, seq, row_max=True), (q, k, v))
        x2 = _out_mlp(x2, yf, o, ag, w_out, mg, w_mlp_in, w_mlp_out, l)
    return x2.reshape(batch, seq, D_MODEL)
```

```python
import functools
import math

import numpy as np
import jax
import jax.numpy as jnp
from jax import lax
from jax.experimental import pallas as pl
from jax.experimental.pallas import tpu as pltpu

D_MODEL = 1024
F_GROUPS = 4
F_GROUP_DIM = 64
F_WIDTH = F_GROUPS * F_GROUP_DIM
N_HEADS = 6
Q_LORA = 256
KV_LORA = 256
QK_NOPE = 128
QK_ROPE = 64
V_DIM = 128
QK_DIM = QK_NOPE + QK_ROPE
A_WIDTH = N_HEADS * V_DIM
ROPE_BASE = 10000.0
D_FF = 4 * D_MODEL
EPS = 1e-6

LANES = 128
SUBLANES = 8
HEAD_PAD = 2 * LANES
HALF = QK_ROPE // 2
Z_WIDTH = F_WIDTH + Q_LORA + KV_LORA + LANES
VMEM_LIMIT = 56 * 1024 * 1024

TRIG_TILE = 2048
ROW_TILE = 512
PROJ_TILE = 512
FLIP_BLOCK = 256
DFT_SPLIT = 64
Q_TILE = 256
ATTN_HEADS = 3
FF_CHUNK = 1024
SHIFT_SAFE = 60.0
BOUND_MARGIN = 1.02

BF16 = jnp.bfloat16
F32 = jnp.float32


def _rope_tile(w, start):
    zeros = jnp.zeros(w.shape[:-1] + (HALF,), w.dtype)
    return jnp.concatenate([w[..., start:start + HALF], zeros, w[..., start + HALF:start + 2 * HALF], zeros],
                           axis=-1)


def _head_layout(w, heads):
    parts = []
    for hd in range(heads):
        base = hd * QK_DIM
        parts += [w[..., base:base + QK_NOPE], _rope_tile(w, base + QK_NOPE)]
    return jnp.concatenate(parts, axis=-1)


def _rms(x, g):
    return x * lax.rsqrt(jnp.mean(x * x, axis=-1, keepdims=True) + EPS) * g


def _rotate(t, cos_t, sin_t):
    return t * cos_t + pltpu.roll(t, 2 * HALF, axis=1) * sin_t


def _fold_kernel(cc_ref, sc_ref, w_ref, out_ref):
    out_ref[...] = jnp.zeros_like(out_ref)
    for g in range(F_GROUPS):
        w = w_ref[0, g]
        a = jnp.dot(cc_ref[...], w, preferred_element_type=F32, precision=lax.Precision.HIGHEST)
        b = jnp.dot(sc_ref[...], w, preferred_element_type=F32, precision=lax.Precision.HIGHEST)
        lo, hi = g * F_GROUP_DIM, (g + 1) * F_GROUP_DIM
        out_ref[0, lo:hi, lo:hi] = a
        out_ref[0, lo:hi, F_WIDTH + lo:F_WIDTH + hi] = b


def _fold_fourier_weights(w_fourier, seq):
    depth = w_fourier.shape[0]
    c = np.arange(F_GROUP_DIM)
    ang = 2.0 * np.pi * ((c[:, None] * c[None, :]) % F_GROUP_DIM) / F_GROUP_DIM
    ortho = 1.0 / math.sqrt(seq * F_GROUP_DIM)
    cc = jnp.asarray(np.cos(ang) * ortho, F32)
    sc = jnp.asarray(np.sin(ang) * ortho, F32)
    return pl.pallas_call(
        _fold_kernel,
        grid=(depth,),
        in_specs=[
            pl.BlockSpec((F_GROUP_DIM, F_GROUP_DIM), lambda l: (0, 0)),
            pl.BlockSpec((F_GROUP_DIM, F_GROUP_DIM), lambda l: (0, 0)),
            pl.BlockSpec((1, F_GROUPS, F_GROUP_DIM, F_GROUP_DIM), lambda l: (l, 0, 0, 0)),
        ],
        out_specs=pl.BlockSpec((1, F_WIDTH, 2 * F_WIDTH), lambda l: (l, 0, 0)),
        out_shape=jax.ShapeDtypeStruct((depth, F_WIDTH, 2 * F_WIDTH), F32),
        name="fourier_fold",
    )(cc, sc, w_fourier)


def _lane_freqs():
    inv_freq = ROPE_BASE ** (-jnp.arange(HALF, dtype=F32) / HALF)
    return jnp.tile(inv_freq, LANES // HALF)[None, :]


def _plain_trig_kernel(ang_ref, cos_ref, sin_ref):
    a = ang_ref[...]
    cos_ref[...] = jnp.cos(a)
    sin_ref[...] = jnp.sin(a)


def _trig(ang):
    rows = ang.shape[0]
    tile = TRIG_TILE if rows % TRIG_TILE == 0 else rows
    spec = pl.BlockSpec((tile, LANES), lambda i: (i, 0))
    return pl.pallas_call(
        _plain_trig_kernel,
        grid=(rows // tile,),
        in_specs=[spec],
        out_specs=[spec, spec],
        out_shape=[jax.ShapeDtypeStruct(ang.shape, F32)] * 2,
        name="rotary_trig",
    )(ang)


def _offset_tables(positions):
    batch, seq = positions.shape
    freqs = _lane_freqs()
    cs, sn = _trig(jnp.arange(seq, dtype=F32)[:, None] * freqs)
    pad = -batch % SUBLANES
    co, so = _trig(jnp.pad(positions[:, 0].astype(F32), (0, pad))[:, None] * freqs)
    return cs, sn, co[:batch, None, :], so[:batch, None, :]


def _general_tables(positions):
    cs, sn = _trig(positions.astype(F32).reshape(positions.size, 1) * _lane_freqs())
    return cs, sn, jnp.ones((1, 1, LANES), F32), jnp.zeros((1, 1, LANES), F32)


def _unit_rms(x):
    return x * lax.rsqrt(jnp.mean(x * x, axis=-1, keepdims=True) + EPS)


def _proj_matmul_stage(x_ref, win_ref, wab_ref, wq_ref, wkv_ref, xab_ref, qf_ref, kvf_ref, kpe_ref):
    h = _unit_rms(x_ref[...])
    z = jnp.dot(h.astype(BF16), win_ref[...], preferred_element_type=F32)
    xab_ref[...] = jnp.dot(z[:, :F_WIDTH].astype(BF16), wab_ref[...],
                           preferred_element_type=F32).astype(BF16)
    cqn = _unit_rms(z[:, F_WIDTH:F_WIDTH + Q_LORA]).astype(BF16)
    ckvn = _unit_rms(z[:, F_WIDTH + Q_LORA:F_WIDTH + Q_LORA + KV_LORA]).astype(BF16)
    kpe_ref[...] = z[:, F_WIDTH + Q_LORA + KV_LORA:]
    qf_ref[...] = jnp.dot(cqn, wq_ref[...], preferred_element_type=F32)
    kvf_ref[...] = jnp.dot(ckvn, wkv_ref[...], preferred_element_type=F32)


def _rotary_tile(cs, sn, co, so):
    c = cs * co - sn * so
    s = sn * co + cs * so
    lane = lax.broadcasted_iota(jnp.int32, c.shape, 1)
    cos_t = jnp.where(lane % (2 * HALF) < HALF, c, 0.0)
    sin_t = jnp.where(lane < HALF, -s, jnp.where((lane >= 2 * HALF) & (lane < 3 * HALF), s, 0.0))
    return cos_t, sin_t


def _proj_norm_stage(qf_ref, kvf_ref, kpe_ref, qg_ref, kg_ref, cos_t, sin_t, q_ref, k_ref, v_ref):
    qg = qg_ref[0:1, :]
    qg_nope, qg_rope = qg[:, :LANES], qg[:, LANES:]
    q_const = qg_ref[1:2, LANES:]
    kg_rope = kg_ref[0:1, :]
    k_const = kg_ref[1:2, :]
    eps_sum = QK_DIM * EPS
    k_pe = kpe_ref[...]
    k_rot = _rotate(k_pe * kg_rope, cos_t, sin_t)
    ss_pe = jnp.sum(k_pe * k_pe, axis=-1, keepdims=True) + eps_sum
    for hd in range(N_HEADS):
        qh = qf_ref[:, hd * HEAD_PAD:(hd + 1) * HEAD_PAD]
        rq = lax.rsqrt(jnp.sum(qh * qh, axis=-1, keepdims=True) + eps_sum)
        q_ref[hd, :, :LANES] = (qh[:, :LANES] * rq * qg_nope).astype(BF16)
        q_rot = _rotate(qh[:, LANES:] * rq * qg_rope, cos_t, sin_t)
        q_ref[hd, :, LANES:] = (q_rot + q_const).astype(BF16)
        kn = kvf_ref[:, hd * HEAD_PAD:hd * HEAD_PAD + LANES]
        rk = lax.rsqrt(jnp.sum(kn * kn, axis=-1, keepdims=True) + ss_pe)
        k_ref[hd, :, :LANES] = (kn * rk).astype(BF16)
        k_ref[hd, :, LANES:] = (k_rot * rk + k_const).astype(BF16)
        v_ref[hd] = kvf_ref[:, hd * HEAD_PAD + LANES:(hd + 1) * HEAD_PAD].astype(BF16)


def _proj_kernel(x_ref, win_ref, wab_ref, wq_ref, wkv_ref, qg_ref, kg_ref, cs_ref, sn_ref, co_ref, so_ref,
                 xab_ref, q_ref, k_ref, v_ref, qf0, kvf0, kpe0, qf1, kvf1, kpe1, *, tiles_per_table):
    step = pl.program_id(0)
    tm = x_ref.shape[0]

    @pl.when(step == 0)
    def _():
        qf1[...] = jnp.zeros_like(qf1)
        kvf1[...] = jnp.zeros_like(kvf1)
        kpe1[...] = jnp.zeros_like(kpe1)

    if tiles_per_table > 1:
        first = pl.multiple_of(((step + tiles_per_table - 1) % tiles_per_table) * tm, tm)
        table_rows = pl.ds(first, tm)
    else:
        table_rows = slice(None)

    def body(fill, drain):
        _proj_matmul_stage(x_ref, win_ref, wab_ref, wq_ref, wkv_ref, xab_ref, *fill)
        cos_t, sin_t = _rotary_tile(cs_ref[table_rows, :], sn_ref[table_rows, :], co_ref[0], so_ref[0])
        _proj_norm_stage(*drain, qg_ref, kg_ref, cos_t, sin_t, q_ref, k_ref, v_ref)

    @pl.when(step % 2 == 0)
    def _():
        body((qf0, kvf0, kpe0), (qf1, kvf1, kpe1))

    @pl.when(step % 2 == 1)
    def _():
        body((qf1, kvf1, kpe1), (qf0, kvf0, kpe0))


def _const_spec(shape):
    return pl.BlockSpec(shape, lambda *_: (0,) * len(shape))


def _layer_spec(shape, layer):
    return pl.BlockSpec((None,) + tuple(shape), lambda *_: (layer,) + (0,) * len(shape))


def _project(x2, w_in, w_ab, w_q, w_kv, qg, kg, tables, seq, layer):
    n = x2.shape[0]
    tm = PROJ_TILE
    nt = n // tm
    cs, sn, co, so = tables
    fill = lambda w: pl.BlockSpec((tm, w), lambda i: (jnp.minimum(i, nt - 1), 0))
    head = lambda w: pl.BlockSpec((N_HEADS, tm, w), lambda i: (0, jnp.maximum(i - 1, 0), 0))
    if cs.shape[0] == seq:
        tiles_per_table = seq // tm
        table = _const_spec((seq, LANES))
        offset = pl.BlockSpec((1, 1, LANES), lambda i: (jnp.maximum(i - 1, 0) // tiles_per_table, 0, 0))
    else:
        tiles_per_table = 1
        table = pl.BlockSpec((tm, LANES), lambda i: (jnp.maximum(i - 1, 0), 0))
        offset = _const_spec((1, 1, LANES))
    up_width = N_HEADS * HEAD_PAD
    handoff = [pltpu.VMEM((tm, up_width), F32), pltpu.VMEM((tm, up_width), F32), pltpu.VMEM((tm, LANES), F32)]
    return pl.pallas_call(
        functools.partial(_proj_kernel, tiles_per_table=tiles_per_table),
        grid=(nt + 1,),
        in_specs=[
            fill(D_MODEL), _layer_spec((D_MODEL, Z_WIDTH), layer), _layer_spec((F_WIDTH, 2 * F_WIDTH), layer),
            _layer_spec((Q_LORA, up_width), layer), _layer_spec((KV_LORA, up_width), layer),
            _layer_spec((2, HEAD_PAD), layer), _layer_spec((2, LANES), layer), table, table, offset, offset,
        ],
        out_specs=[fill(2 * F_WIDTH), head(HEAD_PAD), head(HEAD_PAD), head(V_DIM)],
        out_shape=[
            jax.ShapeDtypeStruct((n, 2 * F_WIDTH), BF16),
            jax.ShapeDtypeStruct((N_HEADS, n, HEAD_PAD), BF16),
            jax.ShapeDtypeStruct((N_HEADS, n, HEAD_PAD), BF16),
            jax.ShapeDtypeStruct((N_HEADS, n, V_DIM), BF16),
        ],
        scratch_shapes=handoff + handoff,
        compiler_params=pltpu.CompilerParams(dimension_semantics=("arbitrary",),
                                             vmem_limit_bytes=VMEM_LIMIT),
        name="in_proj",
    )(x2, w_in, w_ab, w_q, w_kv, qg, kg, cs, sn, co, so)


def _fourier_kernel(c_ref, s_ref, flip_ref, lo_ref, hi_ref, g_ref, y_ref):
    half = lo_ref.shape[0]
    nblk = half // FLIP_BLOCK
    flipped = [jnp.dot(flip_ref[...], hi_ref[(nblk - 1 - i) * FLIP_BLOCK:(nblk - i) * FLIP_BLOCK, :],
                       preferred_element_type=F32) for i in range(nblk)]
    rev = pltpu.roll(jnp.concatenate(flipped, axis=0), 1, axis=0)
    lo = lo_ref[...].astype(F32)
    up = (lo[:, :F_WIDTH] + rev[:, :F_WIDTH]).astype(BF16)
    um = (lo[:, F_WIDTH:] - rev[:, F_WIDTH:]).astype(BF16)
    y = (jnp.dot(c_ref[...], up, preferred_element_type=F32)
         + jnp.dot(s_ref[...], um, preferred_element_type=F32))
    mid = rev[0:1, :F_WIDTH]
    odd = (lax.broadcasted_iota(jnp.int32, y.shape, 0) & 1) == 1
    y = y - jnp.where(odd, 2.0 * mid, 0.0)
    y_ref[...] = _rms(y, g_ref[...]).astype(BF16)


def _fourier(xab, cmat, smat_neg, g, seq, layer):
    n = xab.shape[0]
    half = seq // 2
    flip = jnp.asarray(np.eye(FLIP_BLOCK)[::-1], BF16)
    return pl.pallas_call(
        _fourier_kernel,
        grid=(n // seq,),
        in_specs=[
            _const_spec((seq, half)), _const_spec((seq, half)), _const_spec((FLIP_BLOCK, FLIP_BLOCK)),
            pl.BlockSpec((half, 2 * F_WIDTH), lambda b: (2 * b, 0)),
            pl.BlockSpec((half, 2 * F_WIDTH), lambda b: (2 * b + 1, 0)), _layer_spec((1, F_WIDTH), layer),
        ],
        out_specs=pl.BlockSpec((seq, F_WIDTH), lambda b: (b, 0)),
        out_shape=jax.ShapeDtypeStruct((n, F_WIDTH), BF16),
        compiler_params=pltpu.CompilerParams(dimension_semantics=("arbitrary",),
                                             vmem_limit_bytes=VMEM_LIMIT),
        name="seq_dft",
    )(cmat, smat_neg, flip, xab, xab, g)


def _attn_kernel(q_ref, k_ref, v_ref, o_ref, vx_ref, *, heads, row_max):
    seq = q_ref.shape[1]
    for hd in range(heads):
        vx_ref[hd, :, :V_DIM] = v_ref[hd]
        vx_ref[hd, :, V_DIM:] = jnp.ones((seq, V_DIM), BF16)
    for hd in range(heads):
        for j in range(seq // Q_TILE):
            rows = slice(j * Q_TILE, (j + 1) * Q_TILE)
            s = lax.dot_general(q_ref[hd, rows, :], k_ref[hd], (((1,), (1,)), ((), ())),
                                preferred_element_type=F32)
            if row_max:
                s = s - jnp.max(s, axis=-1, keepdims=True)
            p = jnp.exp2(s).astype(BF16)
            ox = jnp.dot(p, vx_ref[hd], preferred_element_type=F32)
            o_ref[rows, hd * V_DIM:(hd + 1) * V_DIM] = (ox[:, :V_DIM] / ox[:, V_DIM:]).astype(BF16)


def _attention(q, k, v, batch, seq, row_max):
    n = q.shape[1]
    g = 1 if row_max else ATTN_HEADS
    return pl.pallas_call(
        functools.partial(_attn_kernel, heads=g, row_max=row_max),
        grid=(batch, N_HEADS // g),
        in_specs=[
            pl.BlockSpec((g, seq, HEAD_PAD), lambda b, h: (h, b, 0)),
            pl.BlockSpec((g, seq, HEAD_PAD), lambda b, h: (h, b, 0)),
            pl.BlockSpec((g, seq, V_DIM), lambda b, h: (h, b, 0)),
        ],
        out_specs=pl.BlockSpec((seq, g * V_DIM), lambda b, h: (b, h)),
        out_shape=jax.ShapeDtypeStruct((n, A_WIDTH), BF16),
        scratch_shapes=[pltpu.VMEM((g, seq, 2 * V_DIM), BF16)],
        compiler_params=pltpu.CompilerParams(
            dimension_semantics=("arbitrary", "arbitrary"),
            vmem_limit_bytes=VMEM_LIMIT),
        name="attention_rowmax" if row_max else "attention",
    )(q, k, v)


def _out_mlp_kernel(x_ref, yf_ref, o_ref, ag_ref, wo_ref, mg_ref, w1_ref, w2_ref, out_ref, act_ref):
    ya = _rms(o_ref[...].astype(F32), ag_ref[...]).astype(BF16)
    x1 = (x_ref[...]
          + jnp.dot(yf_ref[...], wo_ref[:F_WIDTH, :], preferred_element_type=F32)
          + jnp.dot(ya, wo_ref[F_WIDTH:, :], preferred_element_type=F32))
    hn = _rms(x1, mg_ref[...]).astype(BF16)
    for c in range(D_FF // FF_CHUNK):
        cols = slice(c * FF_CHUNK, (c + 1) * FF_CHUNK)
        hm = jnp.dot(hn, w1_ref[:, cols], preferred_element_type=F32)
        act_ref[:, cols] = jnp.square(jnp.maximum(hm, 0.0)).astype(BF16)
    out_ref[...] = x1 + jnp.dot(act_ref[...], w2_ref[...], preferred_element_type=F32)


def _out_mlp(x2, yf, o, ag, w_out, mg, w1, w2, layer):
    n = x2.shape[0]
    tm = ROW_TILE
    row = lambda w: pl.BlockSpec((tm, w), lambda i: (i, 0))
    return pl.pallas_call(
        _out_mlp_kernel,
        grid=(n // tm,),
        in_specs=[
            row(D_MODEL), row(F_WIDTH), row(A_WIDTH), _layer_spec((1, A_WIDTH), layer),
            _layer_spec((D_MODEL, D_MODEL), layer), _layer_spec((1, D_MODEL), layer),
            _layer_spec((D_MODEL, D_FF), layer), _layer_spec((D_FF, D_MODEL), layer),
        ],
        out_specs=row(D_MODEL),
        out_shape=jax.ShapeDtypeStruct((n, D_MODEL), F32),
        scratch_shapes=[pltpu.VMEM((tm, D_FF), BF16)],
        compiler_params=pltpu.CompilerParams(dimension_semantics=("arbitrary",),
                                             vmem_limit_bytes=VMEM_LIMIT),
        name="out_mlp",
    )(x2, yf, o, ag, w_out, mg, w1, w2)


def _dft_matrices(seq):
    lo_n = DFT_SPLIT
    hi_n = seq // lo_n
    half = seq // 2
    n_idx = np.arange(half)[None, :]
    ang_hi = 2.0 * np.pi * ((n_idx * lo_n * np.arange(hi_n)[:, None]) % seq) / seq
    ang_lo = 2.0 * np.pi * ((n_idx * np.arange(lo_n)[:, None]) % seq) / seq
    ch, sh = jnp.asarray(np.cos(ang_hi), F32)[:, None, :], jnp.asarray(np.sin(ang_hi), F32)[:, None, :]
    cl, sl = jnp.asarray(np.cos(ang_lo), F32)[None, :, :], jnp.asarray(np.sin(ang_lo), F32)[None, :, :]
    cmat = (ch * cl - sh * sl).astype(BF16).reshape(seq, half)
    smat_neg = (-(sh * cl + ch * sl)).astype(BF16).reshape(seq, half)
    return cmat, smat_neg


def kernel(x, positions, attn_norm_g, w_in, w_fourier, q_a_g, w_q_up, kv_a_g, w_kv_up, q_norm_g,
           k_norm_g, fourier_out_g, attn_out_g, w_out, mlp_norm_g, w_mlp_in, w_mlp_out):
    batch, seq, _ = x.shape
    depth = w_in.shape[0]
    n = batch * seq

    consecutive = jnp.all(positions == positions[:, :1] + jnp.arange(seq, dtype=positions.dtype)[None, :])
    offset_tables = _offset_tables(positions)
    cmat, smat_neg = _dft_matrices(seq)
    w_ab = _fold_fourier_weights(w_fourier, seq).astype(BF16)

    root = math.sqrt(QK_DIM)
    q_scale = math.log2(math.e) / root
    nope = np.arange(HEAD_PAD) < QK_NOPE

    latent = F_WIDTH + Q_LORA + KV_LORA
    w_in_g = w_in * attn_norm_g[:, :, None]
    w_in_p = jnp.concatenate([w_in_g[..., :latent], _rope_tile(w_in_g, latent)], axis=-1).astype(BF16)
    w_q_p = _head_layout(w_q_up * q_a_g[:, :, None], N_HEADS).astype(BF16)
    w_kv_p = (w_kv_up * kv_a_g[:, :, None]).astype(BF16)
    qng = _head_layout(q_norm_g, 1)
    kng = _head_layout(k_norm_g, 1)
    qg = jnp.where(jnp.asarray(nope), qng * kng * (root * root * q_scale), qng * (root * q_scale))
    kg = kng[:, LANES:] * root
    bound = BOUND_MARGIN * jnp.maximum(
        jnp.max(jnp.abs(qg[:, :LANES]), axis=1),
        jnp.max(jnp.abs(qg[:, LANES:]), axis=1) * jnp.max(jnp.abs(kg), axis=1))
    const_lane = jnp.asarray(np.arange(LANES) == HALF)
    q_const = jnp.where(jnp.asarray(np.arange(HEAD_PAD) == LANES + HALF), -bound[:, None], 0.0)
    k_const = jnp.broadcast_to(jnp.where(const_lane, 1.0, 0.0).astype(F32), kg.shape)
    qg2 = jnp.stack([qg, q_const], axis=1)
    kg2 = jnp.stack([kg, k_const], axis=1)
    fg, ag, mg = fourier_out_g[:, None, :], attn_out_g[:, None, :], mlp_norm_g[:, None, :]

    x2 = x.reshape(n, D_MODEL)
    for l in range(depth):
        project = functools.partial(_project, x2, w_in_p, w_ab, w_q_p, w_kv_p, qg2, kg2, seq=seq, layer=l)
        xab, q, k, v = lax.cond(consecutive,
                                lambda: project(tables=offset_tables),
                                lambda: project(tables=_general_tables(positions)))
        yf = _fourier(xab, cmat, smat_neg, fg, seq, l)
        o = lax.cond(bound[l] <= SHIFT_SAFE,
                     lambda qkv: _attention(*qkv, batch, seq, row_max=False),
                     lambda qkv: _attention(*qkv, batch, seq, row_max=True), (q, k, v))
        x2 = _out_mlp(x2, yf, o, ag, w_out, mg, w_mlp_in, w_mlp_out, l)
    return x2.reshape(batch, seq, D_MODEL)
```

```python
import functools
import math

import numpy as np
import jax
import jax.numpy as jnp
from jax import lax
from jax.experimental import pallas as pl
from jax.experimental.pallas import tpu as pltpu

D_MODEL = 1024
F_GROUPS = 4
F_GROUP_DIM = 64
F_WIDTH = F_GROUPS * F_GROUP_DIM
N_HEADS = 6
Q_LORA = 256
KV_LORA = 256
QK_NOPE = 128
QK_ROPE = 64
V_DIM = 128
QK_DIM = QK_NOPE + QK_ROPE
A_WIDTH = N_HEADS * V_DIM
ROPE_BASE = 10000.0
D_FF = 4 * D_MODEL
EPS = 1e-6

LANES = 128
SUBLANES = 8
HEAD_PAD = 2 * LANES
HALF = QK_ROPE // 2
Z_WIDTH = F_WIDTH + Q_LORA + KV_LORA + LANES
VMEM_LIMIT = 56 * 1024 * 1024

TRIG_TILE = 2048
ROW_TILE = 512
PROJ_TILE = 512
X_RING = 3
FLIP_BLOCK = 256
DFT_SPLIT = 64
Q_TILE = 256
ATTN_HEADS = 3
FF_CHUNK = 1024
SHIFT_SAFE = 60.0
BOUND_MARGIN = 1.02

BF16 = jnp.bfloat16
F32 = jnp.float32


def _rope_tile(w, start):
    zeros = jnp.zeros(w.shape[:-1] + (HALF,), w.dtype)
    return jnp.concatenate([w[..., start:start + HALF], zeros, w[..., start + HALF:start + 2 * HALF], zeros],
                           axis=-1)


def _head_layout(w, heads):
    parts = []
    for hd in range(heads):
        base = hd * QK_DIM
        parts += [w[..., base:base + QK_NOPE], _rope_tile(w, base + QK_NOPE)]
    return jnp.concatenate(parts, axis=-1)


def _rms(x, g):
    return x * lax.rsqrt(jnp.mean(x * x, axis=-1, keepdims=True) + EPS) * g


def _rotate(t, cos_t, sin_t):
    return t * cos_t + pltpu.roll(t, 2 * HALF, axis=1) * sin_t


def _fold_kernel(cc_ref, sc_ref, w_ref, out_ref):
    out_ref[...] = jnp.zeros_like(out_ref)
    for g in range(F_GROUPS):
        w = w_ref[0, g]
        a = jnp.dot(cc_ref[...], w, preferred_element_type=F32, precision=lax.Precision.HIGHEST)
        b = jnp.dot(sc_ref[...], w, preferred_element_type=F32, precision=lax.Precision.HIGHEST)
        lo, hi = g * F_GROUP_DIM, (g + 1) * F_GROUP_DIM
        out_ref[0, lo:hi, lo:hi] = a
        out_ref[0, lo:hi, F_WIDTH + lo:F_WIDTH + hi] = b


def _fold_fourier_weights(w_fourier, seq):
    depth = w_fourier.shape[0]
    c = np.arange(F_GROUP_DIM)
    ang = 2.0 * np.pi * ((c[:, None] * c[None, :]) % F_GROUP_DIM) / F_GROUP_DIM
    ortho = 1.0 / math.sqrt(seq * F_GROUP_DIM)
    cc = jnp.asarray(np.cos(ang) * ortho, F32)
    sc = jnp.asarray(np.sin(ang) * ortho, F32)
    return pl.pallas_call(
        _fold_kernel,
        grid=(depth,),
        in_specs=[
            pl.BlockSpec((F_GROUP_DIM, F_GROUP_DIM), lambda l: (0, 0)),
            pl.BlockSpec((F_GROUP_DIM, F_GROUP_DIM), lambda l: (0, 0)),
            pl.BlockSpec((1, F_GROUPS, F_GROUP_DIM, F_GROUP_DIM), lambda l: (l, 0, 0, 0)),
        ],
        out_specs=pl.BlockSpec((1, F_WIDTH, 2 * F_WIDTH), lambda l: (l, 0, 0)),
        out_shape=jax.ShapeDtypeStruct((depth, F_WIDTH, 2 * F_WIDTH), F32),
        name="fourier_fold",
    )(cc, sc, w_fourier)


def _lane_freqs():
    inv_freq = ROPE_BASE ** (-jnp.arange(HALF, dtype=F32) / HALF)
    return jnp.tile(inv_freq, LANES // HALF)[None, :]


def _plain_trig_kernel(ang_ref, cos_ref, sin_ref):
    a = ang_ref[...]
    cos_ref[...] = jnp.cos(a)
    sin_ref[...] = jnp.sin(a)


def _trig(ang):
    rows = ang.shape[0]
    tile = TRIG_TILE if rows % TRIG_TILE == 0 else rows
    spec = pl.BlockSpec((tile, LANES), lambda i: (i, 0))
    return pl.pallas_call(
        _plain_trig_kernel,
        grid=(rows // tile,),
        in_specs=[spec],
        out_specs=[spec, spec],
        out_shape=[jax.ShapeDtypeStruct(ang.shape, F32)] * 2,
        name="rotary_trig",
    )(ang)


def _offset_tables(positions):
    batch, seq = positions.shape
    freqs = _lane_freqs()
    cs, sn = _trig(jnp.arange(seq, dtype=F32)[:, None] * freqs)
    pad = -batch % SUBLANES
    co, so = _trig(jnp.pad(positions[:, 0].astype(F32), (0, pad))[:, None] * freqs)
    return cs, sn, co[:batch, None, :], so[:batch, None, :]


def _general_tables(positions):
    cs, sn = _trig(positions.astype(F32).reshape(positions.size, 1) * _lane_freqs())
    return cs, sn, jnp.ones((1, 1, LANES), F32), jnp.zeros((1, 1, LANES), F32)


def _unit_rms(x):
    return x * lax.rsqrt(jnp.mean(x * x, axis=-1, keepdims=True) + EPS)


def _proj_matmul_stage(x_ref, win_ref, wab_ref, wq_ref, wkv_ref, xab_ref, qf_ref, kvf_ref, kpe_ref):
    h = _unit_rms(x_ref[...])
    z = jnp.dot(h.astype(BF16), win_ref[...], preferred_element_type=F32)
    xab_ref[...] = jnp.dot(z[:, :F_WIDTH].astype(BF16), wab_ref[...],
                           preferred_element_type=F32).astype(BF16)
    cqn = _unit_rms(z[:, F_WIDTH:F_WIDTH + Q_LORA]).astype(BF16)
    ckvn = _unit_rms(z[:, F_WIDTH + Q_LORA:F_WIDTH + Q_LORA + KV_LORA]).astype(BF16)
    kpe_ref[...] = z[:, F_WIDTH + Q_LORA + KV_LORA:]
    qf_ref[...] = jnp.dot(cqn, wq_ref[...], preferred_element_type=F32)
    kvf_ref[...] = jnp.dot(ckvn, wkv_ref[...], preferred_element_type=F32)


def _rotary_tile(cs, sn, co, so):
    c = cs * co - sn * so
    s = sn * co + cs * so
    lane = lax.broadcasted_iota(jnp.int32, c.shape, 1)
    cos_t = jnp.where(lane % (2 * HALF) < HALF, c, 0.0)
    sin_t = jnp.where(lane < HALF, -s, jnp.where((lane >= 2 * HALF) & (lane < 3 * HALF), s, 0.0))
    return cos_t, sin_t


def _proj_norm_stage(qf_ref, kvf_ref, kpe_ref, qg_ref, kg_ref, cos_t, sin_t, q_ref, k_ref, v_ref):
    qg = qg_ref[0:1, :]
    qg_nope, qg_rope = qg[:, :LANES], qg[:, LANES:]
    q_const = qg_ref[1:2, LANES:]
    kg_rope = kg_ref[0:1, :]
    k_const = kg_ref[1:2, :]
    eps_sum = QK_DIM * EPS
    k_pe = kpe_ref[...]
    k_rot = _rotate(k_pe * kg_rope, cos_t, sin_t)
    ss_pe = jnp.sum(k_pe * k_pe, axis=-1, keepdims=True) + eps_sum
    for hd in range(N_HEADS):
        qh = qf_ref[:, hd * HEAD_PAD:(hd + 1) * HEAD_PAD]
        rq = lax.rsqrt(jnp.sum(qh * qh, axis=-1, keepdims=True) + eps_sum)
        q_ref[hd, :, :LANES] = (qh[:, :LANES] * rq * qg_nope).astype(BF16)
        q_rot = _rotate(qh[:, LANES:] * rq * qg_rope, cos_t, sin_t)
        q_ref[hd, :, LANES:] = (q_rot + q_const).astype(BF16)
        kn = kvf_ref[:, hd * HEAD_PAD:hd * HEAD_PAD + LANES]
        rk = lax.rsqrt(jnp.sum(kn * kn, axis=-1, keepdims=True) + ss_pe)
        k_ref[hd, :, :LANES] = (kn * rk).astype(BF16)
        k_ref[hd, :, LANES:] = (k_rot * rk + k_const).astype(BF16)
        v_ref[hd] = kvf_ref[:, hd * HEAD_PAD + LANES:(hd + 1) * HEAD_PAD].astype(BF16)


def _proj_kernel(x_ref, win_ref, wab_ref, wq_ref, wkv_ref, qg_ref, kg_ref, cs_ref, sn_ref, co_ref, so_ref,
                 xab_ref, q_ref, k_ref, v_ref, qf0, kvf0, kpe0, qf1, kvf1, kpe1, xbuf, xsem, *, tiles_per_table):
    step = pl.program_id(0)
    tiles = pl.num_programs(0) - 1
    tm = xbuf.shape[1]

    def x_copy(tile):
        slot = tile % X_RING
        return pltpu.make_async_copy(x_ref.at[pl.ds(tile * tm, tm), :], xbuf.at[slot], xsem.at[slot])

    @pl.when(step == 0)
    def _():
        qf1[...] = jnp.zeros_like(qf1)
        kvf1[...] = jnp.zeros_like(kvf1)
        kpe1[...] = jnp.zeros_like(kpe1)
        for t in range(X_RING - 1):
            x_copy(t).start()

    @pl.when(step + (X_RING - 1) < tiles)
    def _():
        x_copy(step + (X_RING - 1)).start()

    @pl.when(step < tiles)
    def _():
        x_copy(step).wait()

    x_tile = xbuf.at[jnp.minimum(step, tiles - 1) % X_RING]

    if tiles_per_table > 1:
        first = pl.multiple_of(((step + tiles_per_table - 1) % tiles_per_table) * tm, tm)
        table_rows = pl.ds(first, tm)
    else:
        table_rows = slice(None)

    def body(fill, drain):
        _proj_matmul_stage(x_tile, win_ref, wab_ref, wq_ref, wkv_ref, xab_ref, *fill)
        cos_t, sin_t = _rotary_tile(cs_ref[table_rows, :], sn_ref[table_rows, :], co_ref[0], so_ref[0])
        _proj_norm_stage(*drain, qg_ref, kg_ref, cos_t, sin_t, q_ref, k_ref, v_ref)

    @pl.when(step % 2 == 0)
    def _():
        body((qf0, kvf0, kpe0), (qf1, kvf1, kpe1))

    @pl.when(step % 2 == 1)
    def _():
        body((qf1, kvf1, kpe1), (qf0, kvf0, kpe0))


def _const_spec(shape):
    return pl.BlockSpec(shape, lambda *_: (0,) * len(shape))


def _layer_spec(shape, layer):
    return pl.BlockSpec((None,) + tuple(shape), lambda *_: (layer,) + (0,) * len(shape))


def _project(x2, w_in, w_ab, w_q, w_kv, qg, kg, tables, seq, layer):
    n = x2.shape[0]
    tm = PROJ_TILE
    nt = n // tm
    cs, sn, co, so = tables
    assert nt >= X_RING - 1, "the x ring pre-starts X_RING - 1 tiles"
    fill = lambda w: pl.BlockSpec((tm, w), lambda i: (jnp.minimum(i, nt - 1), 0))
    head = lambda w: pl.BlockSpec((N_HEADS, tm, w), lambda i: (0, jnp.maximum(i - 1, 0), 0))
    if cs.shape[0] == seq:
        tiles_per_table = seq // tm
        table = _const_spec((seq, LANES))
        offset = pl.BlockSpec((1, 1, LANES), lambda i: (jnp.maximum(i - 1, 0) // tiles_per_table, 0, 0))
    else:
        tiles_per_table = 1
        table = pl.BlockSpec((tm, LANES), lambda i: (jnp.maximum(i - 1, 0), 0))
        offset = _const_spec((1, 1, LANES))
    up_width = N_HEADS * HEAD_PAD
    handoff = [pltpu.VMEM((tm, up_width), F32), pltpu.VMEM((tm, up_width), F32), pltpu.VMEM((tm, LANES), F32)]
    return pl.pallas_call(
        functools.partial(_proj_kernel, tiles_per_table=tiles_per_table),
        grid=(nt + 1,),
        in_specs=[
            pl.BlockSpec(memory_space=pl.ANY), _layer_spec((D_MODEL, Z_WIDTH), layer),
            _layer_spec((F_WIDTH, 2 * F_WIDTH), layer), _layer_spec((Q_LORA, up_width), layer), _layer_spec((KV_LORA, up_width), layer),
            _layer_spec((2, HEAD_PAD), layer), _layer_spec((2, LANES), layer), table, table, offset, offset,
        ],
        out_specs=[fill(2 * F_WIDTH), head(HEAD_PAD), head(HEAD_PAD), head(V_DIM)],
        out_shape=[
            jax.ShapeDtypeStruct((n, 2 * F_WIDTH), BF16),
            jax.ShapeDtypeStruct((N_HEADS, n, HEAD_PAD), BF16),
            jax.ShapeDtypeStruct((N_HEADS, n, HEAD_PAD), BF16),
            jax.ShapeDtypeStruct((N_HEADS, n, V_DIM), BF16),
        ],
        scratch_shapes=handoff + handoff + [pltpu.VMEM((X_RING, tm, D_MODEL), F32),
                                            pltpu.SemaphoreType.DMA((X_RING,))],
        compiler_params=pltpu.CompilerParams(dimension_semantics=("arbitrary",),
                                             vmem_limit_bytes=VMEM_LIMIT),
        name="in_proj",
    )(x2, w_in, w_ab, w_q, w_kv, qg, kg, cs, sn, co, so)


def _fourier_kernel(c_ref, s_ref, flip_ref, lo_ref, hi_ref, g_ref, y_ref):
    half = lo_ref.shape[0]
    nblk = half // FLIP_BLOCK
    flipped = [jnp.dot(flip_ref[...], hi_ref[(nblk - 1 - i) * FLIP_BLOCK:(nblk - i) * FLIP_BLOCK, :],
                       preferred_element_type=F32) for i in range(nblk)]
    rev = pltpu.roll(jnp.concatenate(flipped, axis=0), 1, axis=0)
    lo = lo_ref[...].astype(F32)
    up = (lo[:, :F_WIDTH] + rev[:, :F_WIDTH]).astype(BF16)
    um = (lo[:, F_WIDTH:] - rev[:, F_WIDTH:]).astype(BF16)
    y = (jnp.dot(c_ref[...], up, preferred_element_type=F32)
         + jnp.dot(s_ref[...], um, preferred_element_type=F32))
    mid = rev[0:1, :F_WIDTH]
    odd = (lax.broadcasted_iota(jnp.int32, y.shape, 0) & 1) == 1
    y = y - jnp.where(odd, 2.0 * mid, 0.0)
    y_ref[...] = _rms(y, g_ref[...]).astype(BF16)


def _fourier(xab, cmat, smat_neg, g, seq, layer):
    n = xab.shape[0]
    half = seq // 2
    flip = jnp.asarray(np.eye(FLIP_BLOCK)[::-1], BF16)
    return pl.pallas_call(
        _fourier_kernel,
        grid=(n // seq,),
        in_specs=[
            _const_spec((seq, half)), _const_spec((seq, half)), _const_spec((FLIP_BLOCK, FLIP_BLOCK)),
            pl.BlockSpec((half, 2 * F_WIDTH), lambda b: (2 * b, 0)),
            pl.BlockSpec((half, 2 * F_WIDTH), lambda b: (2 * b + 1, 0)), _layer_spec((1, F_WIDTH), layer),
        ],
        out_specs=pl.BlockSpec((seq, F_WIDTH), lambda b: (b, 0)),
        out_shape=jax.ShapeDtypeStruct((n, F_WIDTH), BF16),
        compiler_params=pltpu.CompilerParams(dimension_semantics=("arbitrary",),
                                             vmem_limit_bytes=VMEM_LIMIT),
        name="seq_dft",
    )(cmat, smat_neg, flip, xab, xab, g)


def _attn_kernel(q_ref, k_ref, v_ref, o_ref, vx_ref, *, heads, row_max):
    seq = q_ref.shape[1]
    for hd in range(heads):
        vx_ref[hd, :, :V_DIM] = v_ref[hd]
        vx_ref[hd, :, V_DIM:] = jnp.ones((seq, V_DIM), BF16)
    for hd in range(heads):
        for j in range(seq // Q_TILE):
            rows = slice(j * Q_TILE, (j + 1) * Q_TILE)
            s = lax.dot_general(q_ref[hd, rows, :], k_ref[hd], (((1,), (1,)), ((), ())),
                                preferred_element_type=F32)
            if row_max:
                s = s - jnp.max(s, axis=-1, keepdims=True)
            p = jnp.exp2(s).astype(BF16)
            ox = jnp.dot(p, vx_ref[hd], preferred_element_type=F32)
            o_ref[rows, hd * V_DIM:(hd + 1) * V_DIM] = (ox[:, :V_DIM] / ox[:, V_DIM:]).astype(BF16)


def _attention(q, k, v, batch, seq, row_max):
    n = q.shape[1]
    g = 1 if row_max else ATTN_HEADS
    return pl.pallas_call(
        functools.partial(_attn_kernel, heads=g, row_max=row_max),
        grid=(batch, N_HEADS // g),
        in_specs=[
            pl.BlockSpec((g, seq, HEAD_PAD), lambda b, h: (h, b, 0)),
            pl.BlockSpec((g, seq, HEAD_PAD), lambda b, h: (h, b, 0)),
            pl.BlockSpec((g, seq, V_DIM), lambda b, h: (h, b, 0)),
        ],
        out_specs=pl.BlockSpec((seq, g * V_DIM), lambda b, h: (b, h)),
        out_shape=jax.ShapeDtypeStruct((n, A_WIDTH), BF16),
        scratch_shapes=[pltpu.VMEM((g, seq, 2 * V_DIM), BF16)],
        compiler_params=pltpu.CompilerParams(
            dimension_semantics=("arbitrary", "arbitrary"),
            vmem_limit_bytes=VMEM_LIMIT),
        name="attention_rowmax" if row_max else "attention",
    )(q, k, v)


def _out_mlp_kernel(x_ref, yf_ref, o_ref, ag_ref, wo_ref, mg_ref, w1_ref, w2_ref, out_ref, act_ref):
    ya = _rms(o_ref[...].astype(F32), ag_ref[...]).astype(BF16)
    x1 = (x_ref[...]
          + jnp.dot(yf_ref[...], wo_ref[:F_WIDTH, :], preferred_element_type=F32)
          + jnp.dot(ya, wo_ref[F_WIDTH:, :], preferred_element_type=F32))
    hn = _rms(x1, mg_ref[...]).astype(BF16)
    for c in range(D_FF // FF_CHUNK):
        cols = slice(c * FF_CHUNK, (c + 1) * FF_CHUNK)
        hm = jnp.dot(hn, w1_ref[:, cols], preferred_element_type=F32)
        act_ref[:, cols] = jnp.square(jnp.maximum(hm, 0.0)).astype(BF16)
    out_ref[...] = x1 + jnp.dot(act_ref[...], w2_ref[...], preferred_element_type=F32)


def _out_mlp(x2, yf, o, ag, w_out, mg, w1, w2, layer):
    n = x2.shape[0]
    tm = ROW_TILE
    row = lambda w: pl.BlockSpec((tm, w), lambda i: (i, 0))
    return pl.pallas_call(
        _out_mlp_kernel,
        grid=(n // tm,),
        in_specs=[
            row(D_MODEL), row(F_WIDTH), row(A_WIDTH), _layer_spec((1, A_WIDTH), layer),
            _layer_spec((D_MODEL, D_MODEL), layer), _layer_spec((1, D_MODEL), layer),
            _layer_spec((D_MODEL, D_FF), layer), _layer_spec((D_FF, D_MODEL), layer),
        ],
        out_specs=row(D_MODEL),
        out_shape=jax.ShapeDtypeStruct((n, D_MODEL), F32),
        scratch_shapes=[pltpu.VMEM((tm, D_FF), BF16)],
        compiler_params=pltpu.CompilerParams(dimension_semantics=("arbitrary",),
                                             vmem_limit_bytes=VMEM_LIMIT),
        name="out_mlp",
    )(x2, yf, o, ag, w_out, mg, w1, w2)


def _dft_matrices(seq):
    lo_n = DFT_SPLIT
    hi_n = seq // lo_n
    half = seq // 2
    n_idx = np.arange(half)[None, :]
    ang_hi = 2.0 * np.pi * ((n_idx * lo_n * np.arange(hi_n)[:, None]) % seq) / seq
    ang_lo = 2.0 * np.pi * ((n_idx * np.arange(lo_n)[:, None]) % seq) / seq
    ch, sh = jnp.asarray(np.cos(ang_hi), F32)[:, None, :], jnp.asarray(np.sin(ang_hi), F32)[:, None, :]
    cl, sl = jnp.asarray(np.cos(ang_lo), F32)[None, :, :], jnp.asarray(np.sin(ang_lo), F32)[None, :, :]
    cmat = (ch * cl - sh * sl).astype(BF16).reshape(seq, half)
    smat_neg = (-(sh * cl + ch * sl)).astype(BF16).reshape(seq, half)
    return cmat, smat_neg


def kernel(x, positions, attn_norm_g, w_in, w_fourier, q_a_g, w_q_up, kv_a_g, w_kv_up, q_norm_g,
           k_norm_g, fourier_out_g, attn_out_g, w_out, mlp_norm_g, w_mlp_in, w_mlp_out):
    batch, seq, _ = x.shape
    depth = w_in.shape[0]
    n = batch * seq

    consecutive = jnp.all(positions == positions[:, :1] + jnp.arange(seq, dtype=positions.dtype)[None, :])
    offset_tables = _offset_tables(positions)
    cmat, smat_neg = _dft_matrices(seq)
    w_ab = _fold_fourier_weights(w_fourier, seq).astype(BF16)

    root = math.sqrt(QK_DIM)
    q_scale = math.log2(math.e) / root
    nope = np.arange(HEAD_PAD) < QK_NOPE

    latent = F_WIDTH + Q_LORA + KV_LORA
    w_in_g = w_in * attn_norm_g[:, :, None]
    w_in_p = jnp.concatenate([w_in_g[..., :latent], _rope_tile(w_in_g, latent)], axis=-1).astype(BF16)
    w_q_p = _head_layout(w_q_up * q_a_g[:, :, None], N_HEADS).astype(BF16)
    w_kv_p = (w_kv_up * kv_a_g[:, :, None]).astype(BF16)
    qng = _head_layout(q_norm_g, 1)
    kng = _head_layout(k_norm_g, 1)
    qg = jnp.where(jnp.asarray(nope), qng * kng * (root * root * q_scale), qng * (root * q_scale))
    kg = kng[:, LANES:] * root
    bound = BOUND_MARGIN * jnp.maximum(
        jnp.max(jnp.abs(qg[:, :LANES]), axis=1),
        jnp.max(jnp.abs(qg[:, LANES:]), axis=1) * jnp.max(jnp.abs(kg), axis=1))
    const_lane = jnp.asarray(np.arange(LANES) == HALF)
    q_const = jnp.where(jnp.asarray(np.arange(HEAD_PAD) == LANES + HALF), -bound[:, None], 0.0)
    k_const = jnp.broadcast_to(jnp.where(const_lane, 1.0, 0.0).astype(F32), kg.shape)
    qg2 = jnp.stack([qg, q_const], axis=1)
    kg2 = jnp.stack([kg, k_const], axis=1)
    fg, ag, mg = fourier_out_g[:, None, :], attn_out_g[:, None, :], mlp_norm_g[:, None, :]

    x2 = x.reshape(n, D_MODEL)
    for l in range(depth):
        project = functools.partial(_project, x2, w_in_p, w_ab, w_q_p, w_kv_p, qg2, kg2, seq=seq, layer=l)
        xab, q, k, v = lax.cond(consecutive,
                                lambda: project(tables=offset_tables),
                                lambda: project(tables=_general_tables(positions)))
        yf = _fourier(xab, cmat, smat_neg, fg, seq, l)
        o = lax.cond(bound[l] <= SHIFT_SAFE,
                     lambda qkv: _attention(*qkv, batch, seq, row_max=False),
                     lambda qkv: _attention(*qkv, batch, seq, row_max=True), (q, k, v))
        x2 = _out_mlp(x2, yf, o, ag, w_out, mg, w_mlp_in, w_mlp_out, l)
    return x2.reshape(batch, seq, D_MODEL)
```

```python
import functools
import math

import numpy as np
import jax
import jax.numpy as jnp
from jax import lax
from jax.experimental import pallas as pl
from jax.experimental.pallas import tpu as pltpu

D_MODEL = 1024
F_GROUPS = 4
F_GROUP_DIM = 64
F_WIDTH = F_GROUPS * F_GROUP_DIM
N_HEADS = 6
Q_LORA = 256
KV_LORA = 256
QK_NOPE = 128
QK_ROPE = 64
V_DIM = 128
QK_DIM = QK_NOPE + QK_ROPE
A_WIDTH = N_HEADS * V_DIM
ROPE_BASE = 10000.0
D_FF = 4 * D_MODEL
EPS = 1e-6

LANES = 128
SUBLANES = 8
HEAD_PAD = 2 * LANES
HALF = QK_ROPE // 2
Z_WIDTH = F_WIDTH + Q_LORA + KV_LORA + LANES
VMEM_LIMIT = 56 * 1024 * 1024

TRIG_TILE = 2048
ROW_TILE = 512
PROJ_TILE = 512
X_RING = 3
OUT_RING = 3
FLIP_BLOCK = 256
DFT_SPLIT = 64
Q_TILE = 256
ATTN_HEADS = 3
FF_CHUNK = 1024
SHIFT_SAFE = 60.0
BOUND_MARGIN = 1.02

BF16 = jnp.bfloat16
F32 = jnp.float32


def _rope_tile(w, start):
    zeros = jnp.zeros(w.shape[:-1] + (HALF,), w.dtype)
    return jnp.concatenate([w[..., start:start + HALF], zeros, w[..., start + HALF:start + 2 * HALF], zeros],
                           axis=-1)


def _head_layout(w, heads):
    parts = []
    for hd in range(heads):
        base = hd * QK_DIM
        parts += [w[..., base:base + QK_NOPE], _rope_tile(w, base + QK_NOPE)]
    return jnp.concatenate(parts, axis=-1)


def _rms(x, g):
    return x * lax.rsqrt(jnp.mean(x * x, axis=-1, keepdims=True) + EPS) * g


def _rotate(t, cos_t, sin_t):
    return t * cos_t + pltpu.roll(t, 2 * HALF, axis=1) * sin_t


def _fold_kernel(cc_ref, sc_ref, w_ref, out_ref):
    out_ref[...] = jnp.zeros_like(out_ref)
    for g in range(F_GROUPS):
        w = w_ref[0, g]
        a = jnp.dot(cc_ref[...], w, preferred_element_type=F32, precision=lax.Precision.HIGHEST)
        b = jnp.dot(sc_ref[...], w, preferred_element_type=F32, precision=lax.Precision.HIGHEST)
        lo, hi = g * F_GROUP_DIM, (g + 1) * F_GROUP_DIM
        out_ref[0, lo:hi, lo:hi] = a
        out_ref[0, lo:hi, F_WIDTH + lo:F_WIDTH + hi] = b


def _fold_fourier_weights(w_fourier, seq):
    depth = w_fourier.shape[0]
    c = np.arange(F_GROUP_DIM)
    ang = 2.0 * np.pi * ((c[:, None] * c[None, :]) % F_GROUP_DIM) / F_GROUP_DIM
    ortho = 1.0 / math.sqrt(seq * F_GROUP_DIM)
    cc = jnp.asarray(np.cos(ang) * ortho, F32)
    sc = jnp.asarray(np.sin(ang) * ortho, F32)
    return pl.pallas_call(
        _fold_kernel,
        grid=(depth,),
        in_specs=[
            pl.BlockSpec((F_GROUP_DIM, F_GROUP_DIM), lambda l: (0, 0)),
            pl.BlockSpec((F_GROUP_DIM, F_GROUP_DIM), lambda l: (0, 0)),
            pl.BlockSpec((1, F_GROUPS, F_GROUP_DIM, F_GROUP_DIM), lambda l: (l, 0, 0, 0)),
        ],
        out_specs=pl.BlockSpec((1, F_WIDTH, 2 * F_WIDTH), lambda l: (l, 0, 0)),
        out_shape=jax.ShapeDtypeStruct((depth, F_WIDTH, 2 * F_WIDTH), F32),
        name="fourier_fold",
    )(cc, sc, w_fourier)


def _lane_freqs():
    inv_freq = ROPE_BASE ** (-jnp.arange(HALF, dtype=F32) / HALF)
    return jnp.tile(inv_freq, LANES // HALF)[None, :]


def _plain_trig_kernel(ang_ref, cos_ref, sin_ref):
    a = ang_ref[...]
    cos_ref[...] = jnp.cos(a)
    sin_ref[...] = jnp.sin(a)


def _trig(ang):
    rows = ang.shape[0]
    tile = TRIG_TILE if rows % TRIG_TILE == 0 else rows
    spec = pl.BlockSpec((tile, LANES), lambda i: (i, 0))
    return pl.pallas_call(
        _plain_trig_kernel,
        grid=(rows // tile,),
        in_specs=[spec],
        out_specs=[spec, spec],
        out_shape=[jax.ShapeDtypeStruct(ang.shape, F32)] * 2,
        name="rotary_trig",
    )(ang)


def _offset_tables(positions):
    batch, seq = positions.shape
    freqs = _lane_freqs()
    cs, sn = _trig(jnp.arange(seq, dtype=F32)[:, None] * freqs)
    pad = -batch % SUBLANES
    co, so = _trig(jnp.pad(positions[:, 0].astype(F32), (0, pad))[:, None] * freqs)
    return cs, sn, co[:batch, None, :], so[:batch, None, :]


def _general_tables(positions):
    cs, sn = _trig(positions.astype(F32).reshape(positions.size, 1) * _lane_freqs())
    return cs, sn, jnp.ones((1, 1, LANES), F32), jnp.zeros((1, 1, LANES), F32)


def _unit_rms(x):
    return x * lax.rsqrt(jnp.mean(x * x, axis=-1, keepdims=True) + EPS)


def _proj_matmul_stage(x_ref, win_ref, wab_ref, wq_ref, wkv_ref, xab_ref, qf_ref, kvf_ref, kpe_ref):
    h = _unit_rms(x_ref[...])
    z = jnp.dot(h.astype(BF16), win_ref[...], preferred_element_type=F32)
    xab_ref[...] = jnp.dot(z[:, :F_WIDTH].astype(BF16), wab_ref[...],
                           preferred_element_type=F32).astype(BF16)
    cqn = _unit_rms(z[:, F_WIDTH:F_WIDTH + Q_LORA]).astype(BF16)
    ckvn = _unit_rms(z[:, F_WIDTH + Q_LORA:F_WIDTH + Q_LORA + KV_LORA]).astype(BF16)
    kpe_ref[...] = z[:, F_WIDTH + Q_LORA + KV_LORA:]
    qf_ref[...] = jnp.dot(cqn, wq_ref[...], preferred_element_type=F32)
    kvf_ref[...] = jnp.dot(ckvn, wkv_ref[...], preferred_element_type=F32)


def _rotary_tile(cs, sn, co, so):
    c = cs * co - sn * so
    s = sn * co + cs * so
    lane = lax.broadcasted_iota(jnp.int32, c.shape, 1)
    cos_t = jnp.where(lane % (2 * HALF) < HALF, c, 0.0)
    sin_t = jnp.where(lane < HALF, -s, jnp.where((lane >= 2 * HALF) & (lane < 3 * HALF), s, 0.0))
    return cos_t, sin_t


def _proj_norm_stage(qf_ref, kvf_ref, kpe_ref, qg_ref, kg_ref, cos_t, sin_t, q_ref, k_ref, v_ref):
    qg = qg_ref[0:1, :]
    qg_nope, qg_rope = qg[:, :LANES], qg[:, LANES:]
    q_const = qg_ref[1:2, LANES:]
    kg_rope = kg_ref[0:1, :]
    k_const = kg_ref[1:2, :]
    eps_sum = QK_DIM * EPS
    k_pe = kpe_ref[...]
    k_rot = _rotate(k_pe * kg_rope, cos_t, sin_t)
    ss_pe = jnp.sum(k_pe * k_pe, axis=-1, keepdims=True) + eps_sum
    for hd in range(N_HEADS):
        qh = qf_ref[:, hd * HEAD_PAD:(hd + 1) * HEAD_PAD]
        rq = lax.rsqrt(jnp.sum(qh * qh, axis=-1, keepdims=True) + eps_sum)
        q_ref[hd, :, :LANES] = (qh[:, :LANES] * rq * qg_nope).astype(BF16)
        q_rot = _rotate(qh[:, LANES:] * rq * qg_rope, cos_t, sin_t)
        q_ref[hd, :, LANES:] = (q_rot + q_const).astype(BF16)
        kn = kvf_ref[:, hd * HEAD_PAD:hd * HEAD_PAD + LANES]
        rk = lax.rsqrt(jnp.sum(kn * kn, axis=-1, keepdims=True) + ss_pe)
        k_ref[hd, :, :LANES] = (kn * rk).astype(BF16)
        k_ref[hd, :, LANES:] = (k_rot * rk + k_const).astype(BF16)
        v_ref[hd] = kvf_ref[:, hd * HEAD_PAD + LANES:(hd + 1) * HEAD_PAD].astype(BF16)


def _proj_kernel(x_ref, win_ref, wab_ref, wq_ref, wkv_ref, qg_ref, kg_ref, cs_ref, sn_ref, co_ref, so_ref,
                 xab_ref, q_ref, k_ref, v_ref, qf0, kvf0, kpe0, qf1, kvf1, kpe1, xbuf, xsem,
                 qbuf, kbuf, vbuf, osem, *, tiles_per_table):
    step = pl.program_id(0)
    tiles = pl.num_programs(0) - 1
    tm = xbuf.shape[1]

    def x_copy(tile):
        slot = tile % X_RING
        return pltpu.make_async_copy(x_ref.at[pl.ds(tile * tm, tm), :], xbuf.at[slot], xsem.at[slot])

    @pl.when(step == 0)
    def _():
        qf1[...] = jnp.zeros_like(qf1)
        kvf1[...] = jnp.zeros_like(kvf1)
        kpe1[...] = jnp.zeros_like(kpe1)
        for t in range(X_RING - 1):
            x_copy(t).start()

    @pl.when(step + (X_RING - 1) < tiles)
    def _():
        x_copy(step + (X_RING - 1)).start()

    @pl.when(step < tiles)
    def _():
        x_copy(step).wait()

    x_tile = xbuf.at[jnp.minimum(step, tiles - 1) % X_RING]

    out_slot = (step + OUT_RING - 1) % OUT_RING

    def out_copies(tile):
        slot = tile % OUT_RING
        rows = pl.ds(tile * tm, tm)
        return [pltpu.make_async_copy(buf.at[slot], hbm.at[:, rows, :], osem.at[j, slot])
                for j, (buf, hbm) in enumerate(((qbuf, q_ref), (kbuf, k_ref), (vbuf, v_ref)))]

    @pl.when(step >= OUT_RING + 1)
    def _():
        for cp in out_copies(step - 1 - OUT_RING):
            cp.wait()

    if tiles_per_table > 1:
        first = pl.multiple_of(((step + tiles_per_table - 1) % tiles_per_table) * tm, tm)
        table_rows = pl.ds(first, tm)
    else:
        table_rows = slice(None)

    def body(fill, drain):
        _proj_matmul_stage(x_tile, win_ref, wab_ref, wq_ref, wkv_ref, xab_ref, *fill)
        cos_t, sin_t = _rotary_tile(cs_ref[table_rows, :], sn_ref[table_rows, :], co_ref[0], so_ref[0])
        _proj_norm_stage(*drain, qg_ref, kg_ref, cos_t, sin_t,
                         qbuf.at[out_slot], kbuf.at[out_slot], vbuf.at[out_slot])

    @pl.when(step % 2 == 0)
    def _():
        body((qf0, kvf0, kpe0), (qf1, kvf1, kpe1))

    @pl.when(step % 2 == 1)
    def _():
        body((qf1, kvf1, kpe1), (qf0, kvf0, kpe0))

    @pl.when(step >= 1)
    def _():
        for cp in out_copies(step - 1):
            cp.start()

    @pl.when(step == tiles)
    def _():
        for back in range(OUT_RING):
            for cp in out_copies(tiles - 1 - back):
                cp.wait()


def _const_spec(shape):
    return pl.BlockSpec(shape, lambda *_: (0,) * len(shape))


def _layer_spec(shape, layer):
    return pl.BlockSpec((None,) + tuple(shape), lambda *_: (layer,) + (0,) * len(shape))


def _project(x2, w_in, w_ab, w_q, w_kv, qg, kg, tables, seq, layer):
    n = x2.shape[0]
    tm = PROJ_TILE
    nt = n // tm
    cs, sn, co, so = tables
    assert nt >= max(X_RING - 1, OUT_RING), "ring pre-start / final drain assume at least this many tiles"
    fill = lambda w: pl.BlockSpec((tm, w), lambda i: (jnp.minimum(i, nt - 1), 0))
    if cs.shape[0] == seq:
        tiles_per_table = seq // tm
        table = _const_spec((seq, LANES))
        offset = pl.BlockSpec((1, 1, LANES), lambda i: (jnp.maximum(i - 1, 0) // tiles_per_table, 0, 0))
    else:
        tiles_per_table = 1
        table = pl.BlockSpec((tm, LANES), lambda i: (jnp.maximum(i - 1, 0), 0))
        offset = _const_spec((1, 1, LANES))
    up_width = N_HEADS * HEAD_PAD
    handoff = [pltpu.VMEM((tm, up_width), F32), pltpu.VMEM((tm, up_width), F32), pltpu.VMEM((tm, LANES), F32)]
    return pl.pallas_call(
        functools.partial(_proj_kernel, tiles_per_table=tiles_per_table),
        grid=(nt + 1,),
        in_specs=[
            pl.BlockSpec(memory_space=pl.ANY), _layer_spec((D_MODEL, Z_WIDTH), layer),
            _layer_spec((F_WIDTH, 2 * F_WIDTH), layer), _layer_spec((Q_LORA, up_width), layer), _layer_spec((KV_LORA, up_width), layer),
            _layer_spec((2, HEAD_PAD), layer), _layer_spec((2, LANES), layer), table, table, offset, offset,
        ],
        out_specs=[fill(2 * F_WIDTH)] + [pl.BlockSpec(memory_space=pl.ANY)] * 3,
        out_shape=[
            jax.ShapeDtypeStruct((n, 2 * F_WIDTH), BF16),
            jax.ShapeDtypeStruct((N_HEADS, n, HEAD_PAD), BF16),
            jax.ShapeDtypeStruct((N_HEADS, n, HEAD_PAD), BF16),
            jax.ShapeDtypeStruct((N_HEADS, n, V_DIM), BF16),
        ],
        scratch_shapes=handoff + handoff + [
            pltpu.VMEM((X_RING, tm, D_MODEL), F32), pltpu.SemaphoreType.DMA((X_RING,)),
            pltpu.VMEM((OUT_RING, N_HEADS, tm, HEAD_PAD), BF16), pltpu.VMEM((OUT_RING, N_HEADS, tm, HEAD_PAD), BF16),
            pltpu.VMEM((OUT_RING, N_HEADS, tm, V_DIM), BF16), pltpu.SemaphoreType.DMA((3, OUT_RING)),
        ],
        compiler_params=pltpu.CompilerParams(dimension_semantics=("arbitrary",),
                                             vmem_limit_bytes=VMEM_LIMIT),
        name="in_proj",
    )(x2, w_in, w_ab, w_q, w_kv, qg, kg, cs, sn, co, so)


def _fourier_kernel(c_ref, s_ref, flip_ref, lo_ref, hi_ref, g_ref, y_ref):
    half = lo_ref.shape[0]
    nblk = half // FLIP_BLOCK
    flipped = [jnp.dot(flip_ref[...], hi_ref[(nblk - 1 - i) * FLIP_BLOCK:(nblk - i) * FLIP_BLOCK, :],
                       preferred_element_type=F32) for i in range(nblk)]
    rev = pltpu.roll(jnp.concatenate(flipped, axis=0), 1, axis=0)
    lo = lo_ref[...].astype(F32)
    up = (lo[:, :F_WIDTH] + rev[:, :F_WIDTH]).astype(BF16)
    um = (lo[:, F_WIDTH:] - rev[:, F_WIDTH:]).astype(BF16)
    y = (jnp.dot(c_ref[...], up, preferred_element_type=F32)
         + jnp.dot(s_ref[...], um, preferred_element_type=F32))
    mid = rev[0:1, :F_WIDTH]
    odd = (lax.broadcasted_iota(jnp.int32, y.shape, 0) & 1) == 1
    y = y - jnp.where(odd, 2.0 * mid, 0.0)
    y_ref[...] = _rms(y, g_ref[...]).astype(BF16)


def _fourier(xab, cmat, smat_neg, g, seq, layer):
    n = xab.shape[0]
    half = seq // 2
    flip = jnp.asarray(np.eye(FLIP_BLOCK)[::-1], BF16)
    return pl.pallas_call(
        _fourier_kernel,
        grid=(n // seq,),
        in_specs=[
            _const_spec((seq, half)), _const_spec((seq, half)), _const_spec((FLIP_BLOCK, FLIP_BLOCK)),
            pl.BlockSpec((half, 2 * F_WIDTH), lambda b: (2 * b, 0)),
            pl.BlockSpec((half, 2 * F_WIDTH), lambda b: (2 * b + 1, 0)), _layer_spec((1, F_WIDTH), layer),
        ],
        out_specs=pl.BlockSpec((seq, F_WIDTH), lambda b: (b, 0)),
        out_shape=jax.ShapeDtypeStruct((n, F_WIDTH), BF16),
        compiler_params=pltpu.CompilerParams(dimension_semantics=("arbitrary",),
                                             vmem_limit_bytes=VMEM_LIMIT),
        name="seq_dft",
    )(cmat, smat_neg, flip, xab, xab, g)


def _attn_kernel(q_ref, k_ref, v_ref, o_ref, vx_ref, *, heads, row_max):
    seq = q_ref.shape[1]
    for hd in range(heads):
        vx_ref[hd, :, :V_DIM] = v_ref[hd]
        vx_ref[hd, :, V_DIM:] = jnp.ones((seq, V_DIM), BF16)
    for hd in range(heads):
        for j in range(seq // Q_TILE):
            rows = slice(j * Q_TILE, (j + 1) * Q_TILE)
            s = lax.dot_general(q_ref[hd, rows, :], k_ref[hd], (((1,), (1,)), ((), ())),
                                preferred_element_type=F32)
            if row_max:
                s = s - jnp.max(s, axis=-1, keepdims=True)
            p = jnp.exp2(s).astype(BF16)
            ox = jnp.dot(p, vx_ref[hd], preferred_element_type=F32)
            o_ref[rows, hd * V_DIM:(hd + 1) * V_DIM] = (ox[:, :V_DIM] / ox[:, V_DIM:]).astype(BF16)


def _attention(q, k, v, batch, seq, row_max):
    n = q.shape[1]
    g = 1 if row_max else ATTN_HEADS
    return pl.pallas_call(
        functools.partial(_attn_kernel, heads=g, row_max=row_max),
        grid=(batch, N_HEADS // g),
        in_specs=[
            pl.BlockSpec((g, seq, HEAD_PAD), lambda b, h: (h, b, 0)),
            pl.BlockSpec((g, seq, HEAD_PAD), lambda b, h: (h, b, 0)),
            pl.BlockSpec((g, seq, V_DIM), lambda b, h: (h, b, 0)),
        ],
        out_specs=pl.BlockSpec((seq, g * V_DIM), lambda b, h: (b, h)),
        out_shape=jax.ShapeDtypeStruct((n, A_WIDTH), BF16),
        scratch_shapes=[pltpu.VMEM((g, seq, 2 * V_DIM), BF16)],
        compiler_params=pltpu.CompilerParams(
            dimension_semantics=("arbitrary", "arbitrary"),
            vmem_limit_bytes=VMEM_LIMIT),
        name="attention_rowmax" if row_max else "attention",
    )(q, k, v)


def _out_mlp_kernel(x_ref, yf_ref, o_ref, ag_ref, wo_ref, mg_ref, w1_ref, w2_ref, out_ref, act_ref):
    ya = _rms(o_ref[...].astype(F32), ag_ref[...]).astype(BF16)
    x1 = (x_ref[...]
          + jnp.dot(yf_ref[...], wo_ref[:F_WIDTH, :], preferred_element_type=F32)
          + jnp.dot(ya, wo_ref[F_WIDTH:, :], preferred_element_type=F32))
    hn = _rms(x1, mg_ref[...]).astype(BF16)
    for c in range(D_FF // FF_CHUNK):
        cols = slice(c * FF_CHUNK, (c + 1) * FF_CHUNK)
        hm = jnp.dot(hn, w1_ref[:, cols], preferred_element_type=F32)
        act_ref[:, cols] = jnp.square(jnp.maximum(hm, 0.0)).astype(BF16)
    out_ref[...] = x1 + jnp.dot(act_ref[...], w2_ref[...], preferred_element_type=F32)


def _out_mlp(x2, yf, o, ag, w_out, mg, w1, w2, layer):
    n = x2.shape[0]
    tm = ROW_TILE
    row = lambda w: pl.BlockSpec((tm, w), lambda i: (i, 0))
    return pl.pallas_call(
        _out_mlp_kernel,
        grid=(n // tm,),
        in_specs=[
            row(D_MODEL), row(F_WIDTH), row(A_WIDTH), _layer_spec((1, A_WIDTH), layer),
            _layer_spec((D_MODEL, D_MODEL), layer), _layer_spec((1, D_MODEL), layer),
            _layer_spec((D_MODEL, D_FF), layer), _layer_spec((D_FF, D_MODEL), layer),
        ],
        out_specs=row(D_MODEL),
        out_shape=jax.ShapeDtypeStruct((n, D_MODEL), F32),
        scratch_shapes=[pltpu.VMEM((tm, D_FF), BF16)],
        compiler_params=pltpu.CompilerParams(dimension_semantics=("arbitrary",),
                                             vmem_limit_bytes=VMEM_LIMIT),
        name="out_mlp",
    )(x2, yf, o, ag, w_out, mg, w1, w2)


def _dft_matrices(seq):
    lo_n = DFT_SPLIT
    hi_n = seq // lo_n
    half = seq // 2
    n_idx = np.arange(half)[None, :]
    ang_hi = 2.0 * np.pi * ((n_idx * lo_n * np.arange(hi_n)[:, None]) % seq) / seq
    ang_lo = 2.0 * np.pi * ((n_idx * np.arange(lo_n)[:, None]) % seq) / seq
    ch, sh = jnp.asarray(np.cos(ang_hi), F32)[:, None, :], jnp.asarray(np.sin(ang_hi), F32)[:, None, :]
    cl, sl = jnp.asarray(np.cos(ang_lo), F32)[None, :, :], jnp.asarray(np.sin(ang_lo), F32)[None, :, :]
    cmat = (ch * cl - sh * sl).astype(BF16).reshape(seq, half)
    smat_neg = (-(sh * cl + ch * sl)).astype(BF16).reshape(seq, half)
    return cmat, smat_neg


def kernel(x, positions, attn_norm_g, w_in, w_fourier, q_a_g, w_q_up, kv_a_g, w_kv_up, q_norm_g,
           k_norm_g, fourier_out_g, attn_out_g, w_out, mlp_norm_g, w_mlp_in, w_mlp_out):
    batch, seq, _ = x.shape
    depth = w_in.shape[0]
    n = batch * seq

    consecutive = jnp.all(positions == positions[:, :1] + jnp.arange(seq, dtype=positions.dtype)[None, :])
    offset_tables = _offset_tables(positions)
    cmat, smat_neg = _dft_matrices(seq)
    w_ab = _fold_fourier_weights(w_fourier, seq).astype(BF16)

    root = math.sqrt(QK_DIM)
    q_scale = math.log2(math.e) / root
    nope = np.arange(HEAD_PAD) < QK_NOPE

    latent = F_WIDTH + Q_LORA + KV_LORA
    w_in_g = w_in * attn_norm_g[:, :, None]
    w_in_p = jnp.concatenate([w_in_g[..., :latent], _rope_tile(w_in_g, latent)], axis=-1).astype(BF16)
    w_q_p = _head_layout(w_q_up * q_a_g[:, :, None], N_HEADS).astype(BF16)
    w_kv_p = (w_kv_up * kv_a_g[:, :, None]).astype(BF16)
    qng = _head_layout(q_norm_g, 1)
    kng = _head_layout(k_norm_g, 1)
    qg = jnp.where(jnp.asarray(nope), qng * kng * (root * root * q_scale), qng * (root * q_scale))
    kg = kng[:, LANES:] * root
    bound = BOUND_MARGIN * jnp.maximum(
        jnp.max(jnp.abs(qg[:, :LANES]), axis=1),
        jnp.max(jnp.abs(qg[:, LANES:]), axis=1) * jnp.max(jnp.abs(kg), axis=1))
    const_lane = jnp.asarray(np.arange(LANES) == HALF)
    q_const = jnp.where(jnp.asarray(np.arange(HEAD_PAD) == LANES + HALF), -bound[:, None], 0.0)
    k_const = jnp.broadcast_to(jnp.where(const_lane, 1.0, 0.0).astype(F32), kg.shape)
    qg2 = jnp.stack([qg, q_const], axis=1)
    kg2 = jnp.stack([kg, k_const], axis=1)
    fg, ag, mg = fourier_out_g[:, None, :], attn_out_g[:, None, :], mlp_norm_g[:, None, :]

    x2 = x.reshape(n, D_MODEL)
    for l in range(depth):
        project = functools.partial(_project, x2, w_in_p, w_ab, w_q_p, w_kv_p, qg2, kg2, seq=seq, layer=l)
        xab, q, k, v = lax.cond(consecutive,
                                lambda: project(tables=offset_tables),
                                lambda: project(tables=_general_tables(positions)))
        yf = _fourier(xab, cmat, smat_neg, fg, seq, l)
        o = lax.cond(bound[l] <= SHIFT_SAFE,
                     lambda qkv: _attention(*qkv, batch, seq, row_max=False),
                     lambda qkv: _attention(*qkv, batch, seq, row_max=True), (q, k, v))
        x2 = _out_mlp(x2, yf, o, ag, w_out, mg, w_mlp_in, w_mlp_out, l)
    return x2.reshape(batch, seq, D_MODEL)
```

```python
import functools
import math

import numpy as np
import jax
import jax.numpy as jnp
from jax import lax
from jax.experimental import pallas as pl
from jax.experimental.pallas import tpu as pltpu

D_MODEL = 1024
F_GROUPS = 4
F_GROUP_DIM = 64
F_WIDTH = F_GROUPS * F_GROUP_DIM
N_HEADS = 6
Q_LORA = 256
KV_LORA = 256
QK_NOPE = 128
QK_ROPE = 64
V_DIM = 128
QK_DIM = QK_NOPE + QK_ROPE
A_WIDTH = N_HEADS * V_DIM
ROPE_BASE = 10000.0
D_FF = 4 * D_MODEL
EPS = 1e-6

LANES = 128
SUBLANES = 8
HEAD_PAD = 2 * LANES
HALF = QK_ROPE // 2
Z_WIDTH = F_WIDTH + Q_LORA + KV_LORA + LANES
VMEM_LIMIT = 56 * 1024 * 1024

TRIG_TILE = 2048
ROW_TILE = 512
PROJ_TILE = 512
X_RING = 4
FLIP_BLOCK = 256
DFT_SPLIT = 64
Q_TILE = 256
ATTN_HEADS = 3
FF_CHUNK = 1024
SHIFT_SAFE = 60.0
BOUND_MARGIN = 1.02

BF16 = jnp.bfloat16
F32 = jnp.float32


def _rope_tile(w, start):
    zeros = jnp.zeros(w.shape[:-1] + (HALF,), w.dtype)
    return jnp.concatenate([w[..., start:start + HALF], zeros, w[..., start + HALF:start + 2 * HALF], zeros],
                           axis=-1)


def _head_layout(w, heads):
    parts = []
    for hd in range(heads):
        base = hd * QK_DIM
        parts += [w[..., base:base + QK_NOPE], _rope_tile(w, base + QK_NOPE)]
    return jnp.concatenate(parts, axis=-1)


def _rms(x, g):
    return x * lax.rsqrt(jnp.mean(x * x, axis=-1, keepdims=True) + EPS) * g


def _rotate(t, cos_t, sin_t):
    return t * cos_t + pltpu.roll(t, 2 * HALF, axis=1) * sin_t


def _fold_kernel(cc_ref, sc_ref, w_ref, out_ref):
    out_ref[...] = jnp.zeros_like(out_ref)
    for g in range(F_GROUPS):
        w = w_ref[0, g]
        a = jnp.dot(cc_ref[...], w, preferred_element_type=F32, precision=lax.Precision.HIGHEST)
        b = jnp.dot(sc_ref[...], w, preferred_element_type=F32, precision=lax.Precision.HIGHEST)
        lo, hi = g * F_GROUP_DIM, (g + 1) * F_GROUP_DIM
        out_ref[0, lo:hi, lo:hi] = a
        out_ref[0, lo:hi, F_WIDTH + lo:F_WIDTH + hi] = b


def _fold_fourier_weights(w_fourier, seq):
    depth = w_fourier.shape[0]
    c = np.arange(F_GROUP_DIM)
    ang = 2.0 * np.pi * ((c[:, None] * c[None, :]) % F_GROUP_DIM) / F_GROUP_DIM
    ortho = 1.0 / math.sqrt(seq * F_GROUP_DIM)
    cc = jnp.asarray(np.cos(ang) * ortho, F32)
    sc = jnp.asarray(np.sin(ang) * ortho, F32)
    return pl.pallas_call(
        _fold_kernel,
        grid=(depth,),
        in_specs=[
            pl.BlockSpec((F_GROUP_DIM, F_GROUP_DIM), lambda l: (0, 0)),
            pl.BlockSpec((F_GROUP_DIM, F_GROUP_DIM), lambda l: (0, 0)),
            pl.BlockSpec((1, F_GROUPS, F_GROUP_DIM, F_GROUP_DIM), lambda l: (l, 0, 0, 0)),
        ],
        out_specs=pl.BlockSpec((1, F_WIDTH, 2 * F_WIDTH), lambda l: (l, 0, 0)),
        out_shape=jax.ShapeDtypeStruct((depth, F_WIDTH, 2 * F_WIDTH), F32),
        name="fourier_fold",
    )(cc, sc, w_fourier)


def _lane_freqs():
    inv_freq = ROPE_BASE ** (-jnp.arange(HALF, dtype=F32) / HALF)
    return jnp.tile(inv_freq, LANES // HALF)[None, :]


def _plain_trig_kernel(ang_ref, cos_ref, sin_ref):
    a = ang_ref[...]
    cos_ref[...] = jnp.cos(a)
    sin_ref[...] = jnp.sin(a)


def _trig(ang):
    rows = ang.shape[0]
    tile = TRIG_TILE if rows % TRIG_TILE == 0 else rows
    spec = pl.BlockSpec((tile, LANES), lambda i: (i, 0))
    return pl.pallas_call(
        _plain_trig_kernel,
        grid=(rows // tile,),
        in_specs=[spec],
        out_specs=[spec, spec],
        out_shape=[jax.ShapeDtypeStruct(ang.shape, F32)] * 2,
        name="rotary_trig",
    )(ang)


def _offset_tables(positions):
    batch, seq = positions.shape
    freqs = _lane_freqs()
    cs, sn = _trig(jnp.arange(seq, dtype=F32)[:, None] * freqs)
    pad = -batch % SUBLANES
    co, so = _trig(jnp.pad(positions[:, 0].astype(F32), (0, pad))[:, None] * freqs)
    return cs, sn, co[:batch, None, :], so[:batch, None, :]


def _general_tables(positions):
    cs, sn = _trig(positions.astype(F32).reshape(positions.size, 1) * _lane_freqs())
    return cs, sn, jnp.ones((1, 1, LANES), F32), jnp.zeros((1, 1, LANES), F32)


def _unit_rms(x):
    return x * lax.rsqrt(jnp.mean(x * x, axis=-1, keepdims=True) + EPS)


def _proj_matmul_stage(x_ref, win_ref, wab_ref, wq_ref, wkv_ref, xab_ref, qf_ref, kvf_ref, kpe_ref):
    h = _unit_rms(x_ref[...])
    z = jnp.dot(h.astype(BF16), win_ref[...], preferred_element_type=F32)
    xab_ref[...] = jnp.dot(z[:, :F_WIDTH].astype(BF16), wab_ref[...],
                           preferred_element_type=F32).astype(BF16)
    cqn = _unit_rms(z[:, F_WIDTH:F_WIDTH + Q_LORA]).astype(BF16)
    ckvn = _unit_rms(z[:, F_WIDTH + Q_LORA:F_WIDTH + Q_LORA + KV_LORA]).astype(BF16)
    kpe_ref[...] = z[:, F_WIDTH + Q_LORA + KV_LORA:]
    qf_ref[...] = jnp.dot(cqn, wq_ref[...], preferred_element_type=F32)
    kvf_ref[...] = jnp.dot(ckvn, wkv_ref[...], preferred_element_type=F32)


def _rotary_tile(cs, sn, co, so):
    c = cs * co - sn * so
    s = sn * co + cs * so
    lane = lax.broadcasted_iota(jnp.int32, c.shape, 1)
    cos_t = jnp.where(lane % (2 * HALF) < HALF, c, 0.0)
    sin_t = jnp.where(lane < HALF, -s, jnp.where((lane >= 2 * HALF) & (lane < 3 * HALF), s, 0.0))
    return cos_t, sin_t


def _proj_norm_stage(qf_ref, kvf_ref, kpe_ref, qg_ref, kg_ref, cos_t, sin_t, q_ref, k_ref, v_ref):
    qg = qg_ref[0:1, :]
    qg_nope, qg_rope = qg[:, :LANES], qg[:, LANES:]
    q_const = qg_ref[1:2, LANES:]
    kg_rope = kg_ref[0:1, :]
    k_const = kg_ref[1:2, :]
    eps_sum = QK_DIM * EPS
    k_pe = kpe_ref[...]
    k_rot = _rotate(k_pe * kg_rope, cos_t, sin_t)
    ss_pe = jnp.sum(k_pe * k_pe, axis=-1, keepdims=True) + eps_sum
    for hd in range(N_HEADS):
        qh = qf_ref[:, hd * HEAD_PAD:(hd + 1) * HEAD_PAD]
        rq = lax.rsqrt(jnp.sum(qh * qh, axis=-1, keepdims=True) + eps_sum)
        q_ref[hd, :, :LANES] = (qh[:, :LANES] * rq * qg_nope).astype(BF16)
        q_rot = _rotate(qh[:, LANES:] * rq * qg_rope, cos_t, sin_t)
        q_ref[hd, :, LANES:] = (q_rot + q_const).astype(BF16)
        kn = kvf_ref[:, hd * HEAD_PAD:hd * HEAD_PAD + LANES]
        rk = lax.rsqrt(jnp.sum(kn * kn, axis=-1, keepdims=True) + ss_pe)
        k_ref[hd, :, :LANES] = (kn * rk).astype(BF16)
        k_ref[hd, :, LANES:] = (k_rot * rk + k_const).astype(BF16)
        v_ref[hd] = kvf_ref[:, hd * HEAD_PAD + LANES:(hd + 1) * HEAD_PAD].astype(BF16)


def _proj_kernel(x_ref, win_ref, wab_ref, wq_ref, wkv_ref, qg_ref, kg_ref, cs_ref, sn_ref, co_ref, so_ref,
                 xab_ref, q_ref, k_ref, v_ref, qf0, kvf0, kpe0, qf1, kvf1, kpe1, xbuf, xsem, *, tiles_per_table):
    step = pl.program_id(0)
    tiles = pl.num_programs(0) - 1
    tm = xbuf.shape[1]

    def x_copy(tile):
        slot = tile % X_RING
        return pltpu.make_async_copy(x_ref.at[pl.ds(tile * tm, tm), :], xbuf.at[slot], xsem.at[slot])

    @pl.when(step == 0)
    def _():
        qf1[...] = jnp.zeros_like(qf1)
        kvf1[...] = jnp.zeros_like(kvf1)
        kpe1[...] = jnp.zeros_like(kpe1)
        for t in range(X_RING - 1):
            x_copy(t).start()

    @pl.when(step + (X_RING - 1) < tiles)
    def _():
        x_copy(step + (X_RING - 1)).start()

    @pl.when(step < tiles)
    def _():
        x_copy(step).wait()

    x_tile = xbuf.at[jnp.minimum(step, tiles - 1) % X_RING]

    if tiles_per_table > 1:
        first = pl.multiple_of(((step + tiles_per_table - 1) % tiles_per_table) * tm, tm)
        table_rows = pl.ds(first, tm)
    else:
        table_rows = slice(None)

    def body(fill, drain):
        _proj_matmul_stage(x_tile, win_ref, wab_ref, wq_ref, wkv_ref, xab_ref, *fill)
        cos_t, sin_t = _rotary_tile(cs_ref[table_rows, :], sn_ref[table_rows, :], co_ref[0], so_ref[0])
        _proj_norm_stage(*drain, qg_ref, kg_ref, cos_t, sin_t, q_ref, k_ref, v_ref)

    @pl.when(step % 2 == 0)
    def _():
        body((qf0, kvf0, kpe0), (qf1, kvf1, kpe1))

    @pl.when(step % 2 == 1)
    def _():
        body((qf1, kvf1, kpe1), (qf0, kvf0, kpe0))


def _const_spec(shape):
    return pl.BlockSpec(shape, lambda *_: (0,) * len(shape))


def _layer_spec(shape, layer):
    return pl.BlockSpec((None,) + tuple(shape), lambda *_: (layer,) + (0,) * len(shape))


def _project(x2, w_in, w_ab, w_q, w_kv, qg, kg, tables, seq, layer):
    n = x2.shape[0]
    tm = PROJ_TILE
    nt = n // tm
    cs, sn, co, so = tables
    assert nt >= X_RING - 1, "the x ring pre-starts X_RING - 1 tiles"
    fill = lambda w: pl.BlockSpec((tm, w), lambda i: (jnp.minimum(i, nt - 1), 0))
    head = lambda w: pl.BlockSpec((N_HEADS, tm, w), lambda i: (0, jnp.maximum(i - 1, 0), 0))
    if cs.shape[0] == seq:
        tiles_per_table = seq // tm
        table = _const_spec((seq, LANES))
        offset = pl.BlockSpec((1, 1, LANES), lambda i: (jnp.maximum(i - 1, 0) // tiles_per_table, 0, 0))
    else:
        tiles_per_table = 1
        table = pl.BlockSpec((tm, LANES), lambda i: (jnp.maximum(i - 1, 0), 0))
        offset = _const_spec((1, 1, LANES))
    up_width = N_HEADS * HEAD_PAD
    handoff = [pltpu.VMEM((tm, up_width), F32), pltpu.VMEM((tm, up_width), F32), pltpu.VMEM((tm, LANES), F32)]
    return pl.pallas_call(
        functools.partial(_proj_kernel, tiles_per_table=tiles_per_table),
        grid=(nt + 1,),
        in_specs=[
            pl.BlockSpec(memory_space=pl.ANY), _layer_spec((D_MODEL, Z_WIDTH), layer),
            _layer_spec((F_WIDTH, 2 * F_WIDTH), layer), _layer_spec((Q_LORA, up_width), layer), _layer_spec((KV_LORA, up_width), layer),
            _layer_spec((2, HEAD_PAD), layer), _layer_spec((2, LANES), layer), table, table, offset, offset,
        ],
        out_specs=[fill(2 * F_WIDTH), head(HEAD_PAD), head(HEAD_PAD), head(V_DIM)],
        out_shape=[
            jax.ShapeDtypeStruct((n, 2 * F_WIDTH), BF16),
            jax.ShapeDtypeStruct((N_HEADS, n, HEAD_PAD), BF16),
            jax.ShapeDtypeStruct((N_HEADS, n, HEAD_PAD), BF16),
            jax.ShapeDtypeStruct((N_HEADS, n, V_DIM), BF16),
        ],
        scratch_shapes=handoff + handoff + [pltpu.VMEM((X_RING, tm, D_MODEL), F32),
                                            pltpu.SemaphoreType.DMA((X_RING,))],
        compiler_params=pltpu.CompilerParams(dimension_semantics=("arbitrary",),
                                             vmem_limit_bytes=VMEM_LIMIT),
        name="in_proj",
    )(x2, w_in, w_ab, w_q, w_kv, qg, kg, cs, sn, co, so)


def _fourier_kernel(c_ref, s_ref, flip_ref, lo_ref, hi_ref, g_ref, y_ref):
    half = lo_ref.shape[0]
    nblk = half // FLIP_BLOCK
    flipped = [jnp.dot(flip_ref[...], hi_ref[(nblk - 1 - i) * FLIP_BLOCK:(nblk - i) * FLIP_BLOCK, :],
                       preferred_element_type=F32) for i in range(nblk)]
    rev = pltpu.roll(jnp.concatenate(flipped, axis=0), 1, axis=0)
    lo = lo_ref[...].astype(F32)
    up = (lo[:, :F_WIDTH] + rev[:, :F_WIDTH]).astype(BF16)
    um = (lo[:, F_WIDTH:] - rev[:, F_WIDTH:]).astype(BF16)
    y = (jnp.dot(c_ref[...], up, preferred_element_type=F32)
         + jnp.dot(s_ref[...], um, preferred_element_type=F32))
    mid = rev[0:1, :F_WIDTH]
    odd = (lax.broadcasted_iota(jnp.int32, y.shape, 0) & 1) == 1
    y = y - jnp.where(odd, 2.0 * mid, 0.0)
    y_ref[...] = _rms(y, g_ref[...]).astype(BF16)


def _fourier(xab, cmat, smat_neg, g, seq, layer):
    n = xab.shape[0]
    half = seq // 2
    flip = jnp.asarray(np.eye(FLIP_BLOCK)[::-1], BF16)
    return pl.pallas_call(
        _fourier_kernel,
        grid=(n // seq,),
        in_specs=[
            _const_spec((seq, half)), _const_spec((seq, half)), _const_spec((FLIP_BLOCK, FLIP_BLOCK)),
            pl.BlockSpec((half, 2 * F_WIDTH), lambda b: (2 * b, 0)),
            pl.BlockSpec((half, 2 * F_WIDTH), lambda b: (2 * b + 1, 0)), _layer_spec((1, F_WIDTH), layer),
        ],
        out_specs=pl.BlockSpec((seq, F_WIDTH), lambda b: (b, 0)),
        out_shape=jax.ShapeDtypeStruct((n, F_WIDTH), BF16),
        compiler_params=pltpu.CompilerParams(dimension_semantics=("arbitrary",),
                                             vmem_limit_bytes=VMEM_LIMIT),
        name="seq_dft",
    )(cmat, smat_neg, flip, xab, xab, g)


def _attn_kernel(q_ref, k_ref, v_ref, o_ref, vx_ref, *, heads, row_max):
    seq = q_ref.shape[1]
    for hd in range(heads):
        vx_ref[hd, :, :V_DIM] = v_ref[hd]
        vx_ref[hd, :, V_DIM:] = jnp.ones((seq, V_DIM), BF16)
    for hd in range(heads):
        for j in range(seq // Q_TILE):
            rows = slice(j * Q_TILE, (j + 1) * Q_TILE)
            s = lax.dot_general(q_ref[hd, rows, :], k_ref[hd], (((1,), (1,)), ((), ())),
                                preferred_element_type=F32)
            if row_max:
                s = s - jnp.max(s, axis=-1, keepdims=True)
            p = jnp.exp2(s).astype(BF16)
            ox = jnp.dot(p, vx_ref[hd], preferred_element_type=F32)
            o_ref[rows, hd * V_DIM:(hd + 1) * V_DIM] = (ox[:, :V_DIM] / ox[:, V_DIM:]).astype(BF16)


def _attention(q, k, v, batch, seq, row_max):
    n = q.shape[1]
    g = 1 if row_max else ATTN_HEADS
    return pl.pallas_call(
        functools.partial(_attn_kernel, heads=g, row_max=row_max),
        grid=(batch, N_HEADS // g),
        in_specs=[
            pl.BlockSpec((g, seq, HEAD_PAD), lambda b, h: (h, b, 0)),
            pl.BlockSpec((g, seq, HEAD_PAD), lambda b, h: (h, b, 0)),
            pl.BlockSpec((g, seq, V_DIM), lambda b, h: (h, b, 0)),
        ],
        out_specs=pl.BlockSpec((seq, g * V_DIM), lambda b, h: (b, h)),
        out_shape=jax.ShapeDtypeStruct((n, A_WIDTH), BF16),
        scratch_shapes=[pltpu.VMEM((g, seq, 2 * V_DIM), BF16)],
        compiler_params=pltpu.CompilerParams(
            dimension_semantics=("arbitrary", "arbitrary"),
            vmem_limit_bytes=VMEM_LIMIT),
        name="attention_rowmax" if row_max else "attention",
    )(q, k, v)


def _out_mlp_kernel(x_ref, yf_ref, o_ref, ag_ref, wo_ref, mg_ref, w1_ref, w2_ref, out_ref, act_ref):
    ya = _rms(o_ref[...].astype(F32), ag_ref[...]).astype(BF16)
    x1 = (x_ref[...]
          + jnp.dot(yf_ref[...], wo_ref[:F_WIDTH, :], preferred_element_type=F32)
          + jnp.dot(ya, wo_ref[F_WIDTH:, :], preferred_element_type=F32))
    hn = _rms(x1, mg_ref[...]).astype(BF16)
    for c in range(D_FF // FF_CHUNK):
        cols = slice(c * FF_CHUNK, (c + 1) * FF_CHUNK)
        hm = jnp.dot(hn, w1_ref[:, cols], preferred_element_type=F32)
        act_ref[:, cols] = jnp.square(jnp.maximum(hm, 0.0)).astype(BF16)
    out_ref[...] = x1 + jnp.dot(act_ref[...], w2_ref[...], preferred_element_type=F32)


def _out_mlp(x2, yf, o, ag, w_out, mg, w1, w2, layer):
    n = x2.shape[0]
    tm = ROW_TILE
    row = lambda w: pl.BlockSpec((tm, w), lambda i: (i, 0))
    return pl.pallas_call(
        _out_mlp_kernel,
        grid=(n // tm,),
        in_specs=[
            row(D_MODEL), row(F_WIDTH), row(A_WIDTH), _layer_spec((1, A_WIDTH), layer),
            _layer_spec((D_MODEL, D_MODEL), layer), _layer_spec((1, D_MODEL), layer),
            _layer_spec((D_MODEL, D_FF), layer), _layer_spec((D_FF, D_MODEL), layer),
        ],
        out_specs=row(D_MODEL),
        out_shape=jax.ShapeDtypeStruct((n, D_MODEL), F32),
        scratch_shapes=[pltpu.VMEM((tm, D_FF), BF16)],
        compiler_params=pltpu.CompilerParams(dimension_semantics=("arbitrary",),
                                             vmem_limit_bytes=VMEM_LIMIT),
        name="out_mlp",
    )(x2, yf, o, ag, w_out, mg, w1, w2)


def _dft_matrices(seq):
    lo_n = DFT_SPLIT
    hi_n = seq // lo_n
    half = seq // 2
    n_idx = np.arange(half)[None, :]
    ang_hi = 2.0 * np.pi * ((n_idx * lo_n * np.arange(hi_n)[:, None]) % seq) / seq
    ang_lo = 2.0 * np.pi * ((n_idx * np.arange(lo_n)[:, None]) % seq) / seq
    ch, sh = jnp.asarray(np.cos(ang_hi), F32)[:, None, :], jnp.asarray(np.sin(ang_hi), F32)[:, None, :]
    cl, sl = jnp.asarray(np.cos(ang_lo), F32)[None, :, :], jnp.asarray(np.sin(ang_lo), F32)[None, :, :]
    cmat = (ch * cl - sh * sl).astype(BF16).reshape(seq, half)
    smat_neg = (-(sh * cl + ch * sl)).astype(BF16).reshape(seq, half)
    return cmat, smat_neg


def kernel(x, positions, attn_norm_g, w_in, w_fourier, q_a_g, w_q_up, kv_a_g, w_kv_up, q_norm_g,
           k_norm_g, fourier_out_g, attn_out_g, w_out, mlp_norm_g, w_mlp_in, w_mlp_out):
    batch, seq, _ = x.shape
    depth = w_in.shape[0]
    n = batch * seq

    consecutive = jnp.all(positions == positions[:, :1] + jnp.arange(seq, dtype=positions.dtype)[None, :])
    offset_tables = _offset_tables(positions)
    cmat, smat_neg = _dft_matrices(seq)
    w_ab = _fold_fourier_weights(w_fourier, seq).astype(BF16)

    root = math.sqrt(QK_DIM)
    q_scale = math.log2(math.e) / root
    nope = np.arange(HEAD_PAD) < QK_NOPE

    latent = F_WIDTH + Q_LORA + KV_LORA
    w_in_g = w_in * attn_norm_g[:, :, None]
    w_in_p = jnp.concatenate([w_in_g[..., :latent], _rope_tile(w_in_g, latent)], axis=-1).astype(BF16)
    w_q_p = _head_layout(w_q_up * q_a_g[:, :, None], N_HEADS).astype(BF16)
    w_kv_p = (w_kv_up * kv_a_g[:, :, None]).astype(BF16)
    qng = _head_layout(q_norm_g, 1)
    kng = _head_layout(k_norm_g, 1)
    qg = jnp.where(jnp.asarray(nope), qng * kng * (root * root * q_scale), qng * (root * q_scale))
    kg = kng[:, LANES:] * root
    bound = BOUND_MARGIN * jnp.maximum(
        jnp.max(jnp.abs(qg[:, :LANES]), axis=1),
        jnp.max(jnp.abs(qg[:, LANES:]), axis=1) * jnp.max(jnp.abs(kg), axis=1))
    const_lane = jnp.asarray(np.arange(LANES) == HALF)
    q_const = jnp.where(jnp.asarray(np.arange(HEAD_PAD) == LANES + HALF), -bound[:, None], 0.0)
    k_const = jnp.broadcast_to(jnp.where(const_lane, 1.0, 0.0).astype(F32), kg.shape)
    qg2 = jnp.stack([qg, q_const], axis=1)
    kg2 = jnp.stack([kg, k_const], axis=1)
    fg, ag, mg = fourier_out_g[:, None, :], attn_out_g[:, None, :], mlp_norm_g[:, None, :]

    x2 = x.reshape(n, D_MODEL)
    for l in range(depth):
        project = functools.partial(_project, x2, w_in_p, w_ab, w_q_p, w_kv_p, qg2, kg2, seq=seq, layer=l)
        xab, q, k, v = lax.cond(consecutive,
                                lambda: project(tables=offset_tables),
                                lambda: project(tables=_general_tables(positions)))
        yf = _fourier(xab, cmat, smat_neg, fg, seq, l)
        o = lax.cond(bound[l] <= SHIFT_SAFE,
                     lambda qkv: _attention(*qkv, batch, seq, row_max=False),
                     lambda qkv: _attention(*qkv, batch, seq, row_max=True), (q, k, v))
        x2 = _out_mlp(x2, yf, o, ag, w_out, mg, w_mlp_in, w_mlp_out, l)
    return x2.reshape(batch, seq, D_MODEL)
```
